```python
import math
import jax
import jax.numpy as jnp
from jax import lax
import numpy as np

D_MODEL = 1024
BATCH = 8
SEQ = 8192
DEPTH = 2

EPS = 1e-6
ROPE_THETA = 10000.0
D_FF = 2816
GDN_HEADS = 4
GDN_DK = 128
GDN_DV = 128
GDN_CONV = 4
GDN_CHUNK = 64
RET_HEADS = 4
RET_DK = 64
RET_DV = 128
RET_CHUNK = 64
DIL_HEADS = 4
DIL_DH = 128
DIL_PATTERNS = ((128, 1), (512, 4), (2048, 16))
DIFF_HEADS = 4
DIFF_DK = 64
DIFF_DV = 128
Q_BLOCK = 128

GDN_QK_W = GDN_HEADS * GDN_DK
GDN_V_W = GDN_HEADS * GDN_DV
GDN_CONV_W = 2 * GDN_QK_W + GDN_V_W
RET_QK_W = RET_HEADS * RET_DK
RET_V_W = RET_HEADS * RET_DV
IN0_SPLITS = (GDN_CONV_W, GDN_V_W, GDN_HEADS, GDN_HEADS, RET_QK_W, RET_QK_W, RET_V_W, RET_V_W)
IN0_W = sum(IN0_SPLITS)
MIX0_W = GDN_V_W + RET_V_W
DIL_W = DIL_HEADS * DIL_DH
DIFF_QK_W = DIFF_HEADS * 2 * DIFF_DK
DIFF_V_W = DIFF_HEADS * DIFF_DV
IN1_SPLITS = (DIL_W, DIL_W, DIL_W, DIFF_QK_W, DIFF_QK_W, DIFF_V_W)
IN1_W = sum(IN1_SPLITS)
MIX1_W = DIL_W + DIFF_V_W

kernel_name = 'hybrid_delta_retention_dilated_diff'


def rmsnorm(x, w=None):
    xf = x.astype(jnp.float32)
    y = xf * lax.rsqrt(jnp.mean(xf * xf, axis=-1, keepdims=True) + EPS)
    if w is not None:
        y = y * w.astype(jnp.float32)
    return y.astype(x.dtype)


def l2norm(t):
    tf = t.astype(jnp.float32)
    return (tf * lax.rsqrt(jnp.sum(tf * tf, axis=-1, keepdims=True) + EPS)).astype(t.dtype)


def split_cols(t, sizes):
    offs = np.cumsum(sizes)[:-1].tolist()
    return jnp.split(t, offs, axis=-1)


def swiglu(h, w_up, w_down):
    gate, up = jnp.split(h @ w_up, 2, axis=-1)
    return (jax.nn.silu(gate) * up) @ w_down


def rope_tables(seq, dim):
    inv = ROPE_THETA ** (-jnp.arange(0, dim, 2, dtype=jnp.float32) / dim)
    ang = jnp.arange(seq, dtype=jnp.float32)[:, None] * inv[None, :]
    return jnp.cos(ang), jnp.sin(ang)


def apply_rope(x, cos, sin):
    x1, x2 = jnp.split(x, 2, axis=-1)
    c = cos[None, :, None, :].astype(x.dtype)
    s = sin[None, :, None, :].astype(x.dtype)
    return jnp.concatenate([x1 * c - x2 * s, x1 * s + x2 * c], axis=-1)


def causal_depthwise_conv(x, w):
    K, C = w.shape
    return lax.conv_general_dilated(x, w[:, None, :].astype(x.dtype), window_strides=(1,),
                                    padding=[(K - 1, 0)], dimension_numbers=('NWC', 'WIO', 'NWC'),
                                    feature_group_count=C)


def to_chunks(t, c):
    B, S = t.shape[:2]
    t = t.reshape(B, S // c, c, *t.shape[2:])
    return jnp.swapaxes(t, 2, 3)


def from_chunks(t):
    t = jnp.swapaxes(t, 2, 3)
    return t.reshape(t.shape[0], t.shape[1] * t.shape[2], *t.shape[3:])


def gated_delta_rule_chunked(q, k, v, g, beta):
    B, S, H, dk = q.shape
    dv = v.shape[-1]
    C = GDN_CHUNK
    f32 = jnp.float32
    qc, kc, vc = (to_chunks(t.astype(f32), C) for t in (q, k, v))
    gc, bc = (to_chunks(t.astype(f32), C) for t in (g, beta))
    gcum = jnp.cumsum(gc, axis=-1)
    idx = jnp.arange(C)
    causal = idx[:, None] >= idx[None, :]
    strict = idx[:, None] > idx[None, :]
    gdiff = gcum[..., :, None] - gcum[..., None, :]
    decay = jnp.where(causal, jnp.exp(jnp.where(causal, gdiff, 0.0)), 0.0)
    a_strict = jnp.where(strict, jnp.einsum('bnhid,bnhjd->bnhij', kc, kc) * decay * bc[..., :, None], 0.0)
    rhs = jnp.concatenate([vc * bc[..., None], kc * (bc * jnp.exp(gcum))[..., None]], axis=-1)
    sol = lax.linalg.triangular_solve(a_strict + jnp.eye(C, dtype=f32), rhs, left_side=True,
                                      lower=True, unit_diagonal=True)
    u, w = sol[..., :dv], sol[..., dv:]
    qk = jnp.einsum('bnhid,bnhjd->bnhij', qc, kc) * decay
    q_dec = qc * jnp.exp(gcum)[..., None]
    g_last = gcum[..., -1]
    k_dec = kc * jnp.exp(g_last[..., None] - gcum)[..., None]

    def step(state, xs):
        qk_n, q_n, w_n, u_n, k_n, gl_n = xs
        v_new = u_n - jnp.einsum('bhcd,bhde->bhce', w_n, state)
        o_n = jnp.einsum('bhcd,bhde->bhce', q_n, state) + jnp.einsum('bhij,bhje->bhie', qk_n, v_new)
        state = state * jnp.exp(gl_n)[..., None, None] + jnp.einsum('bhcd,bhce->bhde', k_n, v_new)
        return state, o_n

    xs = tuple(jnp.moveaxis(t, 1, 0) for t in (qk, q_dec, w, u, k_dec, g_last))
    _, o = lax.scan(step, jnp.zeros((B, H, dk, dv), f32), xs)
    return from_chunks(jnp.moveaxis(o, 0, 1)).astype(v.dtype)


def retention_chunked(q, k, v):
    B, S, H, dk = q.shape
    dv = v.shape[-1]
    C = RET_CHUNK
    f32 = jnp.float32
    log_gamma = jnp.log(1.0 - 2.0 ** (-5.0 - jnp.arange(H, dtype=f32)))
    qc, kc, vc = (to_chunks(t.astype(f32), C) for t in (q, k, v))
    idx = jnp.arange(C, dtype=f32)
    rel = idx[:, None] - idx[None, :]
    causal = rel >= 0
    dmask = jnp.where(causal, jnp.exp(jnp.where(causal, rel, 0.0)[None] * log_gamma[:, None, None]), 0.0)
    intra = jnp.einsum('bnhij,bnhje->bnhie', jnp.einsum('bnhid,bnhjd->bnhij', qc, kc) * dmask, vc)
    k_scale = jnp.exp((C - 1 - idx)[None, :] * log_gamma[:, None])
    q_scale = jnp.exp((idx + 1)[None, :] * log_gamma[:, None])
    chunk_decay = jnp.exp(C * log_gamma)
    kv = jnp.einsum('bnhcd,bnhce->bnhde', kc * k_scale[..., None], vc)

    def step(state, kv_n):
        return state * chunk_decay[:, None, None] + kv_n, state

    _, prev = lax.scan(step, jnp.zeros((B, H, dk, dv), f32), jnp.moveaxis(kv, 1, 0))
    prev = jnp.moveaxis(prev, 0, 1)
    inter = jnp.einsum('bnhcd,bnhde->bnhce', qc, prev) * q_scale[..., None]
    return from_chunks(intra + inter).astype(v.dtype)


def dilated_branch(q, k, v, window, dilation):
    B, S, H, d = q.shape
    L = S // dilation
    n_keys = window // dilation
    blk = min(n_keys, L)
    nb = -(-L // blk)
    Lp = nb * blk

    def strided(t):
        t = t.reshape(B, L, dilation, H, d).swapaxes(1, 2)
        t = jnp.pad(t, ((0, 0), (0, 0), (0, Lp - L), (0, 0), (0, 0)))
        return t.reshape(B, dilation, nb, blk, H, d)

    def with_prev(t):
        prev = jnp.pad(t, ((0, 0), (0, 0), (1, 0), (0, 0), (0, 0), (0, 0)))[:, :, :-1]
        return jnp.concatenate([prev, t], axis=3)

    qs = strided(q)
    kb, vb = with_prev(strided(k)), with_prev(strided(v))
    s = jnp.einsum('brnqhd,brnkhd->brnhqk', qs, kb).astype(jnp.float32)
    qi = jnp.arange(blk)[:, None] + blk
    kj = jnp.arange(2 * blk)[None, :]
    dist = qi - kj
    band = (dist >= 0) & (dist <= n_keys)
    has_prev = (jnp.arange(nb) > 0)[:, None, None] | (kj >= blk)[None]
    valid = band[None] & has_prev
    s = jnp.where(valid[:, None], s, -jnp.inf)
    m = jnp.max(s, axis=-1)
    p = jnp.exp(s - m[..., None])
    l = jnp.sum(p, axis=-1)
    o = jnp.einsum('brnhqk,brnkhd->brnqhd', p.astype(v.dtype), vb).astype(jnp.float32)

    def unstrided(t):
        t = t.reshape(B, dilation, Lp, *t.shape[4:])[:, :, :L]
        return t.swapaxes(1, 2).reshape(B, S, *t.shape[3:])

    return unstrided(o), unstrided(m.swapaxes(3, 4)), unstrided(l.swapaxes(3, 4))


def dilated_attention(q, k, v):
    outs = [dilated_branch(q, k, v, w, r) for (w, r) in DIL_PATTERNS]
    m_all = jnp.max(jnp.stack([o[1] for o in outs]), axis=0)
    num = jnp.zeros_like(outs[0][0])
    den = jnp.zeros_like(outs[0][2])
    for o_g, m_g, l_g in outs:
        scale = jnp.exp(m_g - m_all)
        num = num + o_g * scale[..., None]
        den = den + l_g * scale
    return (num / den[..., None]).astype(v.dtype)


def differential_attention(q1, q2, k1, k2, v, lam):
    B, S, H, dk = q1.shape
    nq = S // Q_BLOCK

    def blocks(t):
        return jnp.moveaxis(t.reshape(B, nq, Q_BLOCK, H, t.shape[-1]), 1, 0)

    key_pos = jnp.arange(S)

    def one_block(args):
        i, qa, qb = args
        q_pos = i * Q_BLOCK + jnp.arange(Q_BLOCK)
        causal = (q_pos[:, None] >= key_pos[None, :])[None, None]
        s1 = jnp.where(causal, jnp.einsum('bqhd,bkhd->bhqk', qa, k1).astype(jnp.float32), -jnp.inf)
        s2 = jnp.where(causal, jnp.einsum('bqhd,bkhd->bhqk', qb, k2).astype(jnp.float32), -jnp.inf)
        attn = jax.nn.softmax(s1, axis=-1) - lam * jax.nn.softmax(s2, axis=-1)
        return jnp.einsum('bhqk,bkhe->bqhe', attn.astype(v.dtype), v)

    o = lax.map(one_block, (jnp.arange(nq), blocks(q1), blocks(q2)))
    return jnp.moveaxis(o, 0, 1).reshape(B, S, H, v.shape[-1])


def mixer_delta_retention(h, w_in, conv_w, a_log, dt_bias, gdn_norm, w_out, cos, sin):
    B, S, _ = h.shape
    qkv, z, b, a, rq, rk, rv, rg = split_cols(h @ w_in, IN0_SPLITS)
    qkv = jax.nn.silu(causal_depthwise_conv(qkv, conv_w))
    q, k, v = split_cols(qkv, (GDN_QK_W, GDN_QK_W, GDN_V_W))
    q = l2norm(q.reshape(B, S, GDN_HEADS, GDN_DK)) * GDN_DK ** -0.5
    k = l2norm(k.reshape(B, S, GDN_HEADS, GDN_DK))
    v = v.reshape(B, S, GDN_HEADS, GDN_DV)
    beta = jax.nn.sigmoid(b)
    g = -jnp.exp(a_log) * jax.nn.softplus(a + dt_bias)
    o_a = gated_delta_rule_chunked(q, k, v, g, beta)
    o_a = rmsnorm(o_a, gdn_norm) * jax.nn.silu(z.reshape(B, S, GDN_HEADS, GDN_DV))
    rq = apply_rope(rq.reshape(B, S, RET_HEADS, RET_DK), cos, sin) * RET_DK ** -0.5
    rk = apply_rope(rk.reshape(B, S, RET_HEADS, RET_DK), cos, sin)
    o_b = retention_chunked(rq, rk, rv.reshape(B, S, RET_HEADS, RET_DV))
    o_b = rmsnorm(o_b) * jax.nn.silu(rg.reshape(B, S, RET_HEADS, RET_DV))
    o = jnp.concatenate([o_a.reshape(B, S, GDN_V_W), o_b.reshape(B, S, RET_V_W)], axis=-1)
    return o @ w_out


def mixer_dilated_differential(h, w_in, lambda_q1, lambda_k1, lambda_q2, lambda_k2, diff_norm, w_out,
                               cos128, sin128, cos64, sin64, lambda_init):
    B, S, _ = h.shape
    cq, ck, cv, dq, dk, dv = split_cols(h @ w_in, IN1_SPLITS)
    cq = apply_rope(cq.reshape(B, S, DIL_HEADS, DIL_DH), cos128, sin128) * DIL_DH ** -0.5
    ck = apply_rope(ck.reshape(B, S, DIL_HEADS, DIL_DH), cos128, sin128)
    o_c = dilated_attention(cq, ck, cv.reshape(B, S, DIL_HEADS, DIL_DH))
    dq = dq.reshape(B, S, DIFF_HEADS, 2, DIFF_DK)
    dk = dk.reshape(B, S, DIFF_HEADS, 2, DIFF_DK)
    q1 = apply_rope(dq[..., 0, :], cos64, sin64) * DIFF_DK ** -0.5
    q2 = apply_rope(dq[..., 1, :], cos64, sin64) * DIFF_DK ** -0.5
    k1 = apply_rope(dk[..., 0, :], cos64, sin64)
    k2 = apply_rope(dk[..., 1, :], cos64, sin64)
    lam = (jnp.exp(jnp.sum(lambda_q1 * lambda_k1).astype(jnp.float32))
           - jnp.exp(jnp.sum(lambda_q2 * lambda_k2).astype(jnp.float32)) + lambda_init)
    o_d = differential_attention(q1, q2, k1, k2, dv.reshape(B, S, DIFF_HEADS, DIFF_DV), lam)
    o_d = rmsnorm(o_d, diff_norm) * (1.0 - lambda_init)
    o = jnp.concatenate([o_c.reshape(B, S, DIL_W), o_d.reshape(B, S, DIFF_V_W)], axis=-1)
    return o @ w_out


def setup_inputs(seed: int = 0) -> dict:
    key = jax.random.key(seed)
    ks = iter(jax.random.split(key, 64))

    def nrm(shape, scale):
        return jax.random.normal(next(ks), shape, jnp.float32) * scale

    def gain(n):
        return 1.0 + 0.02 * jax.random.normal(next(ks), (n,), jnp.float32)

    inp = {}
    inp['x'] = nrm((BATCH, SEQ, D_MODEL), 1.0)

    def ffn(prefix):
        inp[prefix + '_norm'] = gain(D_MODEL)
        inp[prefix + '_w_up'] = nrm((D_MODEL, 2 * D_FF), D_MODEL ** -0.5)
        inp[prefix + '_w_down'] = nrm((D_FF, D_MODEL), D_FF ** -0.5)

    ffn('l0_ffn1')
    inp['l0_mix_norm'] = gain(D_MODEL)
    inp['l0_w_in'] = nrm((D_MODEL, IN0_W), D_MODEL ** -0.5)
    inp['l0_conv_w'] = nrm((GDN_CONV, GDN_CONV_W), GDN_CONV ** -0.5)
    inp['l0_a_log'] = jnp.log(jax.random.uniform(next(ks), (GDN_HEADS,), jnp.float32, 1.0, 16.0))
    dt = jnp.exp(jax.random.uniform(next(ks), (GDN_HEADS,), jnp.float32, math.log(1e-3), math.log(1e-1)))
    inp['l0_dt_bias'] = dt + jnp.log(-jnp.expm1(-dt))
    inp['l0_gdn_norm'] = gain(GDN_DV)
    inp['l0_w_out'] = nrm((MIX0_W, D_MODEL), MIX0_W ** -0.5)
    ffn('l0_ffn2')
    ffn('l1_ffn1')
    inp['l1_mix_norm'] = gain(D_MODEL)
    inp['l1_w_in'] = nrm((D_MODEL, IN1_W), D_MODEL ** -0.5)
    inp['l1_lambda_q1'] = nrm((DIFF_DK,), 0.1)
    inp['l1_lambda_k1'] = nrm((DIFF_DK,), 0.1)
    inp['l1_lambda_q2'] = nrm((DIFF_DK,), 0.1)
    inp['l1_lambda_k2'] = nrm((DIFF_DK,), 0.1)
    inp['l1_diff_norm'] = gain(DIFF_DV)
    inp['l1_w_out'] = nrm((MIX1_W, D_MODEL), MIX1_W ** -0.5)
    ffn('l1_ffn2')
    inp['final_norm'] = gain(D_MODEL)
    return inp


def reference(x, l0_ffn1_norm, l0_ffn1_w_up, l0_ffn1_w_down, l0_mix_norm, l0_w_in, l0_conv_w, l0_a_log,
              l0_dt_bias, l0_gdn_norm, l0_w_out, l0_ffn2_norm, l0_ffn2_w_up, l0_ffn2_w_down,
              l1_ffn1_norm, l1_ffn1_w_up, l1_ffn1_w_down, l1_mix_norm, l1_w_in, l1_lambda_q1, l1_lambda_k1,
              l1_lambda_q2, l1_lambda_k2, l1_diff_norm, l1_w_out, l1_ffn2_norm, l1_ffn2_w_up, l1_ffn2_w_down,
              final_norm):
    S = x.shape[1]
    cos64, sin64 = rope_tables(S, 64)
    cos128, sin128 = rope_tables(S, 128)
    ffn1 = ((l0_ffn1_norm, l0_ffn1_w_up, l0_ffn1_w_down), (l1_ffn1_norm, l1_ffn1_w_up, l1_ffn1_w_down))
    ffn2 = ((l0_ffn2_norm, l0_ffn2_w_up, l0_ffn2_w_down), (l1_ffn2_norm, l1_ffn2_w_up, l1_ffn2_w_down))
    mix_norm = (l0_mix_norm, l1_mix_norm)
    for i in range(DEPTH):
        n1, wu1, wd1 = ffn1[i]
        x = x + 0.5 * swiglu(rmsnorm(x, n1), wu1, wd1)
        h = rmsnorm(x, mix_norm[i])
        if i % 2 == 0:
            mix = mixer_delta_retention(h, l0_w_in, l0_conv_w, l0_a_log, l0_dt_bias, l0_gdn_norm, l0_w_out,
                                        cos64, sin64)
        else:
            lambda_init = 0.8 - 0.6 * math.exp(-0.3 * i)
            mix = mixer_dilated_differential(h, l1_w_in, l1_lambda_q1, l1_lambda_k1, l1_lambda_q2, l1_lambda_k2,
                                             l1_diff_norm, l1_w_out, cos128, sin128, cos64, sin64, lambda_init)
        x = x + mix
        n2, wu2, wd2 = ffn2[i]
        x = x + 0.5 * swiglu(rmsnorm(x, n2), wu2, wd2)
    return rmsnorm(x, final_norm)
```

```python
import functools
import math

import numpy as np
import jax
import jax.numpy as jnp
from jax import lax
from jax.experimental import pallas as pl
from jax.experimental.pallas import tpu as pltpu

F32 = jnp.float32
BF16 = jnp.bfloat16
HIGHEST = lax.Precision.HIGHEST

EPS = 1e-6
ROPE_THETA = 10000.0
D_MODEL = 1024
D_FF = 2816
N_HEADS = 4
HEAD_V = 128
GDN_DK = 128
GDN_CONV = 4
GDN_CHUNK = 64
GDN_QK_W = N_HEADS * GDN_DK
GDN_V_W = N_HEADS * HEAD_V
GDN_CONV_W = 2 * GDN_QK_W + GDN_V_W
RET_DK = 64
RET_QK_W = N_HEADS * RET_DK
DIL_DH = 128
DIL_PATTERNS = ((128, 1), (512, 4), (2048, 16))
DIL_KEYS = 128
DIFF_DK = 64
MIX_W = N_HEADS * HEAD_V

LANES = 128
VMEM_LIMIT = 56 * 1024 * 1024

FFN_TM = 512
FFN_TF = 256
PROJ_TM = 512
GDN_TS = 256
RET_C = 256
DIL_BLK = 2048
DIFF_TQ = 512
DIFF_TK = 512


def _rms(x):
    return x * lax.rsqrt(jnp.mean(x * x, axis=-1, keepdims=True) + EPS)


def _silu(x):
    return x * (1.0 / (1.0 + jnp.exp(-x)))


def _dot(a, b):
    return jnp.dot(a, b, preferred_element_type=F32)


def _dot_nt(a, b):
    return lax.dot_general(a, b, (((1,), (1,)), ((), ())), preferred_element_type=F32)


def _dot_tn(a, b):
    return lax.dot_general(a, b, (((0,), (0,)), ((), ())), preferred_element_type=F32)


def _params(*sem):
    return pltpu.CompilerParams(dimension_semantics=sem, vmem_limit_bytes=VMEM_LIMIT)


def _const_spec(shape):
    nd = len(shape)
    return pl.BlockSpec(shape, lambda *_: (0,) * nd)


def _ffn_kernel(x_ref, n_ref, wg_ref, wu_ref, wd_ref, fn_ref, o_ref, acc_ref, *, final):
    x = x_ref[...]
    hb = (_rms(x) * n_ref[...]).astype(BF16)
    nchunks = wg_ref.shape[0]
    for c in range(nchunks):
        g = _dot(hb, wg_ref[c])
        u = _dot(hb, wu_ref[c])
        a = (_silu(g) * u).astype(BF16)
        d = _dot(a, wd_ref[c])
        if c == 0:
            acc_ref[...] = d
        else:
            acc_ref[...] += d
    y = x + 0.5 * acc_ref[...]
    if final:
        y = _rms(y) * fn_ref[...]
    o_ref[...] = y


def _ffn(x2d, norm, w_up, w_down, final_norm=None):
    T = x2d.shape[0]
    nc = D_FF // FFN_TF
    wg = w_up[:, :D_FF].astype(BF16).reshape(D_MODEL, nc, FFN_TF).transpose(1, 0, 2)
    wu = w_up[:, D_FF:].astype(BF16).reshape(D_MODEL, nc, FFN_TF).transpose(1, 0, 2)
    wd = w_down.astype(BF16).reshape(nc, FFN_TF, D_MODEL)
    final = final_norm is not None
    fn = (final_norm if final else norm).reshape(1, D_MODEL)
    return pl.pallas_call(
        functools.partial(_ffn_kernel, final=final),
        grid=(T // FFN_TM,),
        in_specs=[
            pl.BlockSpec((FFN_TM, D_MODEL), lambda i: (i, 0)),
            _const_spec((1, D_MODEL)),
            _const_spec((nc, D_MODEL, FFN_TF)),
            _const_spec((nc, D_MODEL, FFN_TF)),
            _const_spec((nc, FFN_TF, D_MODEL)),
            _const_spec((1, D_MODEL)),
        ],
        out_specs=pl.BlockSpec((FFN_TM, D_MODEL), lambda i: (i, 0)),
        out_shape=jax.ShapeDtypeStruct((T, D_MODEL), F32),
        scratch_shapes=[pltpu.VMEM((FFN_TM, D_MODEL), F32)],
        compiler_params=_params("parallel"),
        name="ffn_final" if final else "ffn",
    )(x2d, norm.reshape(1, D_MODEL), wg, wu, wd, fn)


def _rope_tables(seq, dim):
    half = dim // 2
    inv = ROPE_THETA ** (-jnp.arange(0, dim, 2, dtype=F32) / dim)
    ang = jnp.arange(seq, dtype=F32)[:, None] * inv[None, :]
    reps = (LANES // 2) // half
    cos = jnp.tile(jnp.cos(ang), (1, 2 * reps))
    sin = jnp.tile(jnp.sin(ang), (1, reps))
    return cos, jnp.concatenate([-sin, sin], axis=1)


def _rope(x, cos, sin):
    outs = []
    for g in range(x.shape[1] // LANES):
        xg = x[:, g * LANES:(g + 1) * LANES]
        outs.append(xg * cos + pltpu.roll(xg, LANES // 2, 1) * sin)
    return outs[0] if len(outs) == 1 else jnp.concatenate(outs, axis=1)


_P0_QKV = (0, GDN_CONV_W)
_P0_Z = (_P0_QKV[1], _P0_QKV[1] + GDN_V_W)
_P0_RQ = (_P0_Z[1], _P0_Z[1] + RET_QK_W)
_P0_RK = (_P0_RQ[1], _P0_RQ[1] + RET_QK_W)
_P0_RV = (_P0_RK[1], _P0_RK[1] + MIX_W)
_P0_RG = (_P0_RV[1], _P0_RV[1] + MIX_W)
_P0_B = (_P0_RG[1], _P0_RG[1] + LANES)
_P0_A = (_P0_B[1], _P0_B[1] + LANES)
_P0_W = _P0_A[1]


def _ret_perm():
    perm = np.zeros(RET_QK_W, np.int32)
    for h in range(N_HEADS):
        for d in range(RET_DK):
            new = (h // 2) * LANES + (d // 32) * 64 + (h % 2) * 32 + d % 32
            perm[new] = h * RET_DK + d
    return perm


def _diff_perm():
    perm = np.zeros(N_HEADS * 2 * DIFF_DK, np.int32)
    for h in range(N_HEADS):
        for c in range(2):
            for d in range(DIFF_DK):
                new = h * LANES + (d // 32) * 64 + c * 32 + d % 32
                perm[new] = h * 2 * DIFF_DK + c * DIFF_DK + d
    return perm


def _pack_w_in0(w_in):
    o = np.cumsum((0, GDN_CONV_W, GDN_V_W, N_HEADS, N_HEADS, RET_QK_W, RET_QK_W, MIX_W, MIX_W))
    qkv, z, b, a, rq, rk, rv, rg = (w_in[:, o[i]:o[i + 1]] for i in range(8))
    perm = _ret_perm()
    pad = jnp.zeros((D_MODEL, LANES - N_HEADS), w_in.dtype)
    cols = [qkv, z, rq[:, perm], rk[:, perm], rv, rg, b, pad, a, pad]
    return jnp.concatenate(cols, axis=1).astype(BF16)


def _in0_kernel(x_ref, n_ref, w_ref, cos_ref, sin_ref,
                qkv_ref, z_ref, rq_ref, rk_ref, rv_ref, rg_ref, b_ref, a_ref):
    hb = (_rms(x_ref[...]) * n_ref[...]).astype(BF16)

    def proj(seg):
        return _dot(hb, w_ref[:, seg[0]:seg[1]])

    cos = cos_ref[...]
    sin = sin_ref[...]
    qkv_ref[...] = proj(_P0_QKV)
    z_ref[...] = proj(_P0_Z).astype(BF16)
    rq_ref[...] = (_rope(proj(_P0_RQ), cos, sin) * RET_DK ** -0.5).astype(BF16)
    rk_ref[...] = _rope(proj(_P0_RK), cos, sin).astype(BF16)
    rv_ref[...] = proj(_P0_RV).astype(BF16)
    rg_ref[...] = proj(_P0_RG).astype(BF16)
    b_ref[...] = proj(_P0_B)
    a_ref[...] = proj(_P0_A)


def _in0(x2d, norm, w_packed, cos, sin, seq):
    T = x2d.shape[0]
    tm = PROJ_TM
    spt = seq // tm
    row = lambda w: pl.BlockSpec((tm, w), lambda i: (i, 0))
    tab = pl.BlockSpec((tm, LANES), lambda i: (i % spt, 0))
    widths = (GDN_CONV_W, GDN_V_W, RET_QK_W, RET_QK_W, MIX_W, MIX_W, LANES, LANES)
    dtypes = (F32, BF16, BF16, BF16, BF16, BF16, F32, F32)
    return pl.pallas_call(
        _in0_kernel,
        grid=(T // tm,),
        in_specs=[row(D_MODEL), _const_spec((1, D_MODEL)), _const_spec((D_MODEL, _P0_W)), tab, tab],
        out_specs=[row(w) for w in widths],
        out_shape=[jax.ShapeDtypeStruct((T, w), dt) for w, dt in zip(widths, dtypes)],
        compiler_params=_params("parallel"),
        name="in_proj0",
    )(x2d, norm.reshape(1, D_MODEL), w_packed, cos, sin)


def _unit_lower_inverse(a, ri, ci, eye):
    hp = lambda p, q: jnp.dot(p, q, precision=HIGHEST, preferred_element_type=F32)
    n = jnp.where((ri >> 3) == (ci >> 3), -a, 0.0)
    t = eye + n
    p = hp(n, n)
    t = t + hp(t, p)
    p = hp(p, p)
    t = t + hp(t, p)
    for sh in (3, 4, 5):
        e = jnp.where(((ri >> (sh + 1)) == (ci >> (sh + 1))) & ((ri >> sh) != (ci >> sh)), a, 0.0)
        t = t - hp(hp(t, e), t)
    return t


def _gdn_kernel(qkv_ref, b_ref, a_ref, z_ref, cw_ref, alog_ref, dtb_ref, gn_ref, o_ref,
                xe_ref, q_s, k_s, v_s, gc_s, beta_s, o_s, state_ref):
    i = pl.program_id(1)
    ts = qkv_ref.shape[0]
    C = GDN_CHUNK
    halo = 8

    @pl.when(i == 0)
    def _():
        xe_ref[0:halo, :] = jnp.zeros((halo, GDN_CONV_W), F32)
        state_ref[...] = jnp.zeros_like(state_ref)

    xe_ref[halo:halo + ts, :] = qkv_ref[...]
    conv = cw_ref[0:1, :] * xe_ref[halo - 3:halo - 3 + ts, :]
    for j in range(1, GDN_CONV):
        conv = conv + cw_ref[j:j + 1, :] * xe_ref[halo - 3 + j:halo - 3 + j + ts, :]
    xe_ref[0:halo, :] = xe_ref[ts:ts + halo, :]
    act = _silu(conv)
    for h in range(N_HEADS):
        lo = h * GDN_DK
        qh = act[:, lo:lo + GDN_DK]
        kh = act[:, GDN_QK_W + lo:GDN_QK_W + lo + GDN_DK]
        q_s[:, lo:lo + GDN_DK] = qh * (lax.rsqrt(jnp.sum(qh * qh, -1, keepdims=True) + EPS) * GDN_DK ** -0.5)
        k_s[:, lo:lo + GDN_DK] = kh * lax.rsqrt(jnp.sum(kh * kh, -1, keepdims=True) + EPS)
    v_s[...] = act[:, 2 * GDN_QK_W:]

    beta_s[...] = 1.0 / (1.0 + jnp.exp(-b_ref[...]))
    xa = a_ref[...] + dtb_ref[...]
    softplus = jnp.maximum(xa, 0.0) + jnp.log(1.0 + jnp.exp(-jnp.abs(xa)))
    g = -jnp.exp(alog_ref[...]) * softplus
    rt = lax.broadcasted_iota(jnp.int32, (ts, ts), 0)
    ct = lax.broadcasted_iota(jnp.int32, (ts, ts), 1)
    tri = jnp.where((rt >= ct) & ((rt >> 6) == (ct >> 6)), 1.0, 0.0)
    gc_s[...] = jnp.dot(tri, g, precision=HIGHEST, preferred_element_type=F32)

    ri = lax.broadcasted_iota(jnp.int32, (C, C), 0)
    ci = lax.broadcasted_iota(jnp.int32, (C, C), 1)
    causal = ri >= ci
    strict = ri > ci
    eye = jnp.where(ri == ci, 1.0, 0.0)

    def chunk(c, carry):
        r0 = pl.multiple_of(c * C, C)
        gc_all = gc_s[pl.ds(r0, C), :]
        gct_all = gc_all.T
        beta_all = beta_s[pl.ds(r0, C), :]
        for h in range(N_HEADS):
            lo = h * GDN_DK
            q = q_s[pl.ds(r0, C), lo:lo + GDN_DK]
            k = k_s[pl.ds(r0, C), lo:lo + GDN_DK]
            v = v_s[pl.ds(r0, C), lo:lo + HEAD_V]
            gcol = gc_all[:, h:h + 1]
            grow = gct_all[h:h + 1, :]
            bcol = beta_all[:, h:h + 1]
            glast = gc_all[C - 1:C, h:h + 1]
            decay = jnp.where(causal, jnp.exp(jnp.where(causal, gcol - grow, 0.0)), 0.0)
            kb = k.astype(BF16)
            a_mat = jnp.where(strict, _dot_nt(kb, kb) * decay * bcol, 0.0)
            t_inv = _unit_lower_inverse(a_mat, ri, ci, eye)
            eg = jnp.exp(gcol)
            rhs = jnp.concatenate([v * bcol, k * (bcol * eg)], axis=1)
            sol = _dot(t_inv.astype(BF16), rhs.astype(BF16))
            u = sol[:, :HEAD_V]
            w = sol[:, HEAD_V:]
            qk = _dot_nt(q.astype(BF16), kb) * decay
            state = state_ref[h]
            sb = state.astype(BF16)
            v_new = u - _dot(w.astype(BF16), sb)
            vb = v_new.astype(BF16)
            o = _dot((q * eg).astype(BF16), sb) + _dot(qk.astype(BF16), vb)
            k_dec = (k * jnp.exp(glast - gcol)).astype(BF16)
            state_ref[h] = state * jnp.exp(glast) + _dot_tn(k_dec, vb)
            o_s[pl.ds(r0, C), lo:lo + HEAD_V] = o
        return carry

    lax.fori_loop(0, ts // C, chunk, 0)

    for h in range(N_HEADS):
        lo = h * HEAD_V
        zh = z_ref[:, lo:lo + HEAD_V].astype(F32)
        o_ref[:, lo:lo + HEAD_V] = (_rms(o_s[:, lo:lo + HEAD_V]) * gn_ref[...] * _silu(zh)).astype(BF16)


def _gdn(qkv, b, a, z, conv_w, a_log, dt_bias, gdn_norm, batch, seq):
    T = qkv.shape[0]
    ts = GDN_TS
    spt = seq // ts
    row = lambda w: pl.BlockSpec((ts, w), lambda bi, i: (bi * spt + i, 0))
    pad4 = lambda p: jnp.pad(p.astype(F32), (0, LANES - N_HEADS)).reshape(1, LANES)
    return pl.pallas_call(
        _gdn_kernel,
        grid=(batch, spt),
        in_specs=[row(GDN_CONV_W), row(LANES), row(LANES), row(GDN_V_W),
                  _const_spec((GDN_CONV, GDN_CONV_W)), _const_spec((1, LANES)), _const_spec((1, LANES)),
                  _const_spec((1, HEAD_V))],
        out_specs=row(GDN_V_W),
        out_shape=jax.ShapeDtypeStruct((T, GDN_V_W), BF16),
        scratch_shapes=[
            pltpu.VMEM((ts + 8, GDN_CONV_W), F32),
            pltpu.VMEM((ts, GDN_QK_W), F32),
            pltpu.VMEM((ts, GDN_QK_W), F32),
            pltpu.VMEM((ts, GDN_V_W), F32),
            pltpu.VMEM((ts, LANES), F32),
            pltpu.VMEM((ts, LANES), F32),
            pltpu.VMEM((ts, GDN_V_W), F32),
            pltpu.VMEM((N_HEADS, GDN_DK, HEAD_V), F32),
        ],
        compiler_params=_params("arbitrary", "arbitrary"),
        name="gated_deltanet",
    )(qkv, b, a, z, conv_w.astype(F32), pad4(a_log), pad4(dt_bias), gdn_norm.reshape(1, HEAD_V))


def _ret_kernel(q_ref, k_ref, v_ref, g_ref, o_ref, state_ref):
    i = pl.program_id(1)
    c = q_ref.shape[0]

    @pl.when(i == 0)
    def _():
        state_ref[...] = jnp.zeros_like(state_ref)

    ri = lax.broadcasted_iota(jnp.int32, (c, c), 0)
    ci = lax.broadcasted_iota(jnp.int32, (c, c), 1)
    rel = (ri - ci).astype(F32)
    pos = lax.broadcasted_iota(jnp.int32, (c, 1), 0).astype(F32)
    lane = lax.broadcasted_iota(jnp.int32, (1, LANES), 1)
    for h in range(N_HEADS):
        lg = math.log(1.0 - 2.0 ** (-5.0 - h))
        grp = (h // 2) * LANES
        own = ((lane >> 5) & 1) == h % 2
        qm = jnp.where(own, q_ref[:, grp:grp + LANES], 0)
        km = jnp.where(own, k_ref[:, grp:grp + LANES], 0)
        v = v_ref[:, h * HEAD_V:(h + 1) * HEAD_V]
        dmask = jnp.where(rel >= 0, jnp.exp(jnp.where(rel >= 0, rel, 0.0) * lg), 0.0)
        intra = _dot((_dot_nt(qm, km) * dmask).astype(BF16), v)
        state = state_ref[h]
        inter = _dot(qm, state.astype(BF16)) * jnp.exp((pos + 1.0) * lg)
        k_sc = (km.astype(F32) * jnp.exp((c - 1.0 - pos) * lg)).astype(BF16)
        state_ref[h] = state * math.exp(c * lg) + _dot_tn(k_sc, v)
        gate = _silu(g_ref[:, h * HEAD_V:(h + 1) * HEAD_V].astype(F32))
        o_ref[:, h * HEAD_V:(h + 1) * HEAD_V] = (_rms(intra + inter) * gate).astype(BF16)


def _retention(rq, rk, rv, rg, batch, seq):
    T = rq.shape[0]
    c = RET_C
    spt = seq // c
    row = lambda w: pl.BlockSpec((c, w), lambda bi, i: (bi * spt + i, 0))
    return pl.pallas_call(
        _ret_kernel,
        grid=(batch, spt),
        in_specs=[row(RET_QK_W), row(RET_QK_W), row(MIX_W), row(MIX_W)],
        out_specs=row(MIX_W),
        out_shape=jax.ShapeDtypeStruct((T, MIX_W), BF16),
        scratch_shapes=[pltpu.VMEM((N_HEADS, LANES, HEAD_V), F32)],
        compiler_params=_params("arbitrary", "arbitrary"),
        name="retention",
    )(rq, rk, rv, rg)


def _out_kernel(x_ref, a_ref, b_ref, w_ref, o_ref):
    o_ref[...] = (x_ref[...] + _dot(a_ref[...], w_ref[0:MIX_W, :])
                  + _dot(b_ref[...], w_ref[MIX_W:2 * MIX_W, :]))


def _out_proj(x2d, oa, ob, w_out):
    T = x2d.shape[0]
    tm = PROJ_TM
    row = lambda w: pl.BlockSpec((tm, w), lambda i: (i, 0))
    return pl.pallas_call(
        _out_kernel,
        grid=(T // tm,),
        in_specs=[row(D_MODEL), row(MIX_W), row(MIX_W), _const_spec((2 * MIX_W, D_MODEL))],
        out_specs=row(D_MODEL),
        out_shape=jax.ShapeDtypeStruct((T, D_MODEL), F32),
        compiler_params=_params("parallel"),
        name="out_proj",
    )(x2d, oa, ob, w_out.astype(BF16))


def _pack_w_in1(w_in):
    o = np.cumsum((0,) + (MIX_W,) * 6)
    cq, ck, cv, dq, dk, dv = (w_in[:, o[i]:o[i + 1]] for i in range(6))
    perm = _diff_perm()
    return jnp.concatenate([cq, ck, cv, dq[:, perm], dk[:, perm], dv], axis=1).astype(BF16)


def _in1_kernel(x_ref, n_ref, w_ref, cosa_ref, sina_ref, cosb_ref, sinb_ref,
                cq_ref, ck_ref, cv_ref, dq_ref, dk_ref, dv_ref):
    hb = (_rms(x_ref[...]) * n_ref[...]).astype(BF16)

    def proj(j):
        return _dot(hb, w_ref[:, j * MIX_W:(j + 1) * MIX_W])

    cq_ref[...] = _rope(proj(0), cosa_ref[...], sina_ref[...]) * DIL_DH ** -0.5
    ck_ref[...] = _rope(proj(1), cosa_ref[...], sina_ref[...])
    cv_ref[...] = proj(2)
    dq_ref[...] = (_rope(proj(3), cosb_ref[...], sinb_ref[...]) * DIFF_DK ** -0.5).astype(BF16)
    dk_ref[...] = _rope(proj(4), cosb_ref[...], sinb_ref[...]).astype(BF16)
    dv_ref[...] = proj(5).astype(BF16)


def _in1(x2d, norm, w_packed, cosa, sina, cosb, sinb, seq):
    T = x2d.shape[0]
    tm = PROJ_TM
    spt = seq // tm
    row = lambda w: pl.BlockSpec((tm, w), lambda i: (i, 0))
    tab = pl.BlockSpec((tm, LANES), lambda i: (i % spt, 0))
    dtypes = (F32, F32, F32, BF16, BF16, BF16)
    return pl.pallas_call(
        _in1_kernel,
        grid=(T // tm,),
        in_specs=[row(D_MODEL), _const_spec((1, D_MODEL)), _const_spec((D_MODEL, 6 * MIX_W)),
                  tab, tab, tab, tab],
        out_specs=[row(MIX_W)] * 6,
        out_shape=[jax.ShapeDtypeStruct((T, MIX_W), dt) for dt in dtypes],
        compiler_params=_params("parallel"),
        name="in_proj1",
    )(x2d, norm.reshape(1, D_MODEL), w_packed, cosa, sina, cosb, sinb)


def _dil_kernel(q_ref, kp_ref, kc_ref, vp_ref, vc_ref, o_ref, kcat, vcat, ob, mb, lb):
    i = pl.program_id(2)
    blk = q_ref.shape[0]
    nk = DIL_KEYS
    kcat[0:blk, :] = kp_ref[...]
    kcat[blk:2 * blk, :] = kc_ref[...]
    vcat[0:blk, :] = vp_ref[...]
    vcat[blk:2 * blk, :] = vc_ref[...]

    qi = lax.broadcasted_iota(jnp.int32, (nk, 2 * nk), 0)
    kj = lax.broadcasted_iota(jnp.int32, (nk, 2 * nk), 1)
    dist = qi + nk - kj
    band = (dist >= 0) & (dist <= nk)

    for gi, (_, d) in enumerate(DIL_PATTERNS):
        span = nk * d

        def unit(n, carry, gi=gi, d=d, span=span):
            sub = n // d
            r = n % d
            qs = sub * span + r
            ks = qs + blk - span
            q = q_ref[pl.ds(qs, nk, stride=d), :].astype(BF16)
            k = kcat[pl.ds(ks, 2 * nk, stride=d), :].astype(BF16)
            v = vcat[pl.ds(ks, 2 * nk, stride=d), :].astype(BF16)
            first_key = jnp.where((i > 0) | (sub > 0), 0, nk)
            s = jnp.where(band & (kj >= first_key), _dot_nt(q, k), -jnp.inf)
            m = jnp.max(s, axis=-1, keepdims=True)
            p = jnp.exp(s - m)
            l = jnp.sum(p, axis=-1, keepdims=True)
            ob[gi, pl.ds(qs, nk, stride=d), :] = _dot(p.astype(BF16), v)
            mb[gi, pl.ds(qs, nk, stride=d), :] = jnp.broadcast_to(m, (nk, LANES))
            lb[gi, pl.ds(qs, nk, stride=d), :] = jnp.broadcast_to(l, (nk, LANES))
            return carry

        lax.fori_loop(0, blk // nk, unit, 0)

    m_all = jnp.maximum(jnp.maximum(mb[0], mb[1]), mb[2])
    num = jnp.zeros((blk, HEAD_V), F32)
    den = jnp.zeros((blk, LANES), F32)
    for gi in range(len(DIL_PATTERNS)):
        sc = jnp.exp(mb[gi] - m_all)
        num = num + ob[gi] * sc
        den = den + lb[gi] * sc
    o_ref[...] = (num / den).astype(BF16)


def _dilated(cq, ck, cv, batch, seq):
    blk = DIL_BLK
    q3, k3, v3 = (t.reshape(batch, seq, MIX_W) for t in (cq, ck, cv))
    cur = pl.BlockSpec((None, blk, DIL_DH), lambda b, h, i: (b, i, h))
    prev = pl.BlockSpec((None, blk, DIL_DH), lambda b, h, i: (b, jnp.maximum(i - 1, 0), h))
    nb = len(DIL_PATTERNS)
    out = pl.pallas_call(
        _dil_kernel,
        grid=(batch, N_HEADS, seq // blk),
        in_specs=[cur, prev, cur, prev, cur],
        out_specs=cur,
        out_shape=jax.ShapeDtypeStruct((batch, seq, MIX_W), BF16),
        scratch_shapes=[
            pltpu.VMEM((2 * blk, DIL_DH), F32),
            pltpu.VMEM((2 * blk, DIL_DH), F32),
            pltpu.VMEM((nb, blk, HEAD_V), F32),
            pltpu.VMEM((nb, blk, LANES), F32),
            pltpu.VMEM((nb, blk, LANES), F32),
        ],
        compiler_params=_params("parallel", "parallel", "arbitrary"),
        name="dilated_attention",
    )(q3, k3, k3, v3, v3)
    return out.reshape(batch * seq, MIX_W)


def _diff_kernel(q_ref, k_ref, v_ref, lq1_ref, lk1_ref, lq2_ref, lk2_ref, dn_ref, o_ref,
                 m_s, l_s, acc_s, *, lambda_init):
    i = pl.program_id(2)
    tq = q_ref.shape[0]
    tk = DIFF_TK
    lane = lax.broadcasted_iota(jnp.int32, (1, LANES), 1)
    first = ((lane >> 5) & 1) == 0
    q = q_ref[...]
    qq = jnp.concatenate([jnp.where(first, q, 0), jnp.where(first, 0, q)], axis=0)
    m_s[...] = jnp.full_like(m_s, -jnp.inf)
    l_s[...] = jnp.zeros_like(l_s)
    acc_s[...] = jnp.zeros_like(acc_s)

    def step(kt, masked):
        k0 = pl.multiple_of(kt * tk, tk)
        s = _dot_nt(qq, k_ref[pl.ds(k0, tk), :])
        if masked:
            row = lax.broadcasted_iota(jnp.int32, (2 * tq, tk), 0) & (tq - 1)
            col = lax.broadcasted_iota(jnp.int32, (2 * tq, tk), 1)
            s = jnp.where(row >= col, s, -jnp.inf)
        m_prev = m_s[...]
        m_new = jnp.maximum(m_prev, jnp.max(s, axis=-1, keepdims=True))
        alpha = jnp.exp(m_prev - m_new)
        p = jnp.exp(s - m_new[:, 0:1])
        l_s[...] = alpha * l_s[...] + jnp.sum(p, axis=-1, keepdims=True)
        acc_s[...] = alpha * acc_s[...] + _dot(p.astype(BF16), v_ref[pl.ds(k0, tk), :])
        m_s[...] = m_new

    def body(kt, carry):
        step(kt, False)
        return carry

    lax.fori_loop(0, i, body, 0)
    step(i, True)

    lam = (jnp.exp(jnp.sum(lq1_ref[...] * lk1_ref[...], axis=-1, keepdims=True))
           - jnp.exp(jnp.sum(lq2_ref[...] * lk2_ref[...], axis=-1, keepdims=True)) + lambda_init)
    o = acc_s[0:tq, :] / l_s[0:tq, :] - lam * (acc_s[tq:2 * tq, :] / l_s[tq:2 * tq, :])
    o_ref[...] = (_rms(o) * dn_ref[...] * (1.0 - lambda_init)).astype(BF16)


def _differential(dq, dk, dv, lq1, lk1, lq2, lk2, diff_norm, lambda_init, batch, seq):
    tq = DIFF_TQ
    q3, k3, v3 = (t.reshape(batch, seq, MIX_W) for t in (dq, dk, dv))
    qspec = pl.BlockSpec((None, tq, LANES), lambda b, h, i: (b, i, h))
    kvspec = pl.BlockSpec((None, seq, LANES), lambda b, h, i: (b, 0, h))
    vec = lambda p: p.reshape(1, DIFF_DK).astype(F32)
    out = pl.pallas_call(
        functools.partial(_diff_kernel, lambda_init=lambda_init),
        grid=(batch, N_HEADS, seq // tq),
        in_specs=[qspec, kvspec, kvspec] + [_const_spec((1, DIFF_DK))] * 4 + [_const_spec((1, HEAD_V))],
        out_specs=qspec,
        out_shape=jax.ShapeDtypeStruct((batch, seq, MIX_W), BF16),
        scratch_shapes=[
            pltpu.VMEM((2 * tq, LANES), F32),
            pltpu.VMEM((2 * tq, LANES), F32),
            pltpu.VMEM((2 * tq, HEAD_V), F32),
        ],
        compiler_params=_params("parallel", "parallel", "arbitrary"),
        name="differential_attention",
    )(q3, k3, v3, vec(lq1), vec(lk1), vec(lq2), vec(lk2), diff_norm.reshape(1, HEAD_V))
    return out.reshape(batch * seq, MIX_W)


def kernel(x, l0_ffn1_norm, l0_ffn1_w_up, l0_ffn1_w_down, l0_mix_norm, l0_w_in, l0_conv_w, l0_a_log,
           l0_dt_bias, l0_gdn_norm, l0_w_out, l0_ffn2_norm, l0_ffn2_w_up, l0_ffn2_w_down,
           l1_ffn1_norm, l1_ffn1_w_up, l1_ffn1_w_down, l1_mix_norm, l1_w_in, l1_lambda_q1, l1_lambda_k1,
           l1_lambda_q2, l1_lambda_k2, l1_diff_norm, l1_w_out, l1_ffn2_norm, l1_ffn2_w_up, l1_ffn2_w_down,
           final_norm):
    batch, seq, _ = x.shape
    cos64, sin64 = _rope_tables(seq, 64)
    cos128, sin128 = _rope_tables(seq, 128)
    xf = x.reshape(batch * seq, D_MODEL)

    xf = _ffn(xf, l0_ffn1_norm, l0_ffn1_w_up, l0_ffn1_w_down)
    qkv, z, rq, rk, rv, rg, b, a = _in0(xf, l0_mix_norm, _pack_w_in0(l0_w_in), cos64, sin64, seq)
    o_a = _gdn(qkv, b, a, z, l0_conv_w, l0_a_log, l0_dt_bias, l0_gdn_norm, batch, seq)
    o_b = _retention(rq, rk, rv, rg, batch, seq)
    xf = _out_proj(xf, o_a, o_b, l0_w_out)
    xf = _ffn(xf, l0_ffn2_norm, l0_ffn2_w_up, l0_ffn2_w_down)

    xf = _ffn(xf, l1_ffn1_norm, l1_ffn1_w_up, l1_ffn1_w_down)
    cq, ck, cv, dq, dk, dv = _in1(xf, l1_mix_norm, _pack_w_in1(l1_w_in), cos128, sin128, cos64, sin64, seq)
    o_c = _dilated(cq, ck, cv, batch, seq)
    lambda_init = 0.8 - 0.6 * math.exp(-0.3 * 1)
    o_d = _differential(dq, dk, dv, l1_lambda_q1, l1_lambda_k1, l1_lambda_q2, l1_lambda_k2,
                        l1_diff_norm, lambda_init, batch, seq)
    xf = _out_proj(xf, o_c, o_d, l1_w_out)
    xf = _ffn(xf, l1_ffn2_norm, l1_ffn2_w_up, l1_ffn2_w_down, final_norm=final_norm)
    return xf.reshape(batch, seq, D_MODEL)
```

```python
import functools
import math

import numpy as np
import jax
import jax.numpy as jnp
from jax import lax
from jax.experimental import pallas as pl
from jax.experimental.pallas import tpu as pltpu

F32 = jnp.float32
BF16 = jnp.bfloat16
HIGHEST = lax.Precision.HIGHEST

EPS = 1e-6
ROPE_THETA = 10000.0
D_MODEL = 1024
D_FF = 2816
N_HEADS = 4
HEAD_V = 128
GDN_DK = 128
GDN_CONV = 4
GDN_CHUNK = 64
GDN_QK_W = N_HEADS * GDN_DK
GDN_V_W = N_HEADS * HEAD_V
GDN_CONV_W = 2 * GDN_QK_W + GDN_V_W
RET_DK = 64
RET_QK_W = N_HEADS * RET_DK
DIL_DH = 128
DIL_PATTERNS = ((128, 1), (512, 4), (2048, 16))
DIL_KEYS = 128
DIFF_DK = 64
MIX_W = N_HEADS * HEAD_V

LANES = 128
VMEM_LIMIT = 56 * 1024 * 1024

FFN_TM = 512
FFN_TF = 256
PROJ_TM = 512
GDN_TS = 256
RET_C = 256
DIL_BLK = 2048
DIFF_TQ = 512
DIFF_TK = 512


def _rms(x):
    return x * lax.rsqrt(jnp.mean(x * x, axis=-1, keepdims=True) + EPS)


def _silu(x):
    return x * (1.0 / (1.0 + jnp.exp(-x)))


def _dot(a, b):
    return jnp.dot(a, b, preferred_element_type=F32)


def _dot_nt(a, b):
    return lax.dot_general(a, b, (((1,), (1,)), ((), ())), preferred_element_type=F32)


def _dot_tn(a, b):
    return lax.dot_general(a, b, (((0,), (0,)), ((), ())), preferred_element_type=F32)


def _params(*sem):
    return pltpu.CompilerParams(dimension_semantics=sem, vmem_limit_bytes=VMEM_LIMIT)


def _const_spec(shape):
    nd = len(shape)
    return pl.BlockSpec(shape, lambda *_: (0,) * nd)


def _ffn_kernel(x_ref, n_ref, wg_ref, wu_ref, wd_ref, fn_ref, o_ref, acc_ref, *, final):
    x = x_ref[...]
    hb = (_rms(x) * n_ref[...]).astype(BF16)
    nchunks = wg_ref.shape[0]
    for c in range(nchunks):
        g = _dot(hb, wg_ref[c])
        u = _dot(hb, wu_ref[c])
        a = (_silu(g) * u).astype(BF16)
        d = _dot(a, wd_ref[c])
        if c == 0:
            acc_ref[...] = d
        else:
            acc_ref[...] += d
    y = x + 0.5 * acc_ref[...]
    if final:
        y = _rms(y) * fn_ref[...]
    o_ref[...] = y


def _ffn(x2d, norm, w_up, w_down, final_norm=None):
    T = x2d.shape[0]
    nc = D_FF // FFN_TF
    wg = w_up[:, :D_FF].astype(BF16).reshape(D_MODEL, nc, FFN_TF).transpose(1, 0, 2)
    wu = w_up[:, D_FF:].astype(BF16).reshape(D_MODEL, nc, FFN_TF).transpose(1, 0, 2)
    wd = w_down.astype(BF16).reshape(nc, FFN_TF, D_MODEL)
    final = final_norm is not None
    fn = (final_norm if final else norm).reshape(1, D_MODEL)
    return pl.pallas_call(
        functools.partial(_ffn_kernel, final=final),
        grid=(T // FFN_TM,),
        in_specs=[
            pl.BlockSpec((FFN_TM, D_MODEL), lambda i: (i, 0)),
            _const_spec((1, D_MODEL)),
            _const_spec((nc, D_MODEL, FFN_TF)),
            _const_spec((nc, D_MODEL, FFN_TF)),
            _const_spec((nc, FFN_TF, D_MODEL)),
            _const_spec((1, D_MODEL)),
        ],
        out_specs=pl.BlockSpec((FFN_TM, D_MODEL), lambda i: (i, 0)),
        out_shape=jax.ShapeDtypeStruct((T, D_MODEL), F32),
        scratch_shapes=[pltpu.VMEM((FFN_TM, D_MODEL), F32)],
        compiler_params=_params("parallel"),
        name="ffn_final" if final else "ffn",
    )(x2d, norm.reshape(1, D_MODEL), wg, wu, wd, fn)


def _rope_tables(seq, dim):
    half = dim // 2
    inv = ROPE_THETA ** (-jnp.arange(0, dim, 2, dtype=F32) / dim)
    ang = jnp.arange(seq, dtype=F32)[:, None] * inv[None, :]
    reps = (LANES // 2) // half
    cos = jnp.tile(jnp.cos(ang), (1, 2 * reps))
    sin = jnp.tile(jnp.sin(ang), (1, reps))
    return cos, jnp.concatenate([-sin, sin], axis=1)


def _rope(x, cos, sin):
    outs = []
    for g in range(x.shape[1] // LANES):
        xg = x[:, g * LANES:(g + 1) * LANES]
        outs.append(xg * cos + pltpu.roll(xg, LANES // 2, 1) * sin)
    return outs[0] if len(outs) == 1 else jnp.concatenate(outs, axis=1)


_P0_QKV = (0, GDN_CONV_W)
_P0_Z = (_P0_QKV[1], _P0_QKV[1] + GDN_V_W)
_P0_RQ = (_P0_Z[1], _P0_Z[1] + RET_QK_W)
_P0_RK = (_P0_RQ[1], _P0_RQ[1] + RET_QK_W)
_P0_RV = (_P0_RK[1], _P0_RK[1] + MIX_W)
_P0_RG = (_P0_RV[1], _P0_RV[1] + MIX_W)
_P0_B = (_P0_RG[1], _P0_RG[1] + LANES)
_P0_A = (_P0_B[1], _P0_B[1] + LANES)
_P0_W = _P0_A[1]


def _ret_perm():
    perm = np.zeros(RET_QK_W, np.int32)
    for h in range(N_HEADS):
        for d in range(RET_DK):
            new = (h // 2) * LANES + (d // 32) * 64 + (h % 2) * 32 + d % 32
            perm[new] = h * RET_DK + d
    return perm


def _diff_perm():
    perm = np.zeros(N_HEADS * 2 * DIFF_DK, np.int32)
    for h in range(N_HEADS):
        for c in range(2):
            for d in range(DIFF_DK):
                new = h * LANES + (d // 32) * 64 + c * 32 + d % 32
                perm[new] = h * 2 * DIFF_DK + c * DIFF_DK + d
    return perm


def _pack_w_in0(w_in):
    o = np.cumsum((0, GDN_CONV_W, GDN_V_W, N_HEADS, N_HEADS, RET_QK_W, RET_QK_W, MIX_W, MIX_W))
    qkv, z, b, a, rq, rk, rv, rg = (w_in[:, o[i]:o[i + 1]] for i in range(8))
    perm = _ret_perm()
    pad = jnp.zeros((D_MODEL, LANES - N_HEADS), w_in.dtype)
    cols = [qkv, z, rq[:, perm], rk[:, perm], rv, rg, b, pad, a, pad]
    return jnp.concatenate(cols, axis=1).astype(BF16)


def _in0_kernel(x_ref, n_ref, w_ref, cos_ref, sin_ref,
                qkv_ref, z_ref, rq_ref, rk_ref, rv_ref, rg_ref, b_ref, a_ref):
    hb = (_rms(x_ref[...]) * n_ref[...]).astype(BF16)

    def proj(seg):
        return _dot(hb, w_ref[:, seg[0]:seg[1]])

    cos = cos_ref[...]
    sin = sin_ref[...]
    qkv_ref[...] = proj(_P0_QKV)
    z_ref[...] = proj(_P0_Z).astype(BF16)
    rq_ref[...] = (_rope(proj(_P0_RQ), cos, sin) * RET_DK ** -0.5).astype(BF16)
    rk_ref[...] = _rope(proj(_P0_RK), cos, sin).astype(BF16)
    rv_ref[...] = proj(_P0_RV).astype(BF16)
    rg_ref[...] = proj(_P0_RG).astype(BF16)
    b_ref[...] = proj(_P0_B)
    a_ref[...] = proj(_P0_A)


def _in0(x2d, norm, w_packed, cos, sin, seq):
    T = x2d.shape[0]
    tm = PROJ_TM
    spt = seq // tm
    row = lambda w: pl.BlockSpec((tm, w), lambda i: (i, 0))
    tab = pl.BlockSpec((tm, LANES), lambda i: (i % spt, 0))
    widths = (GDN_CONV_W, GDN_V_W, RET_QK_W, RET_QK_W, MIX_W, MIX_W, LANES, LANES)
    dtypes = (F32, BF16, BF16, BF16, BF16, BF16, F32, F32)
    return pl.pallas_call(
        _in0_kernel,
        grid=(T // tm,),
        in_specs=[row(D_MODEL), _const_spec((1, D_MODEL)), _const_spec((D_MODEL, _P0_W)), tab, tab],
        out_specs=[row(w) for w in widths],
        out_shape=[jax.ShapeDtypeStruct((T, w), dt) for w, dt in zip(widths, dtypes)],
        compiler_params=_params("parallel"),
        name="in_proj0",
    )(x2d, norm.reshape(1, D_MODEL), w_packed, cos, sin)


def _split_bf16(x):
    hi = x.astype(BF16)
    return hi, (x - hi.astype(F32)).astype(BF16)


def _bmm3(a, b):
    ah, al = _split_bf16(a)
    bh, bl = _split_bf16(b)
    mm = lambda p, q: jnp.einsum("gik,gkj->gij", p, q, preferred_element_type=F32)
    return mm(ah, bh) + mm(ah, bl) + mm(al, bh)


def _unit_lower_inverse(a, ri, ci):
    n = jnp.where((ri >> 3) == (ci >> 3), -a, 0.0)
    t = jnp.where(ri == ci, 1.0, 0.0) + n
    p = _bmm3(n, n)
    t = t + _bmm3(t, p)
    p = _bmm3(p, p)
    t = t + _bmm3(t, p)
    for sh in (3, 4, 5):
        e = jnp.where(((ri >> (sh + 1)) == (ci >> (sh + 1))) & ((ri >> sh) != (ci >> sh)), a, 0.0)
        t = t - _bmm3(_bmm3(t, e), t)
    return t


def _gdn_kernel(qkv_ref, b_ref, a_ref, z_ref, cw_ref, alog_ref, dtb_ref, gn_ref, o_ref,
                xe_ref, o_s, state_ref):
    i = pl.program_id(1)
    ts = qkv_ref.shape[0]
    C = GDN_CHUNK
    nch = ts // C
    halo = 8

    @pl.when(i == 0)
    def _():
        xe_ref[0:halo, :] = jnp.zeros((halo, GDN_CONV_W), F32)
        state_ref[...] = jnp.zeros_like(state_ref)

    xe_ref[halo:halo + ts, :] = qkv_ref[...]
    conv = cw_ref[0:1, :] * xe_ref[halo - 3:halo - 3 + ts, :]
    for j in range(1, GDN_CONV):
        conv = conv + cw_ref[j:j + 1, :] * xe_ref[halo - 3 + j:halo - 3 + j + ts, :]
    xe_ref[0:halo, :] = xe_ref[ts:ts + halo, :]
    act = _silu(conv)
    qn, kn, vv = [], [], []
    for h in range(N_HEADS):
        lo = h * GDN_DK
        qh = act[:, lo:lo + GDN_DK]
        kh = act[:, GDN_QK_W + lo:GDN_QK_W + lo + GDN_DK]
        qn.append(qh * (lax.rsqrt(jnp.sum(qh * qh, -1, keepdims=True) + EPS) * GDN_DK ** -0.5))
        kn.append(kh * lax.rsqrt(jnp.sum(kh * kh, -1, keepdims=True) + EPS))
        vv.append(act[:, 2 * GDN_QK_W + h * HEAD_V:2 * GDN_QK_W + (h + 1) * HEAD_V])

    beta = 1.0 / (1.0 + jnp.exp(-b_ref[...]))
    xa = a_ref[...] + dtb_ref[...]
    softplus = jnp.maximum(xa, 0.0) + jnp.log(1.0 + jnp.exp(-jnp.abs(xa)))
    g = -jnp.exp(alog_ref[...]) * softplus
    rt = lax.broadcasted_iota(jnp.int32, (ts, ts), 0)
    ct = lax.broadcasted_iota(jnp.int32, (ts, ts), 1)
    tri = jnp.where((rt >= ct) & ((rt >> 6) == (ct >> 6)), 1.0, 0.0)
    gc = jnp.dot(tri, g, precision=HIGHEST, preferred_element_type=F32)

    units = [(c, h) for c in range(nch) for h in range(N_HEADS)]
    rows = lambda x, c: x[c * C:(c + 1) * C]
    stack = lambda f: jnp.stack([f(c, h) for c, h in units], axis=0)
    q = stack(lambda c, h: rows(qn[h], c))
    k = stack(lambda c, h: rows(kn[h], c))
    v = stack(lambda c, h: rows(vv[h], c))
    gct = [rows(gc, c).T for c in range(nch)]
    gcol = stack(lambda c, h: rows(gc, c)[:, h:h + 1])
    grow = stack(lambda c, h: gct[c][h:h + 1, :])
    bcol = stack(lambda c, h: rows(beta, c)[:, h:h + 1])
    glast = gcol[:, C - 1:C, :]

    ri = lax.broadcasted_iota(jnp.int32, (1, C, C), 1)
    ci = lax.broadcasted_iota(jnp.int32, (1, C, C), 2)
    causal = ri >= ci
    decay = jnp.where(causal, jnp.exp(jnp.where(causal, gcol - grow, 0.0)), 0.0)
    kb = k.astype(BF16)
    kk = jnp.einsum("gid,gjd->gij", kb, kb, preferred_element_type=F32)
    a_mat = jnp.where(ri > ci, kk * decay * bcol, 0.0)
    t_inv = _unit_lower_inverse(a_mat, ri, ci)
    eg = jnp.exp(gcol)
    rhs = jnp.concatenate([v * bcol, k * (bcol * eg)], axis=2).astype(BF16)
    sol = jnp.einsum("gik,gkj->gij", t_inv.astype(BF16), rhs, preferred_element_type=F32)
    u_all = sol[:, :, :HEAD_V]
    w_all = sol[:, :, HEAD_V:].astype(BF16)
    qk_all = (jnp.einsum("gid,gjd->gij", q.astype(BF16), kb, preferred_element_type=F32) * decay).astype(BF16)
    qd_all = (q * eg).astype(BF16)
    kd_all = (k * jnp.exp(glast - gcol)).astype(BF16)
    egl = jnp.exp(glast)

    states = [state_ref[h] for h in range(N_HEADS)]
    for n, (c, h) in enumerate(units):
        sb = states[h].astype(BF16)
        v_new = u_all[n] - _dot(w_all[n], sb)
        vb = v_new.astype(BF16)
        o_s[c * C:(c + 1) * C, h * HEAD_V:(h + 1) * HEAD_V] = _dot(qd_all[n], sb) + _dot(qk_all[n], vb)
        states[h] = states[h] * egl[n] + _dot_tn(kd_all[n], vb)
    for h in range(N_HEADS):
        state_ref[h] = states[h]

    for h in range(N_HEADS):
        lo = h * HEAD_V
        zh = z_ref[:, lo:lo + HEAD_V].astype(F32)
        o_ref[:, lo:lo + HEAD_V] = (_rms(o_s[:, lo:lo + HEAD_V]) * gn_ref[...] * _silu(zh)).astype(BF16)


def _gdn(qkv, b, a, z, conv_w, a_log, dt_bias, gdn_norm, batch, seq):
    T = qkv.shape[0]
    ts = GDN_TS
    spt = seq // ts
    row = lambda w: pl.BlockSpec((ts, w), lambda bi, i: (bi * spt + i, 0))
    pad4 = lambda p: jnp.pad(p.astype(F32), (0, LANES - N_HEADS)).reshape(1, LANES)
    return pl.pallas_call(
        _gdn_kernel,
        grid=(batch, spt),
        in_specs=[row(GDN_CONV_W), row(LANES), row(LANES), row(GDN_V_W),
                  _const_spec((GDN_CONV, GDN_CONV_W)), _const_spec((1, LANES)), _const_spec((1, LANES)),
                  _const_spec((1, HEAD_V))],
        out_specs=row(GDN_V_W),
        out_shape=jax.ShapeDtypeStruct((T, GDN_V_W), BF16),
        scratch_shapes=[
            pltpu.VMEM((ts + 8, GDN_CONV_W), F32),
            pltpu.VMEM((ts, GDN_V_W), F32),
            pltpu.VMEM((N_HEADS, GDN_DK, HEAD_V), F32),
        ],
        compiler_params=_params("arbitrary", "arbitrary"),
        name="gated_deltanet",
    )(qkv, b, a, z, conv_w.astype(F32), pad4(a_log), pad4(dt_bias), gdn_norm.reshape(1, HEAD_V))


def _ret_kernel(q_ref, k_ref, v_ref, g_ref, o_ref, state_ref):
    i = pl.program_id(1)
    c = q_ref.shape[0]

    @pl.when(i == 0)
    def _():
        state_ref[...] = jnp.zeros_like(state_ref)

    ri = lax.broadcasted_iota(jnp.int32, (c, c), 0)
    ci = lax.broadcasted_iota(jnp.int32, (c, c), 1)
    rel = (ri - ci).astype(F32)
    pos = lax.broadcasted_iota(jnp.int32, (c, 1), 0).astype(F32)
    lane = lax.broadcasted_iota(jnp.int32, (1, LANES), 1)
    for h in range(N_HEADS):
        lg = math.log(1.0 - 2.0 ** (-5.0 - h))
        grp = (h // 2) * LANES
        own = ((lane >> 5) & 1) == h % 2
        qm = jnp.where(own, q_ref[:, grp:grp + LANES], 0)
        km = jnp.where(own, k_ref[:, grp:grp + LANES], 0)
        v = v_ref[:, h * HEAD_V:(h + 1) * HEAD_V]
        dmask = jnp.where(rel >= 0, jnp.exp(jnp.where(rel >= 0, rel, 0.0) * lg), 0.0)
        intra = _dot((_dot_nt(qm, km) * dmask).astype(BF16), v)
        state = state_ref[h]
        inter = _dot(qm, state.astype(BF16)) * jnp.exp((pos + 1.0) * lg)
        k_sc = (km.astype(F32) * jnp.exp((c - 1.0 - pos) * lg)).astype(BF16)
        state_ref[h] = state * math.exp(c * lg) + _dot_tn(k_sc, v)
        gate = _silu(g_ref[:, h * HEAD_V:(h + 1) * HEAD_V].astype(F32))
        o_ref[:, h * HEAD_V:(h + 1) * HEAD_V] = (_rms(intra + inter) * gate).astype(BF16)


def _retention(rq, rk, rv, rg, batch, seq):
    T = rq.shape[0]
    c = RET_C
    spt = seq // c
    row = lambda w: pl.BlockSpec((c, w), lambda bi, i: (bi * spt + i, 0))
    return pl.pallas_call(
        _ret_kernel,
        grid=(batch, spt),
        in_specs=[row(RET_QK_W), row(RET_QK_W), row(MIX_W), row(MIX_W)],
        out_specs=row(MIX_W),
        out_shape=jax.ShapeDtypeStruct((T, MIX_W), BF16),
        scratch_shapes=[pltpu.VMEM((N_HEADS, LANES, HEAD_V), F32)],
        compiler_params=_params("arbitrary", "arbitrary"),
        name="retention",
    )(rq, rk, rv, rg)


def _out_kernel(x_ref, a_ref, b_ref, w_ref, o_ref):
    o_ref[...] = (x_ref[...] + _dot(a_ref[...], w_ref[0:MIX_W, :])
                  + _dot(b_ref[...], w_ref[MIX_W:2 * MIX_W, :]))


def _out_proj(x2d, oa, ob, w_out):
    T = x2d.shape[0]
    tm = PROJ_TM
    row = lambda w: pl.BlockSpec((tm, w), lambda i: (i, 0))
    return pl.pallas_call(
        _out_kernel,
        grid=(T // tm,),
        in_specs=[row(D_MODEL), row(MIX_W), row(MIX_W), _const_spec((2 * MIX_W, D_MODEL))],
        out_specs=row(D_MODEL),
        out_shape=jax.ShapeDtypeStruct((T, D_MODEL), F32),
        compiler_params=_params("parallel"),
        name="out_proj",
    )(x2d, oa, ob, w_out.astype(BF16))


def _pack_w_in1(w_in):
    o = np.cumsum((0,) + (MIX_W,) * 6)
    cq, ck, cv, dq, dk, dv = (w_in[:, o[i]:o[i + 1]] for i in range(6))
    perm = _diff_perm()
    w = jnp.concatenate([cq, ck, cv, dq[:, perm], dk[:, perm]], axis=1).astype(BF16)
    return w, dv.T.astype(BF16)


def _in1_kernel(x_ref, n_ref, w_ref, wvt_ref, cosa_ref, sina_ref, cosb_ref, sinb_ref,
                cq_ref, ck_ref, cv_ref, dq_ref, dk_ref, dvt_ref):
    hb = (_rms(x_ref[...]) * n_ref[...]).astype(BF16)

    def proj(j):
        return _dot(hb, w_ref[:, j * MIX_W:(j + 1) * MIX_W])

    cq_ref[...] = _rope(proj(0), cosa_ref[...], sina_ref[...]) * DIL_DH ** -0.5
    ck_ref[...] = _rope(proj(1), cosa_ref[...], sina_ref[...])
    cv_ref[...] = proj(2)
    dq_ref[...] = (_rope(proj(3), cosb_ref[...], sinb_ref[...]) * DIFF_DK ** -0.5).astype(BF16)
    dk_ref[...] = _rope(proj(4), cosb_ref[...], sinb_ref[...]).astype(BF16)
    dvt_ref[...] = _dot_nt(wvt_ref[...], hb).astype(BF16)


def _in1(x2d, norm, w_packed, wv_t, cosa, sina, cosb, sinb, seq):
    T = x2d.shape[0]
    tm = DIFF_TK
    spt = seq // tm
    row = lambda w: pl.BlockSpec((tm, w), lambda i: (i, 0))
    tab = pl.BlockSpec((tm, LANES), lambda i: (i % spt, 0))
    out_shape = [jax.ShapeDtypeStruct((T, MIX_W), dt) for dt in (F32, F32, F32, BF16, BF16)]
    out_shape.append(jax.ShapeDtypeStruct((T // tm, MIX_W, tm), BF16))
    return pl.pallas_call(
        _in1_kernel,
        grid=(T // tm,),
        in_specs=[row(D_MODEL), _const_spec((1, D_MODEL)), _const_spec((D_MODEL, 5 * MIX_W)),
                  _const_spec((MIX_W, D_MODEL)), tab, tab, tab, tab],
        out_specs=[row(MIX_W)] * 5 + [pl.BlockSpec((None, MIX_W, tm), lambda i: (i, 0, 0))],
        out_shape=out_shape,
        compiler_params=_params("parallel"),
        name="in_proj1",
    )(x2d, norm.reshape(1, D_MODEL), w_packed, wv_t, cosa, sina, cosb, sinb)


def _dil_kernel(q_ref, kp_ref, kc_ref, vp_ref, vc_ref, o_ref, kcat, vcat, ob, mb, lb):
    i = pl.program_id(2)
    blk = q_ref.shape[0]
    nk = DIL_KEYS
    kcat[0:blk, :] = kp_ref[...]
    kcat[blk:2 * blk, :] = kc_ref[...]
    vcat[0:blk, :] = vp_ref[...]
    vcat[blk:2 * blk, :] = vc_ref[...]

    qi = lax.broadcasted_iota(jnp.int32, (nk, 2 * nk), 0)
    kj = lax.broadcasted_iota(jnp.int32, (nk, 2 * nk), 1)
    dist = qi + nk - kj
    band = (dist >= 0) & (dist <= nk)

    for gi, (_, d) in enumerate(DIL_PATTERNS):
        span = nk * d

        def unit(n, carry, gi=gi, d=d, span=span):
            sub = n // d
            r = n % d
            qs = sub * span + r
            ks = qs + blk - span
            q = q_ref[pl.ds(qs, nk, stride=d), :].astype(BF16)
            k = kcat[pl.ds(ks, 2 * nk, stride=d), :].astype(BF16)
            v = vcat[pl.ds(ks, 2 * nk, stride=d), :].astype(BF16)
            first_key = jnp.where((i > 0) | (sub > 0), 0, nk)
            s = jnp.where(band & (kj >= first_key), _dot_nt(q, k), -jnp.inf)
            m = jnp.max(s, axis=-1, keepdims=True)
            p = jnp.exp(s - m)
            l = jnp.sum(p, axis=-1, keepdims=True)
            ob[gi, pl.ds(qs, nk, stride=d), :] = _dot(p.astype(BF16), v)
            mb[gi, pl.ds(qs, nk, stride=d), :] = jnp.broadcast_to(m, (nk, LANES))
            lb[gi, pl.ds(qs, nk, stride=d), :] = jnp.broadcast_to(l, (nk, LANES))
            return carry

        lax.fori_loop(0, blk // nk, unit, 0, unroll=4)

    m_all = jnp.maximum(jnp.maximum(mb[0], mb[1]), mb[2])
    num = jnp.zeros((blk, HEAD_V), F32)
    den = jnp.zeros((blk, LANES), F32)
    for gi in range(len(DIL_PATTERNS)):
        sc = jnp.exp(mb[gi] - m_all)
        num = num + ob[gi] * sc
        den = den + lb[gi] * sc
    o_ref[...] = (num / den).astype(BF16)


def _dilated(cq, ck, cv, batch, seq):
    blk = DIL_BLK
    q3, k3, v3 = (t.reshape(batch, seq, MIX_W) for t in (cq, ck, cv))
    cur = pl.BlockSpec((None, blk, DIL_DH), lambda b, h, i: (b, i, h))
    prev = pl.BlockSpec((None, blk, DIL_DH), lambda b, h, i: (b, jnp.maximum(i - 1, 0), h))
    nb = len(DIL_PATTERNS)
    out = pl.pallas_call(
        _dil_kernel,
        grid=(batch, N_HEADS, seq // blk),
        in_specs=[cur, prev, cur, prev, cur],
        out_specs=cur,
        out_shape=jax.ShapeDtypeStruct((batch, seq, MIX_W), BF16),
        scratch_shapes=[
            pltpu.VMEM((2 * blk, DIL_DH), F32),
            pltpu.VMEM((2 * blk, DIL_DH), F32),
            pltpu.VMEM((nb, blk, HEAD_V), F32),
            pltpu.VMEM((nb, blk, LANES), F32),
            pltpu.VMEM((nb, blk, LANES), F32),
        ],
        compiler_params=_params("parallel", "parallel", "arbitrary"),
        name="dilated_attention",
    )(q3, k3, k3, v3, v3)
    return out.reshape(batch * seq, MIX_W)


def _diff_kernel(q_ref, k_ref, vt_ref, lq1_ref, lk1_ref, lq2_ref, lk2_ref, dn_ref, o_ref,
                 m_s, l_s, acc_s, *, lambda_init):
    i = pl.program_id(2)
    tq = q_ref.shape[0]
    tk = DIFF_TK
    lane = lax.broadcasted_iota(jnp.int32, (1, LANES), 1)
    first = ((lane >> 5) & 1) == 0
    q = q_ref[...]
    qq = jnp.concatenate([jnp.where(first, q, 0), jnp.where(first, 0, q)], axis=0)
    m_s[...] = jnp.full_like(m_s, -jnp.inf)
    l_s[...] = jnp.zeros_like(l_s)
    acc_s[...] = jnp.zeros_like(acc_s)

    def scores(kt):
        k0 = pl.multiple_of(kt * tk, tk)
        return _dot_nt(k_ref[pl.ds(k0, tk), :], qq)

    def update(s, kt, masked):
        if masked:
            key = lax.broadcasted_iota(jnp.int32, (tk, 2 * tq), 0)
            qry = lax.broadcasted_iota(jnp.int32, (tk, 2 * tq), 1) & (tq - 1)
            s = jnp.where(qry >= key, s, -jnp.inf)
        m_prev = m_s[...]
        m_new = jnp.maximum(m_prev, jnp.max(s, axis=0, keepdims=True))
        alpha = jnp.exp(m_prev - m_new)
        p = jnp.exp(s - m_new)
        l_s[...] = alpha * l_s[...] + jnp.sum(p, axis=0, keepdims=True)
        acc_s[...] = alpha * acc_s[...] + _dot(vt_ref[kt], p.astype(BF16))
        m_s[...] = m_new

    def body(kt, s_cur):
        s_next = scores(kt + 1)
        update(s_cur, kt, False)
        return s_next

    s_last = lax.fori_loop(0, i, body, scores(0))
    update(s_last, i, True)

    lam = (jnp.exp(jnp.sum(lq1_ref[...] * lk1_ref[...], axis=-1, keepdims=True))
           - jnp.exp(jnp.sum(lq2_ref[...] * lk2_ref[...], axis=-1, keepdims=True)) + lambda_init)
    o = acc_s[:, 0:tq] / l_s[:, 0:tq] - lam * (acc_s[:, tq:2 * tq] / l_s[:, tq:2 * tq])
    o = o * lax.rsqrt(jnp.mean(o * o, axis=0, keepdims=True) + EPS) * dn_ref[...] * (1.0 - lambda_init)
    o_ref[...] = o.T.astype(BF16)


def _differential(dq, dk, dvt, lq1, lk1, lq2, lk2, diff_norm, lambda_init, batch, seq):
    tq = DIFF_TQ
    nkt = seq // DIFF_TK
    q3, k3 = (t.reshape(batch, seq, MIX_W) for t in (dq, dk))
    qspec = pl.BlockSpec((None, tq, LANES), lambda b, h, i: (b, i, h))
    kspec = pl.BlockSpec((None, seq, LANES), lambda b, h, i: (b, 0, h))
    vspec = pl.BlockSpec((nkt, HEAD_V, DIFF_TK), lambda b, h, i: (b, h, 0))
    vec = lambda p: p.reshape(1, DIFF_DK).astype(F32)
    out = pl.pallas_call(
        functools.partial(_diff_kernel, lambda_init=lambda_init),
        grid=(batch, N_HEADS, seq // tq),
        in_specs=[qspec, kspec, vspec] + [_const_spec((1, DIFF_DK))] * 4 + [_const_spec((HEAD_V, 1))],
        out_specs=qspec,
        out_shape=jax.ShapeDtypeStruct((batch, seq, MIX_W), BF16),
        scratch_shapes=[
            pltpu.VMEM((1, 2 * tq), F32),
            pltpu.VMEM((1, 2 * tq), F32),
            pltpu.VMEM((HEAD_V, 2 * tq), F32),
        ],
        compiler_params=_params("parallel", "parallel", "arbitrary"),
        name="differential_attention",
    )(q3, k3, dvt, vec(lq1), vec(lk1), vec(lq2), vec(lk2), diff_norm.reshape(HEAD_V, 1))
    return out.reshape(batch * seq, MIX_W)


def kernel(x, l0_ffn1_norm, l0_ffn1_w_up, l0_ffn1_w_down, l0_mix_norm, l0_w_in, l0_conv_w, l0_a_log,
           l0_dt_bias, l0_gdn_norm, l0_w_out, l0_ffn2_norm, l0_ffn2_w_up, l0_ffn2_w_down,
           l1_ffn1_norm, l1_ffn1_w_up, l1_ffn1_w_down, l1_mix_norm, l1_w_in, l1_lambda_q1, l1_lambda_k1,
           l1_lambda_q2, l1_lambda_k2, l1_diff_norm, l1_w_out, l1_ffn2_norm, l1_ffn2_w_up, l1_ffn2_w_down,
           final_norm):
    batch, seq, _ = x.shape
    cos64, sin64 = _rope_tables(seq, 64)
    cos128, sin128 = _rope_tables(seq, 128)
    xf = x.reshape(batch * seq, D_MODEL)

    xf = _ffn(xf, l0_ffn1_norm, l0_ffn1_w_up, l0_ffn1_w_down)
    qkv, z, rq, rk, rv, rg, b, a = _in0(xf, l0_mix_norm, _pack_w_in0(l0_w_in), cos64, sin64, seq)
    o_a = _gdn(qkv, b, a, z, l0_conv_w, l0_a_log, l0_dt_bias, l0_gdn_norm, batch, seq)
    o_b = _retention(rq, rk, rv, rg, batch, seq)
    xf = _out_proj(xf, o_a, o_b, l0_w_out)
    xf = _ffn(xf, l0_ffn2_norm, l0_ffn2_w_up, l0_ffn2_w_down)

    xf = _ffn(xf, l1_ffn1_norm, l1_ffn1_w_up, l1_ffn1_w_down)
    w1, wv_t = _pack_w_in1(l1_w_in)
    cq, ck, cv, dq, dk, dvt = _in1(xf, l1_mix_norm, w1, wv_t, cos128, sin128, cos64, sin64, seq)
    o_c = _dilated(cq, ck, cv, batch, seq)
    lambda_init = 0.8 - 0.6 * math.exp(-0.3 * 1)
    o_d = _differential(dq, dk, dvt, l1_lambda_q1, l1_lambda_k1, l1_lambda_q2, l1_lambda_k2,
                        l1_diff_norm, lambda_init, batch, seq)
    xf = _out_proj(xf, o_c, o_d, l1_w_out)
    xf = _ffn(xf, l1_ffn2_norm, l1_ffn2_w_up, l1_ffn2_w_down, final_norm=final_norm)
    return xf.reshape(batch, seq, D_MODEL)
```

```python
import functools
import math

import numpy as np
import jax
import jax.numpy as jnp
from jax import lax
from jax.experimental import pallas as pl
from jax.experimental.pallas import tpu as pltpu

F32 = jnp.float32
BF16 = jnp.bfloat16
HIGHEST = lax.Precision.HIGHEST

EPS = 1e-6
ROPE_THETA = 10000.0
D_MODEL = 1024
D_FF = 2816
N_HEADS = 4
HEAD_V = 128
GDN_DK = 128
GDN_CONV = 4
GDN_CHUNK = 64
GDN_QK_W = N_HEADS * GDN_DK
GDN_V_W = N_HEADS * HEAD_V
GDN_CONV_W = 2 * GDN_QK_W + GDN_V_W
RET_DK = 64
RET_QK_W = N_HEADS * RET_DK
DIL_DH = 128
DIL_PATTERNS = ((128, 1), (512, 4), (2048, 16))
DIL_KEYS = 128
DIFF_DK = 64
MIX_W = N_HEADS * HEAD_V

LANES = 128
VMEM_LIMIT = 56 * 1024 * 1024

FFN_TM = 512
FFN_TF = 256
PROJ_TM = 512
GDN_TS = 256
RET_C = 256
DIL_GROUP = 8
DIL_BLK = 2048
DIFF_TQ = 512
DIFF_TK = 512
DIFF_HPS = 4
DIFF_GW = 256
LOG2E = math.log2(math.e)


def _rms(x):
    return x * lax.rsqrt(jnp.mean(x * x, axis=-1, keepdims=True) + EPS)


def _silu(x):
    return x * (1.0 / (1.0 + jnp.exp(-x)))


def _dot(a, b):
    return jnp.dot(a, b, preferred_element_type=F32)


def _dot_nt(a, b):
    return lax.dot_general(a, b, (((1,), (1,)), ((), ())), preferred_element_type=F32)


def _dot_tn(a, b):
    return lax.dot_general(a, b, (((0,), (0,)), ((), ())), preferred_element_type=F32)


def _params(*sem):
    return pltpu.CompilerParams(dimension_semantics=sem, vmem_limit_bytes=VMEM_LIMIT)


def _const_spec(shape):
    nd = len(shape)
    return pl.BlockSpec(shape, lambda *_: (0,) * nd)


def _ffn_kernel(x_ref, n_ref, wg_ref, wu_ref, wd_ref, fn_ref, o_ref, acc_ref, *, final):
    x = x_ref[...]
    hb = (_rms(x) * n_ref[...]).astype(BF16)
    nchunks = wg_ref.shape[0]
    for c in range(nchunks):
        g = _dot(hb, wg_ref[c])
        u = _dot(hb, wu_ref[c])
        a = (_silu(g) * u).astype(BF16)
        d = _dot(a, wd_ref[c])
        if c == 0:
            acc_ref[...] = d
        else:
            acc_ref[...] += d
    y = x + 0.5 * acc_ref[...]
    if final:
        y = _rms(y) * fn_ref[...]
    o_ref[...] = y


def _ffn(x2d, norm, w_up, w_down, final_norm=None):
    T = x2d.shape[0]
    nc = D_FF // FFN_TF
    wg = w_up[:, :D_FF].astype(BF16).reshape(D_MODEL, nc, FFN_TF).transpose(1, 0, 2)
    wu = w_up[:, D_FF:].astype(BF16).reshape(D_MODEL, nc, FFN_TF).transpose(1, 0, 2)
    wd = w_down.astype(BF16).reshape(nc, FFN_TF, D_MODEL)
    final = final_norm is not None
    fn = (final_norm if final else norm).reshape(1, D_MODEL)
    return pl.pallas_call(
        functools.partial(_ffn_kernel, final=final),
        grid=(T // FFN_TM,),
        in_specs=[
            pl.BlockSpec((FFN_TM, D_MODEL), lambda i: (i, 0)),
            _const_spec((1, D_MODEL)),
            _const_spec((nc, D_MODEL, FFN_TF)),
            _const_spec((nc, D_MODEL, FFN_TF)),
            _const_spec((nc, FFN_TF, D_MODEL)),
            _const_spec((1, D_MODEL)),
        ],
        out_specs=pl.BlockSpec((FFN_TM, D_MODEL), lambda i: (i, 0)),
        out_shape=jax.ShapeDtypeStruct((T, D_MODEL), F32),
        scratch_shapes=[pltpu.VMEM((FFN_TM, D_MODEL), F32)],
        compiler_params=_params("parallel"),
        name="ffn_final" if final else "ffn",
    )(x2d, norm.reshape(1, D_MODEL), wg, wu, wd, fn)


def _rope_tables(seq, dim):
    half = dim // 2
    inv = ROPE_THETA ** (-jnp.arange(0, dim, 2, dtype=F32) / dim)
    ang = jnp.arange(seq, dtype=F32)[:, None] * inv[None, :]
    reps = (LANES // 2) // half
    cos = jnp.tile(jnp.cos(ang), (1, 2 * reps))
    sin = jnp.tile(jnp.sin(ang), (1, reps))
    return cos, jnp.concatenate([-sin, sin], axis=1)


def _rope(x, cos, sin):
    outs = []
    for g in range(x.shape[1] // LANES):
        xg = x[:, g * LANES:(g + 1) * LANES]
        outs.append(xg * cos + pltpu.roll(xg, LANES // 2, 1) * sin)
    return outs[0] if len(outs) == 1 else jnp.concatenate(outs, axis=1)


_P0_QKV = (0, GDN_CONV_W)
_P0_Z = (_P0_QKV[1], _P0_QKV[1] + GDN_V_W)
_P0_RQ = (_P0_Z[1], _P0_Z[1] + RET_QK_W)
_P0_RK = (_P0_RQ[1], _P0_RQ[1] + RET_QK_W)
_P0_RV = (_P0_RK[1], _P0_RK[1] + MIX_W)
_P0_RG = (_P0_RV[1], _P0_RV[1] + MIX_W)
_P0_B = (_P0_RG[1], _P0_RG[1] + LANES)
_P0_A = (_P0_B[1], _P0_B[1] + LANES)
_P0_W = _P0_A[1]


def _ret_perm():
    perm = np.zeros(RET_QK_W, np.int32)
    for h in range(N_HEADS):
        for d in range(RET_DK):
            new = (h // 2) * LANES + (d // 32) * 64 + (h % 2) * 32 + d % 32
            perm[new] = h * RET_DK + d
    return perm


def _diff_perm():
    perm = np.zeros(N_HEADS * 2 * DIFF_DK, np.int32)
    for h in range(N_HEADS):
        for c in range(2):
            for d in range(DIFF_DK):
                new = h * LANES + (d // 32) * 64 + c * 32 + d % 32
                perm[new] = h * 2 * DIFF_DK + c * DIFF_DK + d
    return perm


def _pack_w_in0(w_in):
    o = np.cumsum((0, GDN_CONV_W, GDN_V_W, N_HEADS, N_HEADS, RET_QK_W, RET_QK_W, MIX_W, MIX_W))
    qkv, z, b, a, rq, rk, rv, rg = (w_in[:, o[i]:o[i + 1]] for i in range(8))
    perm = _ret_perm()
    pad = jnp.zeros((D_MODEL, LANES - N_HEADS), w_in.dtype)
    cols = [qkv, z, rq[:, perm], rk[:, perm], rv, rg, b, pad, a, pad]
    return jnp.concatenate(cols, axis=1).astype(BF16)


def _in0_kernel(x_ref, n_ref, w_ref, cos_ref, sin_ref,
                qkv_ref, z_ref, rq_ref, rk_ref, rv_ref, rg_ref, b_ref, a_ref):
    hb = (_rms(x_ref[...]) * n_ref[...]).astype(BF16)

    def proj(seg):
        return _dot(hb, w_ref[:, seg[0]:seg[1]])

    cos = cos_ref[...]
    sin = sin_ref[...]
    qkv_ref[...] = proj(_P0_QKV)
    z_ref[...] = proj(_P0_Z).astype(BF16)
    rq_ref[...] = (_rope(proj(_P0_RQ), cos, sin) * RET_DK ** -0.5).astype(BF16)
    rk_ref[...] = _rope(proj(_P0_RK), cos, sin).astype(BF16)
    rv_ref[...] = proj(_P0_RV).astype(BF16)
    rg_ref[...] = proj(_P0_RG).astype(BF16)
    b_ref[...] = proj(_P0_B)
    a_ref[...] = proj(_P0_A)


def _in0(x2d, norm, w_packed, cos, sin, seq):
    T = x2d.shape[0]
    tm = PROJ_TM
    spt = seq // tm
    row = lambda w: pl.BlockSpec((tm, w), lambda i: (i, 0))
    tab = pl.BlockSpec((tm, LANES), lambda i: (i % spt, 0))
    widths = (GDN_CONV_W, GDN_V_W, RET_QK_W, RET_QK_W, MIX_W, MIX_W, LANES, LANES)
    dtypes = (F32, BF16, BF16, BF16, BF16, BF16, F32, F32)
    return pl.pallas_call(
        _in0_kernel,
        grid=(T // tm,),
        in_specs=[row(D_MODEL), _const_spec((1, D_MODEL)), _const_spec((D_MODEL, _P0_W)), tab, tab],
        out_specs=[row(w) for w in widths],
        out_shape=[jax.ShapeDtypeStruct((T, w), dt) for w, dt in zip(widths, dtypes)],
        compiler_params=_params("parallel"),
        name="in_proj0",
    )(x2d, norm.reshape(1, D_MODEL), w_packed, cos, sin)


def _bmm(a, b):
    return jnp.einsum("gik,gkj->gij", a.astype(BF16), b.astype(BF16), preferred_element_type=F32)


def _unit_lower_inverse(a, ri, ci):
    n = jnp.where((ri >> 3) == (ci >> 3), -a, 0.0)
    t = jnp.where(ri == ci, 1.0, 0.0) + n
    p = _bmm(n, n)
    t = t + _bmm(t, p)
    p = _bmm(p, p)
    t = t + _bmm(t, p)
    for sh in (3, 4, 5):
        e = jnp.where(((ri >> (sh + 1)) == (ci >> (sh + 1))) & ((ri >> sh) != (ci >> sh)), a, 0.0)
        t = t - _bmm(_bmm(t, e), t)
    return t


def _gdn_kernel(qkv_ref, b_ref, a_ref, z_ref, cw_ref, alog_ref, dtb_ref, gn_ref, o_ref,
                xe_ref, o_s, state_ref):
    i = pl.program_id(1)
    ts = qkv_ref.shape[0]
    C = GDN_CHUNK
    nch = ts // C
    halo = 8

    @pl.when(i == 0)
    def _():
        xe_ref[0:halo, :] = jnp.zeros((halo, GDN_CONV_W), F32)
        state_ref[...] = jnp.zeros_like(state_ref)

    xe_ref[halo:halo + ts, :] = qkv_ref[...]
    conv = cw_ref[0:1, :] * xe_ref[halo - 3:halo - 3 + ts, :]
    for j in range(1, GDN_CONV):
        conv = conv + cw_ref[j:j + 1, :] * xe_ref[halo - 3 + j:halo - 3 + j + ts, :]
    xe_ref[0:halo, :] = xe_ref[ts:ts + halo, :]
    act = _silu(conv)
    qn, kn, vv = [], [], []
    for h in range(N_HEADS):
        lo = h * GDN_DK
        qh = act[:, lo:lo + GDN_DK]
        kh = act[:, GDN_QK_W + lo:GDN_QK_W + lo + GDN_DK]
        qn.append(qh * (lax.rsqrt(jnp.sum(qh * qh, -1, keepdims=True) + EPS) * GDN_DK ** -0.5))
        kn.append(kh * lax.rsqrt(jnp.sum(kh * kh, -1, keepdims=True) + EPS))
        vv.append(act[:, 2 * GDN_QK_W + h * HEAD_V:2 * GDN_QK_W + (h + 1) * HEAD_V])

    beta = 1.0 / (1.0 + jnp.exp(-b_ref[...]))
    xa = a_ref[...] + dtb_ref[...]
    softplus = jnp.maximum(xa, 0.0) + jnp.log(1.0 + jnp.exp(-jnp.abs(xa)))
    g = -jnp.exp(alog_ref[...]) * softplus
    rt = lax.broadcasted_iota(jnp.int32, (ts, ts), 0)
    ct = lax.broadcasted_iota(jnp.int32, (ts, ts), 1)
    tri = jnp.where((rt >= ct) & ((rt >> 6) == (ct >> 6)), 1.0, 0.0)
    gc = jnp.dot(tri, g, precision=HIGHEST, preferred_element_type=F32)

    units = [(c, h) for c in range(nch) for h in range(N_HEADS)]
    rows = lambda x, c: x[c * C:(c + 1) * C]
    stack = lambda f: jnp.stack([f(c, h) for c, h in units], axis=0)
    q = stack(lambda c, h: rows(qn[h], c))
    k = stack(lambda c, h: rows(kn[h], c))
    v = stack(lambda c, h: rows(vv[h], c))
    gct = [rows(gc, c).T for c in range(nch)]
    gcol = stack(lambda c, h: rows(gc, c)[:, h:h + 1])
    grow = stack(lambda c, h: gct[c][h:h + 1, :])
    bcol = stack(lambda c, h: rows(beta, c)[:, h:h + 1])
    glast = gcol[:, C - 1:C, :]

    ri = lax.broadcasted_iota(jnp.int32, (1, C, C), 1)
    ci = lax.broadcasted_iota(jnp.int32, (1, C, C), 2)
    causal = ri >= ci
    decay = jnp.where(causal, jnp.exp(jnp.where(causal, gcol - grow, 0.0)), 0.0)
    kb = k.astype(BF16)
    kk = jnp.einsum("gid,gjd->gij", kb, kb, preferred_element_type=F32)
    a_mat = jnp.where(ri > ci, kk * decay * bcol, 0.0)
    t_inv = _unit_lower_inverse(a_mat, ri, ci)
    eg = jnp.exp(gcol)
    rhs = jnp.concatenate([v * bcol, k * (bcol * eg)], axis=2).astype(BF16)
    sol = jnp.einsum("gik,gkj->gij", t_inv.astype(BF16), rhs, preferred_element_type=F32)
    solb = sol.astype(BF16)
    qk = (jnp.einsum("gid,gjd->gij", q.astype(BF16), kb, preferred_element_type=F32) * decay).astype(BF16)
    qk_uw = jnp.einsum("gij,gje->gie", qk, solb, preferred_element_type=F32)
    kd = k * jnp.exp(glast - gcol)
    kdt = jnp.stack([kd[n].T for n in range(len(units))], axis=0).astype(BF16)
    kd_uw = jnp.einsum("gdc,gce->gde", kdt, solb, preferred_element_type=F32)
    lhs_all = jnp.concatenate([kd_uw[:, :, HEAD_V:], q * eg - qk_uw[:, :, HEAD_V:]], axis=1).astype(BF16)
    c_all = kd_uw[:, :, :HEAD_V]
    o0_all = qk_uw[:, :, :HEAD_V]
    egl = jnp.exp(glast)

    states = [state_ref[h] for h in range(N_HEADS)]
    for n, (c, h) in enumerate(units):
        r = _dot(lhs_all[n], states[h].astype(BF16))
        o_s[c * C:(c + 1) * C, h * HEAD_V:(h + 1) * HEAD_V] = o0_all[n] + r[GDN_DK:]
        states[h] = states[h] * egl[n] + c_all[n] - r[:GDN_DK]
    for h in range(N_HEADS):
        state_ref[h] = states[h]

    for h in range(N_HEADS):
        lo = h * HEAD_V
        zh = z_ref[:, lo:lo + HEAD_V].astype(F32)
        o_ref[:, lo:lo + HEAD_V] = (_rms(o_s[:, lo:lo + HEAD_V]) * gn_ref[...] * _silu(zh)).astype(BF16)


def _gdn(qkv, b, a, z, conv_w, a_log, dt_bias, gdn_norm, batch, seq):
    T = qkv.shape[0]
    ts = GDN_TS
    spt = seq // ts
    row = lambda w: pl.BlockSpec((ts, w), lambda bi, i: (bi * spt + i, 0))
    pad4 = lambda p: jnp.pad(p.astype(F32), (0, LANES - N_HEADS)).reshape(1, LANES)
    return pl.pallas_call(
        _gdn_kernel,
        grid=(batch, spt),
        in_specs=[row(GDN_CONV_W), row(LANES), row(LANES), row(GDN_V_W),
                  _const_spec((GDN_CONV, GDN_CONV_W)), _const_spec((1, LANES)), _const_spec((1, LANES)),
                  _const_spec((1, HEAD_V))],
        out_specs=row(GDN_V_W),
        out_shape=jax.ShapeDtypeStruct((T, GDN_V_W), BF16),
        scratch_shapes=[
            pltpu.VMEM((ts + 8, GDN_CONV_W), F32),
            pltpu.VMEM((ts, GDN_V_W), F32),
            pltpu.VMEM((N_HEADS, GDN_DK, HEAD_V), F32),
        ],
        compiler_params=_params("arbitrary", "arbitrary"),
        name="gated_deltanet",
    )(qkv, b, a, z, conv_w.astype(F32), pad4(a_log), pad4(dt_bias), gdn_norm.reshape(1, HEAD_V))


def _ret_kernel(q_ref, k_ref, v_ref, g_ref, o_ref, state_ref):
    i = pl.program_id(1)
    c = q_ref.shape[0]

    @pl.when(i == 0)
    def _():
        state_ref[...] = jnp.zeros_like(state_ref)

    ri = lax.broadcasted_iota(jnp.int32, (c, c), 0)
    ci = lax.broadcasted_iota(jnp.int32, (c, c), 1)
    rel = (ri - ci).astype(F32)
    pos = lax.broadcasted_iota(jnp.int32, (c, 1), 0).astype(F32)
    lane = lax.broadcasted_iota(jnp.int32, (1, LANES), 1)
    for h in range(N_HEADS):
        lg = math.log(1.0 - 2.0 ** (-5.0 - h))
        grp = (h // 2) * LANES
        own = ((lane >> 5) & 1) == h % 2
        qm = jnp.where(own, q_ref[:, grp:grp + LANES], 0)
        km = jnp.where(own, k_ref[:, grp:grp + LANES], 0)
        v = v_ref[:, h * HEAD_V:(h + 1) * HEAD_V]
        dmask = jnp.where(rel >= 0, jnp.exp(jnp.where(rel >= 0, rel, 0.0) * lg), 0.0)
        intra = _dot((_dot_nt(qm, km) * dmask).astype(BF16), v)
        state = state_ref[h]
        inter = _dot(qm, state.astype(BF16)) * jnp.exp((pos + 1.0) * lg)
        k_sc = (km.astype(F32) * jnp.exp((c - 1.0 - pos) * lg)).astype(BF16)
        state_ref[h] = state * math.exp(c * lg) + _dot_tn(k_sc, v)
        gate = _silu(g_ref[:, h * HEAD_V:(h + 1) * HEAD_V].astype(F32))
        o_ref[:, h * HEAD_V:(h + 1) * HEAD_V] = (_rms(intra + inter) * gate).astype(BF16)


def _retention(rq, rk, rv, rg, batch, seq):
    T = rq.shape[0]
    c = RET_C
    spt = seq // c
    row = lambda w: pl.BlockSpec((c, w), lambda bi, i: (bi * spt + i, 0))
    return pl.pallas_call(
        _ret_kernel,
        grid=(batch, spt),
        in_specs=[row(RET_QK_W), row(RET_QK_W), row(MIX_W), row(MIX_W)],
        out_specs=row(MIX_W),
        out_shape=jax.ShapeDtypeStruct((T, MIX_W), BF16),
        scratch_shapes=[pltpu.VMEM((N_HEADS, LANES, HEAD_V), F32)],
        compiler_params=_params("arbitrary", "arbitrary"),
        name="retention",
    )(rq, rk, rv, rg)


def _out_kernel(x_ref, a_ref, b_ref, w_ref, o_ref):
    o_ref[...] = (x_ref[...] + _dot(a_ref[...], w_ref[0:MIX_W, :])
                  + _dot(b_ref[...], w_ref[MIX_W:2 * MIX_W, :]))


def _out_proj(x2d, oa, ob, w_out):
    T = x2d.shape[0]
    tm = PROJ_TM
    row = lambda w: pl.BlockSpec((tm, w), lambda i: (i, 0))
    return pl.pallas_call(
        _out_kernel,
        grid=(T // tm,),
        in_specs=[row(D_MODEL), row(MIX_W), row(MIX_W), _const_spec((2 * MIX_W, D_MODEL))],
        out_specs=row(D_MODEL),
        out_shape=jax.ShapeDtypeStruct((T, D_MODEL), F32),
        compiler_params=_params("parallel"),
        name="out_proj",
    )(x2d, oa, ob, w_out.astype(BF16))


def _pack_w_in1(w_in):
    o = np.cumsum((0,) + (MIX_W,) * 6)
    cq, ck, cv, dq, dk, dv = (w_in[:, o[i]:o[i + 1]] for i in range(6))
    perm = _diff_perm()
    w = jnp.concatenate([cq, ck, cv, dq[:, perm], dk[:, perm]], axis=1).astype(BF16)
    return w, dv.T.astype(BF16)


def _in1_kernel(x_ref, n_ref, w_ref, wvt_ref, cosa_ref, sina_ref, cosb_ref, sinb_ref,
                cq_ref, ck_ref, cv_ref, dq_ref, dk_ref, dvt_ref):
    hb = (_rms(x_ref[...]) * n_ref[...]).astype(BF16)

    def proj(j):
        return _dot(hb, w_ref[:, j * MIX_W:(j + 1) * MIX_W])

    cq_ref[...] = _rope(proj(0), cosa_ref[...], sina_ref[...]) * (DIL_DH ** -0.5 * LOG2E)
    ck_ref[...] = _rope(proj(1), cosa_ref[...], sina_ref[...])
    cv_ref[...] = proj(2)
    dq_ref[...] = (_rope(proj(3), cosb_ref[...], sinb_ref[...]) * (DIFF_DK ** -0.5 * LOG2E)).astype(BF16)
    dk_ref[...] = _rope(proj(4), cosb_ref[...], sinb_ref[...]).astype(BF16)
    vt = _dot_nt(wvt_ref[...], hb).astype(BF16)
    for j in range(dvt_ref.shape[0]):
        dvt_ref[j] = vt[:, j * DIFF_TK:(j + 1) * DIFF_TK]


def _in1(x2d, norm, w_packed, wv_t, cosa, sina, cosb, sinb, seq):
    T = x2d.shape[0]
    tm = PROJ_TM
    spt = seq // tm
    vblk = tm // DIFF_TK
    row = lambda w: pl.BlockSpec((tm, w), lambda i: (i, 0))
    tab = pl.BlockSpec((tm, LANES), lambda i: (i % spt, 0))
    out_shape = [jax.ShapeDtypeStruct((T, MIX_W), dt) for dt in (F32, F32, F32, BF16, BF16)]
    out_shape.append(jax.ShapeDtypeStruct((T // DIFF_TK, MIX_W, DIFF_TK), BF16))
    return pl.pallas_call(
        _in1_kernel,
        grid=(T // tm,),
        in_specs=[row(D_MODEL), _const_spec((1, D_MODEL)), _const_spec((D_MODEL, 5 * MIX_W)),
                  _const_spec((MIX_W, D_MODEL)), tab, tab, tab, tab],
        out_specs=[row(MIX_W)] * 5 + [pl.BlockSpec((vblk, MIX_W, DIFF_TK), lambda i: (i, 0, 0))],
        out_shape=out_shape,
        compiler_params=_params("parallel"),
        name="in_proj1",
    )(x2d, norm.reshape(1, D_MODEL), w_packed, wv_t, cosa, sina, cosb, sinb)


def _dil_kernel(q_ref, kp_ref, kc_ref, vp_ref, vc_ref, o_ref, kcat, vcat, ob, mb, lb):
    i = pl.program_id(2)
    blk = q_ref.shape[0]
    nk = DIL_KEYS
    kcat[0:blk, :] = kp_ref[...]
    kcat[blk:2 * blk, :] = kc_ref[...]
    vcat[0:blk, :] = vp_ref[...]
    vcat[blk:2 * blk, :] = vc_ref[...]

    qi = lax.broadcasted_iota(jnp.int32, (nk, 2 * nk), 0)
    kj = lax.broadcasted_iota(jnp.int32, (nk, 2 * nk), 1)
    dist = qi + nk - kj
    band = (dist >= 0) & (dist <= nk)

    for gi, (_, d) in enumerate(DIL_PATTERNS):
        span = nk * d

        def units(nb, carry, gi=gi, d=d, span=span):
            pos = []
            for u in range(DIL_GROUP):
                n = nb * DIL_GROUP + u
                sub = n // d
                qs = sub * span + n % d
                pos.append((qs, qs + blk - span, sub))
            qs_ = [q_ref[pl.ds(qs, nk, stride=d), :].astype(BF16) for qs, _, _ in pos]
            ks_ = [kcat[pl.ds(ks, 2 * nk, stride=d), :].astype(BF16) for _, ks, _ in pos]
            vs_ = [vcat[pl.ds(ks, 2 * nk, stride=d), :].astype(BF16) for _, ks, _ in pos]
            ss = [_dot_nt(q, k) for q, k in zip(qs_, ks_)]
            ps, ms, ls = [], [], []
            for s, (_, _, sub) in zip(ss, pos):
                first_key = jnp.where((i > 0) | (sub > 0), 0, nk)
                s = jnp.where(band & (kj >= first_key), s, -jnp.inf)
                m = jnp.max(s, axis=-1, keepdims=True)
                p = jnp.exp2(s - m)
                ms.append(m)
                ls.append(jnp.sum(p, axis=-1, keepdims=True))
                ps.append(p.astype(BF16))
            os_ = [_dot(p, v) for p, v in zip(ps, vs_)]
            for (qs, _, _), o, m, l in zip(pos, os_, ms, ls):
                ob[gi, pl.ds(qs, nk, stride=d), :] = o
                mb[gi, pl.ds(qs, nk, stride=d), :] = jnp.broadcast_to(m, (nk, LANES))
                lb[gi, pl.ds(qs, nk, stride=d), :] = jnp.broadcast_to(l, (nk, LANES))
            return carry

        lax.fori_loop(0, blk // nk // DIL_GROUP, units, 0)

    m_all = jnp.maximum(jnp.maximum(mb[0], mb[1]), mb[2])
    num = jnp.zeros((blk, HEAD_V), F32)
    den = jnp.zeros((blk, LANES), F32)
    for gi in range(len(DIL_PATTERNS)):
        sc = jnp.exp2(mb[gi] - m_all)
        num = num + ob[gi] * sc
        den = den + lb[gi] * sc
    o_ref[...] = (num / den).astype(BF16)


def _dilated(cq, ck, cv, batch, seq):
    blk = DIL_BLK
    q3, k3, v3 = (t.reshape(batch, seq, MIX_W) for t in (cq, ck, cv))
    cur = pl.BlockSpec((None, blk, DIL_DH), lambda b, h, i: (b, i, h))
    prev = pl.BlockSpec((None, blk, DIL_DH), lambda b, h, i: (b, jnp.maximum(i - 1, 0), h))
    nb = len(DIL_PATTERNS)
    out = pl.pallas_call(
        _dil_kernel,
        grid=(batch, N_HEADS, seq // blk),
        in_specs=[cur, prev, cur, prev, cur],
        out_specs=cur,
        out_shape=jax.ShapeDtypeStruct((batch, seq, MIX_W), BF16),
        scratch_shapes=[
            pltpu.VMEM((2 * blk, DIL_DH), F32),
            pltpu.VMEM((2 * blk, DIL_DH), F32),
            pltpu.VMEM((nb, blk, HEAD_V), F32),
            pltpu.VMEM((nb, blk, LANES), F32),
            pltpu.VMEM((nb, blk, LANES), F32),
        ],
        compiler_params=_params("parallel", "parallel", "arbitrary"),
        name="dilated_attention",
    )(q3, k3, k3, v3, v3)
    return out.reshape(batch * seq, MIX_W)


def _diff_kernel(q_ref, k_ref, vt_ref, lq1_ref, lk1_ref, lq2_ref, lk2_ref, dn_ref, o_ref,
                 *stats, lambda_init):
    i = pl.program_id(2)
    tq = q_ref.shape[0]
    tk = DIFF_TK
    gw = DIFF_GW
    ngrp = 2 * tq // gw
    nst = DIFF_HPS * ngrp
    m_s, l_s, acc_s = stats[0:nst], stats[nst:2 * nst], stats[2 * nst:3 * nst]
    lane = lax.broadcasted_iota(jnp.int32, (1, LANES), 1)
    first = ((lane >> 5) & 1) == 0
    qgs = []
    for hh in range(DIFF_HPS):
        q = q_ref[:, hh * LANES:(hh + 1) * LANES]
        qq = jnp.concatenate([jnp.where(first, q, 0), jnp.where(first, 0, q)], axis=0)
        qgs.append([qq[g * gw:(g + 1) * gw] for g in range(ngrp)])
    for r in stats[0:nst]:
        r[...] = jnp.full_like(r, -jnp.inf)
    for r in stats[nst:3 * nst]:
        r[...] = jnp.zeros_like(r)

    def scores(kt, hh):
        k0 = pl.multiple_of(kt * tk, tk)
        k_rows = k_ref[pl.ds(k0, tk), hh * LANES:(hh + 1) * LANES]
        return [_dot_nt(k_rows, qg) for qg in qgs[hh]]

    def update(ss, kt, hh, diag):
        vt = vt_ref[kt, hh * HEAD_V:(hh + 1) * HEAD_V, :]
        ps, alphas = [], []
        for g, s in enumerate(ss):
            n = hh * ngrp + g
            if diag is not None:
                key = lax.broadcasted_iota(jnp.int32, (tk, gw), 0) + diag * tk
                qry = (lax.broadcasted_iota(jnp.int32, (tk, gw), 1) + g * gw) & (tq - 1)
                s = jnp.where(qry >= key, s, -jnp.inf)
            m_prev = m_s[n][...]
            m_new = jnp.maximum(m_prev, jnp.max(s, axis=0, keepdims=True))
            alpha = jnp.exp2(m_prev - m_new)
            p = jnp.exp2(s - m_new)
            l_s[n][...] = alpha * l_s[n][...] + jnp.sum(p, axis=0, keepdims=True)
            m_s[n][...] = m_new
            ps.append(p.astype(BF16))
            alphas.append(alpha)
        pvs = [_dot(vt, p) for p in ps]
        for g in range(ngrp):
            n = hh * ngrp + g
            acc_s[n][...] = alphas[g] * acc_s[n][...] + pvs[g]

    per_q = tq // tk

    def step(j, diagonal):
        work = [(j * per_q + t, hh, t if diagonal else None) for t in range(per_q) for hh in range(DIFF_HPS)]
        sss = [scores(kt, hh) for kt, hh, _ in work]
        for ss, (kt, hh, diag) in zip(sss, work):
            update(ss, kt, hh, diag)

    def body(j, carry):
        step(j, False)
        return carry

    lax.fori_loop(0, i, body, 0)
    step(i, True)

    lam = (jnp.exp(jnp.sum(lq1_ref[...] * lk1_ref[...], axis=-1, keepdims=True))
           - jnp.exp(jnp.sum(lq2_ref[...] * lk2_ref[...], axis=-1, keepdims=True)) + lambda_init)
    half = ngrp // 2
    for hh in range(DIFF_HPS):
        for g in range(half):
            n1, n2 = hh * ngrp + g, hh * ngrp + g + half
            o = acc_s[n1][...] / l_s[n1][...] - lam * (acc_s[n2][...] / l_s[n2][...])
            o = o * lax.rsqrt(jnp.mean(o * o, axis=0, keepdims=True) + EPS) * dn_ref[...] * (1.0 - lambda_init)
            o_ref[g * gw:(g + 1) * gw, hh * HEAD_V:(hh + 1) * HEAD_V] = o.T.astype(BF16)


def _differential(dq, dk, dvt, lq1, lk1, lq2, lk2, diff_norm, lambda_init, batch, seq):
    tq = DIFF_TQ
    nkt = seq // DIFF_TK
    nst = DIFF_HPS * 2 * tq // DIFF_GW
    hw = DIFF_HPS * LANES
    q3, k3 = (t.reshape(batch, seq, MIX_W) for t in (dq, dk))
    qspec = pl.BlockSpec((None, tq, hw), lambda b, h, i: (b, i, h))
    kspec = pl.BlockSpec((None, seq, hw), lambda b, h, i: (b, 0, h))
    vspec = pl.BlockSpec((nkt, hw, DIFF_TK), lambda b, h, i: (b, h, 0))
    vec = lambda p: p.reshape(1, DIFF_DK).astype(F32)
    out = pl.pallas_call(
        functools.partial(_diff_kernel, lambda_init=lambda_init),
        grid=(batch, N_HEADS // DIFF_HPS, seq // tq),
        in_specs=[qspec, kspec, vspec] + [_const_spec((1, DIFF_DK))] * 4 + [_const_spec((HEAD_V, 1))],
        out_specs=qspec,
        out_shape=jax.ShapeDtypeStruct((batch, seq, MIX_W), BF16),
        scratch_shapes=([pltpu.VMEM((1, DIFF_GW), F32)] * (2 * nst)
                        + [pltpu.VMEM((HEAD_V, DIFF_GW), F32)] * nst),
        compiler_params=_params("parallel", "parallel", "arbitrary"),
        name="differential_attention",
    )(q3, k3, dvt, vec(lq1), vec(lk1), vec(lq2), vec(lk2), diff_norm.reshape(HEAD_V, 1))
    return out.reshape(batch * seq, MIX_W)


def kernel(x, l0_ffn1_norm, l0_ffn1_w_up, l0_ffn1_w_down, l0_mix_norm, l0_w_in, l0_conv_w, l0_a_log,
           l0_dt_bias, l0_gdn_norm, l0_w_out, l0_ffn2_norm, l0_ffn2_w_up, l0_ffn2_w_down,
           l1_ffn1_norm, l1_ffn1_w_up, l1_ffn1_w_down, l1_mix_norm, l1_w_in, l1_lambda_q1, l1_lambda_k1,
           l1_lambda_q2, l1_lambda_k2, l1_diff_norm, l1_w_out, l1_ffn2_norm, l1_ffn2_w_up, l1_ffn2_w_down,
           final_norm):
    batch, seq, _ = x.shape
    cos64, sin64 = _rope_tables(seq, 64)
    cos128, sin128 = _rope_tables(seq, 128)
    xf = x.reshape(batch * seq, D_MODEL)

    xf = _ffn(xf, l0_ffn1_norm, l0_ffn1_w_up, l0_ffn1_w_down)
    qkv, z, rq, rk, rv, rg, b, a = _in0(xf, l0_mix_norm, _pack_w_in0(l0_w_in), cos64, sin64, seq)
    o_a = _gdn(qkv, b, a, z, l0_conv_w, l0_a_log, l0_dt_bias, l0_gdn_norm, batch, seq)
    o_b = _retention(rq, rk, rv, rg, batch, seq)
    xf = _out_proj(xf, o_a, o_b, l0_w_out)
    xf = _ffn(xf, l0_ffn2_norm, l0_ffn2_w_up, l0_ffn2_w_down)

    xf = _ffn(xf, l1_ffn1_norm, l1_ffn1_w_up, l1_ffn1_w_down)
    w1, wv_t = _pack_w_in1(l1_w_in)
    cq, ck, cv, dq, dk, dvt = _in1(xf, l1_mix_norm, w1, wv_t, cos128, sin128, cos64, sin64, seq)
    o_c = _dilated(cq, ck, cv, batch, seq)
    lambda_init = 0.8 - 0.6 * math.exp(-0.3 * 1)
    o_d = _differential(dq, dk, dvt, l1_lambda_q1, l1_lambda_k1, l1_lambda_q2, l1_lambda_k2,
                        l1_diff_norm, lambda_init, batch, seq)
    xf = _out_proj(xf, o_c, o_d, l1_w_out)
    xf = _ffn(xf, l1_ffn2_norm, l1_ffn2_w_up, l1_ffn2_w_down, final_norm=final_norm)
    return xf.reshape(batch, seq, D_MODEL)
```

```python
import functools
import math

import numpy as np
import jax
import jax.numpy as jnp
from jax import lax
from jax.experimental import pallas as pl
from jax.experimental.pallas import tpu as pltpu

F32 = jnp.float32
BF16 = jnp.bfloat16
HIGHEST = lax.Precision.HIGHEST

EPS = 1e-6
ROPE_THETA = 10000.0
D_MODEL = 1024
D_FF = 2816
N_HEADS = 4
HEAD_V = 128
GDN_DK = 128
GDN_CONV = 4
GDN_CHUNK = 64
GDN_QK_W = N_HEADS * GDN_DK
GDN_V_W = N_HEADS * HEAD_V
GDN_CONV_W = 2 * GDN_QK_W + GDN_V_W
RET_DK = 64
RET_QK_W = N_HEADS * RET_DK
DIL_DH = 128
DIL_PATTERNS = ((128, 1), (512, 4), (2048, 16))
DIL_KEYS = 128
DIFF_DK = 64
MIX_W = N_HEADS * HEAD_V

LANES = 128
VMEM_LIMIT = 56 * 1024 * 1024

FFN_TM = 512
FFN_TF = 256
PROJ_TM = 512
GDN_TS = 256
RET_C = 256
DIL_GROUP = 8
DIL_BLK = 2048
DIFF_TQ = 512
DIFF_TK = 512
DIFF_HPS = 4
DIFF_GW = 256
LOG2E = math.log2(math.e)


def _rms(x):
    return x * lax.rsqrt(jnp.mean(x * x, axis=-1, keepdims=True) + EPS)


def _silu(x):
    return x * (1.0 / (1.0 + jnp.exp(-x)))


def _dot(a, b):
    return jnp.dot(a, b, preferred_element_type=F32)


def _dot_nt(a, b):
    return lax.dot_general(a, b, (((1,), (1,)), ((), ())), preferred_element_type=F32)


def _dot_tn(a, b):
    return lax.dot_general(a, b, (((0,), (0,)), ((), ())), preferred_element_type=F32)


def _params(*sem):
    return pltpu.CompilerParams(dimension_semantics=sem, vmem_limit_bytes=VMEM_LIMIT)


def _const_spec(shape):
    nd = len(shape)
    return pl.BlockSpec(shape, lambda *_: (0,) * nd)


def _ffn_kernel(*refs, mixed, final):
    refs = list(refs)
    x_ref = refs.pop(0)
    x = x_ref[...]
    if mixed:
        oa_ref, ob_ref, wo_ref = refs.pop(0), refs.pop(0), refs.pop(0)
        x = x + _dot(oa_ref[...], wo_ref[0:MIX_W, :]) + _dot(ob_ref[...], wo_ref[MIX_W:2 * MIX_W, :])
    n_ref, wup_ref, wd_ref = refs.pop(0), refs.pop(0), refs.pop(0)
    fn_ref = refs.pop(0) if final else None
    o_ref, acc_ref = refs
    hb = (_rms(x) * n_ref[...]).astype(BF16)
    for c in range(D_FF // FFN_TF):
        lo = c * FFN_TF
        g = _dot(hb, wup_ref[:, lo:lo + FFN_TF])
        u = _dot(hb, wup_ref[:, D_FF + lo:D_FF + lo + FFN_TF])
        a = (_silu(g) * u).astype(BF16)
        d = _dot(a, wd_ref[lo:lo + FFN_TF, :])
        if c == 0:
            acc_ref[...] = d
        else:
            acc_ref[...] += d
    y = x + 0.5 * acc_ref[...]
    if final:
        y = _rms(y) * fn_ref[...]
    o_ref[...] = y


def _ffn(x2d, norm, w_up, w_down, mix=None, final_norm=None):
    T = x2d.shape[0]
    row = lambda w: pl.BlockSpec((FFN_TM, w), lambda i: (i, 0))
    args, specs = [x2d], [row(D_MODEL)]
    if mix is not None:
        o_a, o_b, w_out = mix
        args += [o_a, o_b, w_out.astype(BF16)]
        specs += [row(MIX_W), row(MIX_W), _const_spec((2 * MIX_W, D_MODEL))]
    args += [norm.reshape(1, D_MODEL), w_up.astype(BF16), w_down.astype(BF16)]
    specs += [_const_spec((1, D_MODEL)), _const_spec((D_MODEL, 2 * D_FF)), _const_spec((D_FF, D_MODEL))]
    if final_norm is not None:
        args.append(final_norm.reshape(1, D_MODEL))
        specs.append(_const_spec((1, D_MODEL)))
    return pl.pallas_call(
        functools.partial(_ffn_kernel, mixed=mix is not None, final=final_norm is not None),
        grid=(T // FFN_TM,),
        in_specs=specs,
        out_specs=row(D_MODEL),
        out_shape=jax.ShapeDtypeStruct((T, D_MODEL), F32),
        scratch_shapes=[pltpu.VMEM((FFN_TM, D_MODEL), F32)],
        compiler_params=_params("parallel"),
        name="ffn_final" if final_norm is not None else ("ffn_mix" if mix is not None else "ffn"),
    )(*args)


def _rope_tables(seq, dim):
    half = dim // 2
    inv = ROPE_THETA ** (-jnp.arange(0, dim, 2, dtype=F32) / dim)
    ang = jnp.arange(seq, dtype=F32)[:, None] * inv[None, :]
    reps = (LANES // 2) // half
    cos = jnp.tile(jnp.cos(ang), (1, 2 * reps))
    sin = jnp.tile(jnp.sin(ang), (1, reps))
    return cos, jnp.concatenate([-sin, sin], axis=1)


def _rope(x, cos, sin):
    outs = []
    for g in range(x.shape[1] // LANES):
        xg = x[:, g * LANES:(g + 1) * LANES]
        outs.append(xg * cos + pltpu.roll(xg, LANES // 2, 1) * sin)
    return outs[0] if len(outs) == 1 else jnp.concatenate(outs, axis=1)


_P0_QKV = (0, GDN_CONV_W)
_P0_Z = (_P0_QKV[1], _P0_QKV[1] + GDN_V_W)
_P0_RQ = (_P0_Z[1], _P0_Z[1] + RET_QK_W)
_P0_RK = (_P0_RQ[1], _P0_RQ[1] + RET_QK_W)
_P0_RV = (_P0_RK[1], _P0_RK[1] + MIX_W)
_P0_RG = (_P0_RV[1], _P0_RV[1] + MIX_W)
_P0_B = (_P0_RG[1], _P0_RG[1] + LANES)
_P0_A = (_P0_B[1], _P0_B[1] + LANES)
_P0_W = _P0_A[1]


def _ret_perm():
    perm = np.zeros(RET_QK_W, np.int32)
    for h in range(N_HEADS):
        for d in range(RET_DK):
            new = (h // 2) * LANES + (d // 32) * 64 + (h % 2) * 32 + d % 32
            perm[new] = h * RET_DK + d
    return perm


def _diff_perm():
    perm = np.zeros(N_HEADS * 2 * DIFF_DK, np.int32)
    for h in range(N_HEADS):
        for c in range(2):
            for d in range(DIFF_DK):
                new = h * LANES + (d // 32) * 64 + c * 32 + d % 32
                perm[new] = h * 2 * DIFF_DK + c * DIFF_DK + d
    return perm


def _pack_w_in0(w_in):
    o = np.cumsum((0, GDN_CONV_W, GDN_V_W, N_HEADS, N_HEADS, RET_QK_W, RET_QK_W, MIX_W, MIX_W))
    qkv, z, b, a, rq, rk, rv, rg = (w_in[:, o[i]:o[i + 1]] for i in range(8))
    perm = _ret_perm()
    pad = jnp.zeros((D_MODEL, LANES - N_HEADS), w_in.dtype)
    cols = [qkv, z, rq[:, perm], rk[:, perm], rv, rg, b, pad, a, pad]
    return jnp.concatenate(cols, axis=1).astype(BF16)


def _in0_kernel(x_ref, n_ref, w_ref, cos_ref, sin_ref,
                qkv_ref, z_ref, rq_ref, rk_ref, rv_ref, rg_ref, b_ref, a_ref):
    hb = (_rms(x_ref[...]) * n_ref[...]).astype(BF16)

    def proj(seg):
        return _dot(hb, w_ref[:, seg[0]:seg[1]])

    cos = cos_ref[...]
    sin = sin_ref[...]
    qkv_ref[...] = proj(_P0_QKV)
    z_ref[...] = proj(_P0_Z).astype(BF16)
    rq_ref[...] = (_rope(proj(_P0_RQ), cos, sin) * RET_DK ** -0.5).astype(BF16)
    rk_ref[...] = _rope(proj(_P0_RK), cos, sin).astype(BF16)
    rv_ref[...] = proj(_P0_RV).astype(BF16)
    rg_ref[...] = proj(_P0_RG).astype(BF16)
    b_ref[...] = proj(_P0_B)
    a_ref[...] = proj(_P0_A)


def _in0(x2d, norm, w_packed, cos, sin, seq):
    T = x2d.shape[0]
    tm = PROJ_TM
    spt = seq // tm
    row = lambda w: pl.BlockSpec((tm, w), lambda i: (i, 0))
    tab = pl.BlockSpec((tm, LANES), lambda i: (i % spt, 0))
    widths = (GDN_CONV_W, GDN_V_W, RET_QK_W, RET_QK_W, MIX_W, MIX_W, LANES, LANES)
    dtypes = (F32, BF16, BF16, BF16, BF16, BF16, F32, F32)
    return pl.pallas_call(
        _in0_kernel,
        grid=(T // tm,),
        in_specs=[row(D_MODEL), _const_spec((1, D_MODEL)), _const_spec((D_MODEL, _P0_W)), tab, tab],
        out_specs=[row(w) for w in widths],
        out_shape=[jax.ShapeDtypeStruct((T, w), dt) for w, dt in zip(widths, dtypes)],
        compiler_params=_params("parallel"),
        name="in_proj0",
    )(x2d, norm.reshape(1, D_MODEL), w_packed, cos, sin)


def _bmm(a, b):
    return jnp.einsum("gik,gkj->gij", a.astype(BF16), b.astype(BF16), preferred_element_type=F32)


def _unit_lower_inverse(a, ri, ci):
    n = jnp.where((ri >> 3) == (ci >> 3), -a, 0.0)
    t = jnp.where(ri == ci, 1.0, 0.0) + n
    p = _bmm(n, n)
    t = t + _bmm(t, p)
    p = _bmm(p, p)
    t = t + _bmm(t, p)
    for sh in (3, 4, 5):
        e = jnp.where(((ri >> (sh + 1)) == (ci >> (sh + 1))) & ((ri >> sh) != (ci >> sh)), a, 0.0)
        t = t - _bmm(_bmm(t, e), t)
    return t


def _gdn_kernel(qkv_ref, b_ref, a_ref, z_ref, cw_ref, alog_ref, dtb_ref, gn_ref, o_ref,
                xe_ref, o_s, state_ref):
    i = pl.program_id(1)
    ts = qkv_ref.shape[0]
    C = GDN_CHUNK
    nch = ts // C
    halo = 8

    @pl.when(i == 0)
    def _():
        xe_ref[0:halo, :] = jnp.zeros((halo, GDN_CONV_W), F32)
        state_ref[...] = jnp.zeros_like(state_ref)

    xe_ref[halo:halo + ts, :] = qkv_ref[...]
    conv = cw_ref[0:1, :] * xe_ref[halo - 3:halo - 3 + ts, :]
    for j in range(1, GDN_CONV):
        conv = conv + cw_ref[j:j + 1, :] * xe_ref[halo - 3 + j:halo - 3 + j + ts, :]
    xe_ref[0:halo, :] = xe_ref[ts:ts + halo, :]
    act = _silu(conv)
    qn, kn, vv = [], [], []
    for h in range(N_HEADS):
        lo = h * GDN_DK
        qh = act[:, lo:lo + GDN_DK]
        kh = act[:, GDN_QK_W + lo:GDN_QK_W + lo + GDN_DK]
        qn.append(qh * (lax.rsqrt(jnp.sum(qh * qh, -1, keepdims=True) + EPS) * GDN_DK ** -0.5))
        kn.append(kh * lax.rsqrt(jnp.sum(kh * kh, -1, keepdims=True) + EPS))
        vv.append(act[:, 2 * GDN_QK_W + h * HEAD_V:2 * GDN_QK_W + (h + 1) * HEAD_V])

    beta = 1.0 / (1.0 + jnp.exp(-b_ref[...]))
    xa = a_ref[...] + dtb_ref[...]
    softplus = jnp.maximum(xa, 0.0) + jnp.log(1.0 + jnp.exp(-jnp.abs(xa)))
    g = -jnp.exp(alog_ref[...]) * softplus
    rt = lax.broadcasted_iota(jnp.int32, (ts, ts), 0)
    ct = lax.broadcasted_iota(jnp.int32, (ts, ts), 1)
    tri = jnp.where((rt >= ct) & ((rt >> 6) == (ct >> 6)), 1.0, 0.0)
    gc = jnp.dot(tri, g, precision=HIGHEST, preferred_element_type=F32)

    units = [(c, h) for c in range(nch) for h in range(N_HEADS)]
    rows = lambda x, c: x[c * C:(c + 1) * C]
    stack = lambda f: jnp.stack([f(c, h) for c, h in units], axis=0)
    q = stack(lambda c, h: rows(qn[h], c))
    k = stack(lambda c, h: rows(kn[h], c))
    v = stack(lambda c, h: rows(vv[h], c))
    gct = [rows(gc, c).T for c in range(nch)]
    gcol = stack(lambda c, h: rows(gc, c)[:, h:h + 1])
    grow = stack(lambda c, h: gct[c][h:h + 1, :])
    bcol = stack(lambda c, h: rows(beta, c)[:, h:h + 1])
    glast = gcol[:, C - 1:C, :]

    ri = lax.broadcasted_iota(jnp.int32, (1, C, C), 1)
    ci = lax.broadcasted_iota(jnp.int32, (1, C, C), 2)
    causal = ri >= ci
    decay = jnp.where(causal, jnp.exp(jnp.where(causal, gcol - grow, 0.0)), 0.0)
    kb = k.astype(BF16)
    kk = jnp.einsum("gid,gjd->gij", kb, kb, preferred_element_type=F32)
    a_mat = jnp.where(ri > ci, kk * decay * bcol, 0.0)
    t_inv = _unit_lower_inverse(a_mat, ri, ci)
    eg = jnp.exp(gcol)
    rhs = jnp.concatenate([v * bcol, k * (bcol * eg)], axis=2).astype(BF16)
    sol = jnp.einsum("gik,gkj->gij", t_inv.astype(BF16), rhs, preferred_element_type=F32)
    solb = sol.astype(BF16)
    qk = (jnp.einsum("gid,gjd->gij", q.astype(BF16), kb, preferred_element_type=F32) * decay).astype(BF16)
    qk_uw = jnp.einsum("gij,gje->gie", qk, solb, preferred_element_type=F32)
    kd = k * jnp.exp(glast - gcol)
    kdt = jnp.stack([kd[n].T for n in range(len(units))], axis=0).astype(BF16)
    kd_uw = jnp.einsum("gdc,gce->gde", kdt, solb, preferred_element_type=F32)
    lhs_all = jnp.concatenate([kd_uw[:, :, HEAD_V:], q * eg - qk_uw[:, :, HEAD_V:]], axis=1).astype(BF16)
    c_all = kd_uw[:, :, :HEAD_V]
    o0_all = qk_uw[:, :, :HEAD_V]
    egl = jnp.exp(glast)

    states = [state_ref[h] for h in range(N_HEADS)]
    for n, (c, h) in enumerate(units):
        r = _dot(lhs_all[n], states[h].astype(BF16))
        o_s[c * C:(c + 1) * C, h * HEAD_V:(h + 1) * HEAD_V] = o0_all[n] + r[GDN_DK:]
        states[h] = states[h] * egl[n] + c_all[n] - r[:GDN_DK]
    for h in range(N_HEADS):
        state_ref[h] = states[h]

    for h in range(N_HEADS):
        lo = h * HEAD_V
        zh = z_ref[:, lo:lo + HEAD_V].astype(F32)
        o_ref[:, lo:lo + HEAD_V] = (_rms(o_s[:, lo:lo + HEAD_V]) * gn_ref[...] * _silu(zh)).astype(BF16)


def _gdn(qkv, b, a, z, conv_w, a_log, dt_bias, gdn_norm, batch, seq):
    T = qkv.shape[0]
    ts = GDN_TS
    spt = seq // ts
    row = lambda w: pl.BlockSpec((ts, w), lambda bi, i: (bi * spt + i, 0))
    pad4 = lambda p: jnp.pad(p.astype(F32), (0, LANES - N_HEADS)).reshape(1, LANES)
    return pl.pallas_call(
        _gdn_kernel,
        grid=(batch, spt),
        in_specs=[row(GDN_CONV_W), row(LANES), row(LANES), row(GDN_V_W),
                  _const_spec((GDN_CONV, GDN_CONV_W)), _const_spec((1, LANES)), _const_spec((1, LANES)),
                  _const_spec((1, HEAD_V))],
        out_specs=row(GDN_V_W),
        out_shape=jax.ShapeDtypeStruct((T, GDN_V_W), BF16),
        scratch_shapes=[
            pltpu.VMEM((ts + 8, GDN_CONV_W), F32),
            pltpu.VMEM((ts, GDN_V_W), F32),
            pltpu.VMEM((N_HEADS, GDN_DK, HEAD_V), F32),
        ],
        compiler_params=_params("arbitrary", "arbitrary"),
        name="gated_deltanet",
    )(qkv, b, a, z, conv_w.astype(F32), pad4(a_log), pad4(dt_bias), gdn_norm.reshape(1, HEAD_V))


def _ret_kernel(q_ref, k_ref, v_ref, g_ref, o_ref, state_ref):
    i = pl.program_id(1)
    c = q_ref.shape[0]

    @pl.when(i == 0)
    def _():
        state_ref[...] = jnp.zeros_like(state_ref)

    ri = lax.broadcasted_iota(jnp.int32, (c, c), 0)
    ci = lax.broadcasted_iota(jnp.int32, (c, c), 1)
    rel = (ri - ci).astype(F32)
    pos = lax.broadcasted_iota(jnp.int32, (c, 1), 0).astype(F32)
    lane = lax.broadcasted_iota(jnp.int32, (1, LANES), 1)
    for h in range(N_HEADS):
        lg = math.log(1.0 - 2.0 ** (-5.0 - h))
        grp = (h // 2) * LANES
        own = ((lane >> 5) & 1) == h % 2
        qm = jnp.where(own, q_ref[:, grp:grp + LANES], 0)
        km = jnp.where(own, k_ref[:, grp:grp + LANES], 0)
        v = v_ref[:, h * HEAD_V:(h + 1) * HEAD_V]
        dmask = jnp.where(rel >= 0, jnp.exp(jnp.where(rel >= 0, rel, 0.0) * lg), 0.0)
        intra = _dot((_dot_nt(qm, km) * dmask).astype(BF16), v)
        state = state_ref[h]
        inter = _dot(qm, state.astype(BF16)) * jnp.exp((pos + 1.0) * lg)
        k_sc = (km.astype(F32) * jnp.exp((c - 1.0 - pos) * lg)).astype(BF16)
        state_ref[h] = state * math.exp(c * lg) + _dot_tn(k_sc, v)
        gate = _silu(g_ref[:, h * HEAD_V:(h + 1) * HEAD_V].astype(F32))
        o_ref[:, h * HEAD_V:(h + 1) * HEAD_V] = (_rms(intra + inter) * gate).astype(BF16)


def _retention(rq, rk, rv, rg, batch, seq):
    T = rq.shape[0]
    c = RET_C
    spt = seq // c
    row = lambda w: pl.BlockSpec((c, w), lambda bi, i: (bi * spt + i, 0))
    return pl.pallas_call(
        _ret_kernel,
        grid=(batch, spt),
        in_specs=[row(RET_QK_W), row(RET_QK_W), row(MIX_W), row(MIX_W)],
        out_specs=row(MIX_W),
        out_shape=jax.ShapeDtypeStruct((T, MIX_W), BF16),
        scratch_shapes=[pltpu.VMEM((N_HEADS, LANES, HEAD_V), F32)],
        compiler_params=_params("arbitrary", "arbitrary"),
        name="retention",
    )(rq, rk, rv, rg)


def _pack_w_in1(w_in):
    o = np.cumsum((0,) + (MIX_W,) * 6)
    cq, ck, cv, dq, dk, dv = (w_in[:, o[i]:o[i + 1]] for i in range(6))
    perm = _diff_perm()
    w = jnp.concatenate([cq, ck, cv, dq[:, perm], dk[:, perm]], axis=1).astype(BF16)
    return w, dv.T.astype(BF16)


def _in1_kernel(x_ref, n_ref, w_ref, wvt_ref, cosa_ref, sina_ref, cosb_ref, sinb_ref,
                cq_ref, ck_ref, cv_ref, dq_ref, dk_ref, dvt_ref):
    hb = (_rms(x_ref[...]) * n_ref[...]).astype(BF16)

    def proj(j):
        return _dot(hb, w_ref[:, j * MIX_W:(j + 1) * MIX_W])

    cq_ref[...] = _rope(proj(0), cosa_ref[...], sina_ref[...]) * (DIL_DH ** -0.5 * LOG2E)
    ck_ref[...] = _rope(proj(1), cosa_ref[...], sina_ref[...])
    cv_ref[...] = proj(2)
    dq_ref[...] = (_rope(proj(3), cosb_ref[...], sinb_ref[...]) * (DIFF_DK ** -0.5 * LOG2E)).astype(BF16)
    dk_ref[...] = _rope(proj(4), cosb_ref[...], sinb_ref[...]).astype(BF16)
    vt = _dot_nt(wvt_ref[...], hb).astype(BF16)
    for j in range(dvt_ref.shape[0]):
        dvt_ref[j] = vt[:, j * DIFF_TK:(j + 1) * DIFF_TK]


def _in1(x2d, norm, w_packed, wv_t, cosa, sina, cosb, sinb, seq):
    T = x2d.shape[0]
    tm = PROJ_TM
    spt = seq // tm
    vblk = tm // DIFF_TK
    row = lambda w: pl.BlockSpec((tm, w), lambda i: (i, 0))
    tab = pl.BlockSpec((tm, LANES), lambda i: (i % spt, 0))
    out_shape = [jax.ShapeDtypeStruct((T, MIX_W), dt) for dt in (F32, F32, F32, BF16, BF16)]
    out_shape.append(jax.ShapeDtypeStruct((T // DIFF_TK, MIX_W, DIFF_TK), BF16))
    return pl.pallas_call(
        _in1_kernel,
        grid=(T // tm,),
        in_specs=[row(D_MODEL), _const_spec((1, D_MODEL)), _const_spec((D_MODEL, 5 * MIX_W)),
                  _const_spec((MIX_W, D_MODEL)), tab, tab, tab, tab],
        out_specs=[row(MIX_W)] * 5 + [pl.BlockSpec((vblk, MIX_W, DIFF_TK), lambda i: (i, 0, 0))],
        out_shape=out_shape,
        compiler_params=_params("parallel"),
        name="in_proj1",
    )(x2d, norm.reshape(1, D_MODEL), w_packed, wv_t, cosa, sina, cosb, sinb)


def _dil_kernel(q_ref, kp_ref, kc_ref, vp_ref, vc_ref, o_ref, kcat, vcat, ob, mb, lb):
    i = pl.program_id(2)
    blk = q_ref.shape[0]
    nk = DIL_KEYS
    kcat[0:blk, :] = kp_ref[...]
    kcat[blk:2 * blk, :] = kc_ref[...]
    vcat[0:blk, :] = vp_ref[...]
    vcat[blk:2 * blk, :] = vc_ref[...]

    qi = lax.broadcasted_iota(jnp.int32, (nk, 2 * nk), 0)
    kj = lax.broadcasted_iota(jnp.int32, (nk, 2 * nk), 1)
    dist = qi + nk - kj
    band = (dist >= 0) & (dist <= nk)

    for gi, (_, d) in enumerate(DIL_PATTERNS):
        span = nk * d

        def units(nb, carry, gi=gi, d=d, span=span):
            pos = []
            for u in range(DIL_GROUP):
                n = nb * DIL_GROUP + u
                sub = n // d
                qs = sub * span + n % d
                pos.append((qs, qs + blk - span, sub))
            qs_ = [q_ref[pl.ds(qs, nk, stride=d), :].astype(BF16) for qs, _, _ in pos]
            ks_ = [kcat[pl.ds(ks, 2 * nk, stride=d), :].astype(BF16) for _, ks, _ in pos]
            vs_ = [vcat[pl.ds(ks, 2 * nk, stride=d), :].astype(BF16) for _, ks, _ in pos]
            ss = [_dot_nt(q, k) for q, k in zip(qs_, ks_)]
            ps, ms, ls = [], [], []
            for s, (_, _, sub) in zip(ss, pos):
                first_key = jnp.where((i > 0) | (sub > 0), 0, nk)
                s = jnp.where(band & (kj >= first_key), s, -jnp.inf)
                m = jnp.max(s, axis=-1, keepdims=True)
                p = jnp.exp2(s - m)
                ms.append(m)
                ls.append(jnp.sum(p, axis=-1, keepdims=True))
                ps.append(p.astype(BF16))
            os_ = [_dot(p, v) for p, v in zip(ps, vs_)]
            for (qs, _, _), o, m, l in zip(pos, os_, ms, ls):
                ob[gi, pl.ds(qs, nk, stride=d), :] = o
                mb[gi, pl.ds(qs, nk, stride=d), :] = jnp.broadcast_to(m, (nk, LANES))
                lb[gi, pl.ds(qs, nk, stride=d), :] = jnp.broadcast_to(l, (nk, LANES))
            return carry

        lax.fori_loop(0, blk // nk // DIL_GROUP, units, 0)

    m_all = jnp.maximum(jnp.maximum(mb[0], mb[1]), mb[2])
    num = jnp.zeros((blk, HEAD_V), F32)
    den = jnp.zeros((blk, LANES), F32)
    for gi in range(len(DIL_PATTERNS)):
        sc = jnp.exp2(mb[gi] - m_all)
        num = num + ob[gi] * sc
        den = den + lb[gi] * sc
    o_ref[...] = (num / den).astype(BF16)


def _dilated(cq, ck, cv, batch, seq):
    blk = DIL_BLK
    q3, k3, v3 = (t.reshape(batch, seq, MIX_W) for t in (cq, ck, cv))
    cur = pl.BlockSpec((None, blk, DIL_DH), lambda b, h, i: (b, i, h))
    prev = pl.BlockSpec((None, blk, DIL_DH), lambda b, h, i: (b, jnp.maximum(i - 1, 0), h))
    nb = len(DIL_PATTERNS)
    out = pl.pallas_call(
        _dil_kernel,
        grid=(batch, N_HEADS, seq // blk),
        in_specs=[cur, prev, cur, prev, cur],
        out_specs=cur,
        out_shape=jax.ShapeDtypeStruct((batch, seq, MIX_W), BF16),
        scratch_shapes=[
            pltpu.VMEM((2 * blk, DIL_DH), F32),
            pltpu.VMEM((2 * blk, DIL_DH), F32),
            pltpu.VMEM((nb, blk, HEAD_V), F32),
            pltpu.VMEM((nb, blk, LANES), F32),
            pltpu.VMEM((nb, blk, LANES), F32),
        ],
        compiler_params=_params("parallel", "parallel", "arbitrary"),
        name="dilated_attention",
    )(q3, k3, k3, v3, v3)
    return out.reshape(batch * seq, MIX_W)


def _diff_kernel(q_ref, k_ref, vt_ref, lq1_ref, lk1_ref, lq2_ref, lk2_ref, dn_ref, o_ref,
                 *stats, lambda_init):
    i = pl.program_id(2)
    tq = q_ref.shape[0]
    tk = DIFF_TK
    gw = DIFF_GW
    ngrp = 2 * tq // gw
    nst = DIFF_HPS * ngrp
    m_s, l_s, acc_s = stats[0:nst], stats[nst:2 * nst], stats[2 * nst:3 * nst]
    lane = lax.broadcasted_iota(jnp.int32, (1, LANES), 1)
    first = ((lane >> 5) & 1) == 0
    qgs = []
    for hh in range(DIFF_HPS):
        q = q_ref[:, hh * LANES:(hh + 1) * LANES]
        qq = jnp.concatenate([jnp.where(first, q, 0), jnp.where(first, 0, q)], axis=0)
        qgs.append([qq[g * gw:(g + 1) * gw] for g in range(ngrp)])
    for r in stats[0:nst]:
        r[...] = jnp.full_like(r, -jnp.inf)
    for r in stats[nst:3 * nst]:
        r[...] = jnp.zeros_like(r)

    def scores(kt, hh):
        k0 = pl.multiple_of(kt * tk, tk)
        k_rows = k_ref[pl.ds(k0, tk), hh * LANES:(hh + 1) * LANES]
        return [_dot_nt(k_rows, qg) for qg in qgs[hh]]

    def update(ss, kt, hh, diag):
        vt = vt_ref[kt, hh * HEAD_V:(hh + 1) * HEAD_V, :]
        ps, alphas = [], []
        for g, s in enumerate(ss):
            n = hh * ngrp + g
            if diag is not None:
                key = lax.broadcasted_iota(jnp.int32, (tk, gw), 0) + diag * tk
                qry = (lax.broadcasted_iota(jnp.int32, (tk, gw), 1) + g * gw) & (tq - 1)
                s = jnp.where(qry >= key, s, -jnp.inf)
            m_prev = m_s[n][...]
            m_new = jnp.maximum(m_prev, jnp.max(s, axis=0, keepdims=True))
            alpha = jnp.exp2(m_prev - m_new)
            p = jnp.exp2(s - m_new)
            l_s[n][...] = alpha * l_s[n][...] + jnp.sum(p, axis=0, keepdims=True)
            m_s[n][...] = m_new
            ps.append(p.astype(BF16))
            alphas.append(alpha)
        pvs = [_dot(vt, p) for p in ps]
        for g in range(ngrp):
            n = hh * ngrp + g
            acc_s[n][...] = alphas[g] * acc_s[n][...] + pvs[g]

    per_q = tq // tk

    def step(j, diagonal):
        work = [(j * per_q + t, hh, t if diagonal else None) for t in range(per_q) for hh in range(DIFF_HPS)]
        sss = [scores(kt, hh) for kt, hh, _ in work]
        for ss, (kt, hh, diag) in zip(sss, work):
            update(ss, kt, hh, diag)

    def body(j, carry):
        step(j, False)
        return carry

    lax.fori_loop(0, i, body, 0)
    step(i, True)

    lam = (jnp.exp(jnp.sum(lq1_ref[...] * lk1_ref[...], axis=-1, keepdims=True))
           - jnp.exp(jnp.sum(lq2_ref[...] * lk2_ref[...], axis=-1, keepdims=True)) + lambda_init)
    half = ngrp // 2
    for hh in range(DIFF_HPS):
        for g in range(half):
            n1, n2 = hh * ngrp + g, hh * ngrp + g + half
            o = acc_s[n1][...] / l_s[n1][...] - lam * (acc_s[n2][...] / l_s[n2][...])
            o = o * lax.rsqrt(jnp.mean(o * o, axis=0, keepdims=True) + EPS) * dn_ref[...] * (1.0 - lambda_init)
            o_ref[g * gw:(g + 1) * gw, hh * HEAD_V:(hh + 1) * HEAD_V] = o.T.astype(BF16)


def _differential(dq, dk, dvt, lq1, lk1, lq2, lk2, diff_norm, lambda_init, batch, seq):
    tq = DIFF_TQ
    nkt = seq // DIFF_TK
    nst = DIFF_HPS * 2 * tq // DIFF_GW
    hw = DIFF_HPS * LANES
    q3, k3 = (t.reshape(batch, seq, MIX_W) for t in (dq, dk))
    qspec = pl.BlockSpec((None, tq, hw), lambda b, h, i: (b, i, h))
    kspec = pl.BlockSpec((None, seq, hw), lambda b, h, i: (b, 0, h))
    vspec = pl.BlockSpec((nkt, hw, DIFF_TK), lambda b, h, i: (b, h, 0))
    vec = lambda p: p.reshape(1, DIFF_DK).astype(F32)
    out = pl.pallas_call(
        functools.partial(_diff_kernel, lambda_init=lambda_init),
        grid=(batch, N_HEADS // DIFF_HPS, seq // tq),
        in_specs=[qspec, kspec, vspec] + [_const_spec((1, DIFF_DK))] * 4 + [_const_spec((HEAD_V, 1))],
        out_specs=qspec,
        out_shape=jax.ShapeDtypeStruct((batch, seq, MIX_W), BF16),
        scratch_shapes=([pltpu.VMEM((1, DIFF_GW), F32)] * (2 * nst)
                        + [pltpu.VMEM((HEAD_V, DIFF_GW), F32)] * nst),
        compiler_params=_params("parallel", "parallel", "arbitrary"),
        name="differential_attention",
    )(q3, k3, dvt, vec(lq1), vec(lk1), vec(lq2), vec(lk2), diff_norm.reshape(HEAD_V, 1))
    return out.reshape(batch * seq, MIX_W)


def kernel(x, l0_ffn1_norm, l0_ffn1_w_up, l0_ffn1_w_down, l0_mix_norm, l0_w_in, l0_conv_w, l0_a_log,
           l0_dt_bias, l0_gdn_norm, l0_w_out, l0_ffn2_norm, l0_ffn2_w_up, l0_ffn2_w_down,
           l1_ffn1_norm, l1_ffn1_w_up, l1_ffn1_w_down, l1_mix_norm, l1_w_in, l1_lambda_q1, l1_lambda_k1,
           l1_lambda_q2, l1_lambda_k2, l1_diff_norm, l1_w_out, l1_ffn2_norm, l1_ffn2_w_up, l1_ffn2_w_down,
           final_norm):
    batch, seq, _ = x.shape
    cos64, sin64 = _rope_tables(seq, 64)
    cos128, sin128 = _rope_tables(seq, 128)
    xf = x.reshape(batch * seq, D_MODEL)

    xf = _ffn(xf, l0_ffn1_norm, l0_ffn1_w_up, l0_ffn1_w_down)
    qkv, z, rq, rk, rv, rg, b, a = _in0(xf, l0_mix_norm, _pack_w_in0(l0_w_in), cos64, sin64, seq)
    o_a = _gdn(qkv, b, a, z, l0_conv_w, l0_a_log, l0_dt_bias, l0_gdn_norm, batch, seq)
    o_b = _retention(rq, rk, rv, rg, batch, seq)
    xf = _ffn(xf, l0_ffn2_norm, l0_ffn2_w_up, l0_ffn2_w_down, mix=(o_a, o_b, l0_w_out))

    xf = _ffn(xf, l1_ffn1_norm, l1_ffn1_w_up, l1_ffn1_w_down)
    w1, wv_t = _pack_w_in1(l1_w_in)
    cq, ck, cv, dq, dk, dvt = _in1(xf, l1_mix_norm, w1, wv_t, cos128, sin128, cos64, sin64, seq)
    o_c = _dilated(cq, ck, cv, batch, seq)
    lambda_init = 0.8 - 0.6 * math.exp(-0.3 * 1)
    o_d = _differential(dq, dk, dvt, l1_lambda_q1, l1_lambda_k1, l1_lambda_q2, l1_lambda_k2,
                        l1_diff_norm, lambda_init, batch, seq)
    xf = _ffn(xf, l1_ffn2_norm, l1_ffn2_w_up, l1_ffn2_w_down, mix=(o_c, o_d, l1_w_out), final_norm=final_norm)
    return xf.reshape(batch, seq, D_MODEL)
```

```python
import functools
import math

import numpy as np
import jax
import jax.numpy as jnp
from jax import lax
from jax.experimental import pallas as pl
from jax.experimental.pallas import tpu as pltpu

F32 = jnp.float32
BF16 = jnp.bfloat16
HIGHEST = lax.Precision.HIGHEST

EPS = 1e-6
ROPE_THETA = 10000.0
D_MODEL = 1024
D_FF = 2816
N_HEADS = 4
HEAD_V = 128
GDN_DK = 128
GDN_CONV = 4
GDN_CHUNK = 64
GDN_QK_W = N_HEADS * GDN_DK
GDN_V_W = N_HEADS * HEAD_V
GDN_CONV_W = 2 * GDN_QK_W + GDN_V_W
RET_DK = 64
RET_QK_W = N_HEADS * RET_DK
DIL_DH = 128
DIL_PATTERNS = ((128, 1), (512, 4), (2048, 16))
DIL_KEYS = 128
DIFF_DK = 64
MIX_W = N_HEADS * HEAD_V

LANES = 128
VMEM_LIMIT = 56 * 1024 * 1024

FFN_TM = 512
FFN_TF = 256
PROJ_TM = 512
GDN_TS = 256
RET_C = 256
DIL_GROUP = 8
DIL_BLK = 2048
DIFF_TQ = 512
DIFF_TK = 512
DIFF_HPS = 4
DIFF_GW = 256
LOG2E = math.log2(math.e)


def _rms(x):
    return x * lax.rsqrt(jnp.mean(x * x, axis=-1, keepdims=True) + EPS)


def _silu(x):
    return x * (1.0 / (1.0 + jnp.exp(-x)))


def _dot(a, b):
    return jnp.dot(a, b, preferred_element_type=F32)


def _dot_nt(a, b):
    return lax.dot_general(a, b, (((1,), (1,)), ((), ())), preferred_element_type=F32)


def _dot_tn(a, b):
    return lax.dot_general(a, b, (((0,), (0,)), ((), ())), preferred_element_type=F32)


def _params(*sem):
    return pltpu.CompilerParams(dimension_semantics=sem, vmem_limit_bytes=VMEM_LIMIT)


def _const_spec(shape):
    nd = len(shape)
    return pl.BlockSpec(shape, lambda *_: (0,) * nd)


def _ffn_kernel(*refs, mixed, final):
    refs = list(refs)
    x_ref = refs.pop(0)
    x = x_ref[...]
    if mixed:
        oa_ref, ob_ref, wo_ref = refs.pop(0), refs.pop(0), refs.pop(0)
        x = x + _dot(oa_ref[...], wo_ref[0:MIX_W, :]) + _dot(ob_ref[...], wo_ref[MIX_W:2 * MIX_W, :])
    n_ref, wup_ref, wd_ref = refs.pop(0), refs.pop(0), refs.pop(0)
    fn_ref = refs.pop(0) if final else None
    o_ref, acc_ref = refs
    hb = (_rms(x) * n_ref[...]).astype(BF16)
    for c in range(D_FF // FFN_TF):
        lo = c * FFN_TF
        g = _dot(hb, wup_ref[:, lo:lo + FFN_TF])
        u = _dot(hb, wup_ref[:, D_FF + lo:D_FF + lo + FFN_TF])
        a = (_silu(g) * u).astype(BF16)
        d = _dot(a, wd_ref[lo:lo + FFN_TF, :])
        if c == 0:
            acc_ref[...] = d
        else:
            acc_ref[...] += d
    y = x + 0.5 * acc_ref[...]
    if final:
        y = _rms(y) * fn_ref[...]
    o_ref[...] = y


def _ffn(x2d, norm, w_up, w_down, mix=None, final_norm=None):
    T = x2d.shape[0]
    row = lambda w: pl.BlockSpec((FFN_TM, w), lambda i: (i, 0))
    args, specs = [x2d], [row(D_MODEL)]
    if mix is not None:
        o_a, o_b, w_out = mix
        args += [o_a, o_b, w_out.astype(BF16)]
        specs += [row(MIX_W), row(MIX_W), _const_spec((2 * MIX_W, D_MODEL))]
    args += [norm.reshape(1, D_MODEL), w_up.astype(BF16), w_down.astype(BF16)]
    specs += [_const_spec((1, D_MODEL)), _const_spec((D_MODEL, 2 * D_FF)), _const_spec((D_FF, D_MODEL))]
    if final_norm is not None:
        args.append(final_norm.reshape(1, D_MODEL))
        specs.append(_const_spec((1, D_MODEL)))
    return pl.pallas_call(
        functools.partial(_ffn_kernel, mixed=mix is not None, final=final_norm is not None),
        grid=(T // FFN_TM,),
        in_specs=specs,
        out_specs=row(D_MODEL),
        out_shape=jax.ShapeDtypeStruct((T, D_MODEL), F32),
        scratch_shapes=[pltpu.VMEM((FFN_TM, D_MODEL), F32)],
        compiler_params=_params("parallel"),
        name="ffn_final" if final_norm is not None else ("ffn_mix" if mix is not None else "ffn"),
    )(*args)


def _rope_tables(seq, dim):
    half = dim // 2
    inv = ROPE_THETA ** (-jnp.arange(0, dim, 2, dtype=F32) / dim)
    ang = jnp.arange(seq, dtype=F32)[:, None] * inv[None, :]
    reps = (LANES // 2) // half
    cos = jnp.tile(jnp.cos(ang), (1, 2 * reps))
    sin = jnp.tile(jnp.sin(ang), (1, reps))
    return cos, jnp.concatenate([-sin, sin], axis=1)


def _rope(x, cos, sin):
    outs = []
    for g in range(x.shape[1] // LANES):
        xg = x[:, g * LANES:(g + 1) * LANES]
        outs.append(xg * cos + pltpu.roll(xg, LANES // 2, 1) * sin)
    return outs[0] if len(outs) == 1 else jnp.concatenate(outs, axis=1)


_P0_QKV = (0, GDN_CONV_W)
_P0_Z = (_P0_QKV[1], _P0_QKV[1] + GDN_V_W)
_P0_RQ = (_P0_Z[1], _P0_Z[1] + RET_QK_W)
_P0_RK = (_P0_RQ[1], _P0_RQ[1] + RET_QK_W)
_P0_RV = (_P0_RK[1], _P0_RK[1] + MIX_W)
_P0_RG = (_P0_RV[1], _P0_RV[1] + MIX_W)
_P0_B = (_P0_RG[1], _P0_RG[1] + LANES)
_P0_A = (_P0_B[1], _P0_B[1] + LANES)
_P0_W = _P0_A[1]


def _ret_perm():
    perm = np.zeros(RET_QK_W, np.int32)
    for h in range(N_HEADS):
        for d in range(RET_DK):
            new = (h // 2) * LANES + (d // 32) * 64 + (h % 2) * 32 + d % 32
            perm[new] = h * RET_DK + d
    return perm


def _diff_perm():
    perm = np.zeros(N_HEADS * 2 * DIFF_DK, np.int32)
    for h in range(N_HEADS):
        for c in range(2):
            for d in range(DIFF_DK):
                new = h * LANES + (d // 32) * 64 + c * 32 + d % 32
                perm[new] = h * 2 * DIFF_DK + c * DIFF_DK + d
    return perm


def _pack_w_in0(w_in):
    o = np.cumsum((0, GDN_CONV_W, GDN_V_W, N_HEADS, N_HEADS, RET_QK_W, RET_QK_W, MIX_W, MIX_W))
    qkv, z, b, a, rq, rk, rv, rg = (w_in[:, o[i]:o[i + 1]] for i in range(8))
    perm = _ret_perm()
    pad = jnp.zeros((D_MODEL, LANES - N_HEADS), w_in.dtype)
    cols = [qkv, z, rq[:, perm], rk[:, perm], rv, rg, b, pad, a, pad]
    return jnp.concatenate(cols, axis=1).astype(BF16)


def _in0_kernel(x_ref, n_ref, w_ref, cos_ref, sin_ref,
                qkv_ref, z_ref, rq_ref, rk_ref, rv_ref, rg_ref, b_ref, a_ref):
    hb = (_rms(x_ref[...]) * n_ref[...]).astype(BF16)

    def proj(seg):
        return _dot(hb, w_ref[:, seg[0]:seg[1]])

    cos = cos_ref[...]
    sin = sin_ref[...]
    qkv_ref[...] = proj(_P0_QKV)
    z_ref[...] = proj(_P0_Z).astype(BF16)
    rq_ref[...] = (_rope(proj(_P0_RQ), cos, sin) * RET_DK ** -0.5).astype(BF16)
    rk_ref[...] = _rope(proj(_P0_RK), cos, sin).astype(BF16)
    rv_ref[...] = proj(_P0_RV).astype(BF16)
    rg_ref[...] = proj(_P0_RG).astype(BF16)
    b_ref[...] = proj(_P0_B)
    a_ref[...] = proj(_P0_A)


def _in0(x2d, norm, w_packed, cos, sin, seq):
    T = x2d.shape[0]
    tm = PROJ_TM
    spt = seq // tm
    row = lambda w: pl.BlockSpec((tm, w), lambda i: (i, 0))
    tab = pl.BlockSpec((tm, LANES), lambda i: (i % spt, 0))
    widths = (GDN_CONV_W, GDN_V_W, RET_QK_W, RET_QK_W, MIX_W, MIX_W, LANES, LANES)
    dtypes = (F32, BF16, BF16, BF16, BF16, BF16, F32, F32)
    return pl.pallas_call(
        _in0_kernel,
        grid=(T // tm,),
        in_specs=[row(D_MODEL), _const_spec((1, D_MODEL)), _const_spec((D_MODEL, _P0_W)), tab, tab],
        out_specs=[row(w) for w in widths],
        out_shape=[jax.ShapeDtypeStruct((T, w), dt) for w, dt in zip(widths, dtypes)],
        compiler_params=_params("parallel"),
        name="in_proj0",
    )(x2d, norm.reshape(1, D_MODEL), w_packed, cos, sin)


def _bmm(a, b):
    return jnp.einsum("gik,gkj->gij", a.astype(BF16), b.astype(BF16), preferred_element_type=F32)


def _unit_lower_inverse(a, ri, ci):
    n = jnp.where((ri >> 3) == (ci >> 3), -a, 0.0)
    t = jnp.where(ri == ci, 1.0, 0.0) + n
    p = _bmm(n, n)
    t = t + _bmm(t, p)
    p = _bmm(p, p)
    t = t + _bmm(t, p)
    for sh in (3, 4, 5):
        e = jnp.where(((ri >> (sh + 1)) == (ci >> (sh + 1))) & ((ri >> sh) != (ci >> sh)), a, 0.0)
        t = t - _bmm(_bmm(t, e), t)
    return t


def _gdn_kernel(qkv_ref, b_ref, a_ref, z_ref, cw_ref, alog_ref, dtb_ref, gn_ref, o_ref,
                xe_ref, o_s, state_ref):
    i = pl.program_id(1)
    ts = qkv_ref.shape[0]
    C = GDN_CHUNK
    nch = ts // C
    halo = 8

    @pl.when(i == 0)
    def _():
        xe_ref[0:halo, :] = jnp.zeros((halo, GDN_CONV_W), F32)
        state_ref[...] = jnp.zeros_like(state_ref)

    xe_ref[halo:halo + ts, :] = qkv_ref[...]
    conv = cw_ref[0:1, :] * xe_ref[halo - 3:halo - 3 + ts, :]
    for j in range(1, GDN_CONV):
        conv = conv + cw_ref[j:j + 1, :] * xe_ref[halo - 3 + j:halo - 3 + j + ts, :]
    xe_ref[0:halo, :] = xe_ref[ts:ts + halo, :]
    act = _silu(conv)
    qn, kn, vv = [], [], []
    for h in range(N_HEADS):
        lo = h * GDN_DK
        qh = act[:, lo:lo + GDN_DK]
        kh = act[:, GDN_QK_W + lo:GDN_QK_W + lo + GDN_DK]
        qn.append(qh * (lax.rsqrt(jnp.sum(qh * qh, -1, keepdims=True) + EPS) * GDN_DK ** -0.5))
        kn.append(kh * lax.rsqrt(jnp.sum(kh * kh, -1, keepdims=True) + EPS))
        vv.append(act[:, 2 * GDN_QK_W + h * HEAD_V:2 * GDN_QK_W + (h + 1) * HEAD_V])

    beta = 1.0 / (1.0 + jnp.exp(-b_ref[...]))
    xa = a_ref[...] + dtb_ref[...]
    softplus = jnp.maximum(xa, 0.0) + jnp.log(1.0 + jnp.exp(-jnp.abs(xa)))
    g = -jnp.exp(alog_ref[...]) * softplus
    rt = lax.broadcasted_iota(jnp.int32, (ts, ts), 0)
    ct = lax.broadcasted_iota(jnp.int32, (ts, ts), 1)
    tri = jnp.where((rt >= ct) & ((rt >> 6) == (ct >> 6)), 1.0, 0.0)
    gc = jnp.dot(tri, g, precision=HIGHEST, preferred_element_type=F32)

    units = [(c, h) for c in range(nch) for h in range(N_HEADS)]
    rows = lambda x, c: x[c * C:(c + 1) * C]
    stack = lambda f: jnp.stack([f(c, h) for c, h in units], axis=0)
    q = stack(lambda c, h: rows(qn[h], c))
    k = stack(lambda c, h: rows(kn[h], c))
    v = stack(lambda c, h: rows(vv[h], c))
    gct = [rows(gc, c).T for c in range(nch)]
    gcol = stack(lambda c, h: rows(gc, c)[:, h:h + 1])
    grow = stack(lambda c, h: gct[c][h:h + 1, :])
    bcol = stack(lambda c, h: rows(beta, c)[:, h:h + 1])
    glast = gcol[:, C - 1:C, :]

    ri = lax.broadcasted_iota(jnp.int32, (1, C, C), 1)
    ci = lax.broadcasted_iota(jnp.int32, (1, C, C), 2)
    causal = ri >= ci
    decay = jnp.where(causal, jnp.exp(jnp.where(causal, gcol - grow, 0.0)), 0.0)
    kb = k.astype(BF16)
    kk = jnp.einsum("gid,gjd->gij", kb, kb, preferred_element_type=F32)
    a_mat = jnp.where(ri > ci, kk * decay * bcol, 0.0)
    t_inv = _unit_lower_inverse(a_mat, ri, ci)
    eg = jnp.exp(gcol)
    rhs = jnp.concatenate([v * bcol, k * (bcol * eg)], axis=2).astype(BF16)
    sol = jnp.einsum("gik,gkj->gij", t_inv.astype(BF16), rhs, preferred_element_type=F32)
    solb = sol.astype(BF16)
    qk = (jnp.einsum("gid,gjd->gij", q.astype(BF16), kb, preferred_element_type=F32) * decay).astype(BF16)
    qk_uw = jnp.einsum("gij,gje->gie", qk, solb, preferred_element_type=F32)
    kd = k * jnp.exp(glast - gcol)
    kdt = jnp.stack([kd[n].T for n in range(len(units))], axis=0).astype(BF16)
    kd_uw = jnp.einsum("gdc,gce->gde", kdt, solb, preferred_element_type=F32)
    lhs_all = jnp.concatenate([kd_uw[:, :, HEAD_V:], q * eg - qk_uw[:, :, HEAD_V:]], axis=1).astype(BF16)
    c_all = kd_uw[:, :, :HEAD_V]
    o0_all = qk_uw[:, :, :HEAD_V]
    egl = jnp.exp(glast)

    states = [state_ref[h] for h in range(N_HEADS)]
    for n, (c, h) in enumerate(units):
        r = _dot(lhs_all[n], states[h].astype(BF16))
        o_s[c * C:(c + 1) * C, h * HEAD_V:(h + 1) * HEAD_V] = o0_all[n] + r[GDN_DK:]
        states[h] = states[h] * egl[n] + c_all[n] - r[:GDN_DK]
    for h in range(N_HEADS):
        state_ref[h] = states[h]

    for h in range(N_HEADS):
        lo = h * HEAD_V
        zh = z_ref[:, lo:lo + HEAD_V].astype(F32)
        o_ref[:, lo:lo + HEAD_V] = (_rms(o_s[:, lo:lo + HEAD_V]) * gn_ref[...] * _silu(zh)).astype(BF16)


def _gdn(qkv, b, a, z, conv_w, a_log, dt_bias, gdn_norm, batch, seq):
    T = qkv.shape[0]
    ts = GDN_TS
    spt = seq // ts
    row = lambda w: pl.BlockSpec((ts, w), lambda bi, i: (bi * spt + i, 0))
    pad4 = lambda p: jnp.pad(p.astype(F32), (0, LANES - N_HEADS)).reshape(1, LANES)
    return pl.pallas_call(
        _gdn_kernel,
        grid=(batch, spt),
        in_specs=[row(GDN_CONV_W), row(LANES), row(LANES), row(GDN_V_W),
                  _const_spec((GDN_CONV, GDN_CONV_W)), _const_spec((1, LANES)), _const_spec((1, LANES)),
                  _const_spec((1, HEAD_V))],
        out_specs=row(GDN_V_W),
        out_shape=jax.ShapeDtypeStruct((T, GDN_V_W), BF16),
        scratch_shapes=[
            pltpu.VMEM((ts + 8, GDN_CONV_W), F32),
            pltpu.VMEM((ts, GDN_V_W), F32),
            pltpu.VMEM((N_HEADS, GDN_DK, HEAD_V), F32),
        ],
        compiler_params=_params("arbitrary", "arbitrary"),
        name="gated_deltanet",
    )(qkv, b, a, z, conv_w.astype(F32), pad4(a_log), pad4(dt_bias), gdn_norm.reshape(1, HEAD_V))


def _ret_kernel(q_ref, k_ref, v_ref, g_ref, o_ref, state_ref):
    i = pl.program_id(1)
    c = q_ref.shape[0]

    @pl.when(i == 0)
    def _():
        state_ref[...] = jnp.zeros_like(state_ref)

    ri = lax.broadcasted_iota(jnp.int32, (c, c), 0)
    ci = lax.broadcasted_iota(jnp.int32, (c, c), 1)
    rel = (ri - ci).astype(F32)
    pos = lax.broadcasted_iota(jnp.int32, (c, 1), 0).astype(F32)
    lane = lax.broadcasted_iota(jnp.int32, (1, LANES), 1)
    for h in range(N_HEADS):
        lg = math.log(1.0 - 2.0 ** (-5.0 - h))
        grp = (h // 2) * LANES
        own = ((lane >> 5) & 1) == h % 2
        qm = jnp.where(own, q_ref[:, grp:grp + LANES], 0)
        km = jnp.where(own, k_ref[:, grp:grp + LANES], 0)
        v = v_ref[:, h * HEAD_V:(h + 1) * HEAD_V]
        dmask = jnp.where(rel >= 0, jnp.exp(jnp.where(rel >= 0, rel, 0.0) * lg), 0.0)
        intra = _dot((_dot_nt(qm, km) * dmask).astype(BF16), v)
        state = state_ref[h]
        inter = _dot(qm, state.astype(BF16)) * jnp.exp((pos + 1.0) * lg)
        k_sc = (km.astype(F32) * jnp.exp((c - 1.0 - pos) * lg)).astype(BF16)
        state_ref[h] = state * math.exp(c * lg) + _dot_tn(k_sc, v)
        gate = _silu(g_ref[:, h * HEAD_V:(h + 1) * HEAD_V].astype(F32))
        o_ref[:, h * HEAD_V:(h + 1) * HEAD_V] = (_rms(intra + inter) * gate).astype(BF16)


def _retention(rq, rk, rv, rg, batch, seq):
    T = rq.shape[0]
    c = RET_C
    spt = seq // c
    row = lambda w: pl.BlockSpec((c, w), lambda bi, i: (bi * spt + i, 0))
    return pl.pallas_call(
        _ret_kernel,
        grid=(batch, spt),
        in_specs=[row(RET_QK_W), row(RET_QK_W), row(MIX_W), row(MIX_W)],
        out_specs=row(MIX_W),
        out_shape=jax.ShapeDtypeStruct((T, MIX_W), BF16),
        scratch_shapes=[pltpu.VMEM((N_HEADS, LANES, HEAD_V), F32)],
        compiler_params=_params("arbitrary", "arbitrary"),
        name="retention",
    )(rq, rk, rv, rg)


def _pack_w_in1(w_in):
    o = np.cumsum((0,) + (MIX_W,) * 6)
    cq, ck, cv, dq, dk, dv = (w_in[:, o[i]:o[i + 1]] for i in range(6))
    perm = _diff_perm()
    w = jnp.concatenate([cq, ck, cv, dq[:, perm], dk[:, perm]], axis=1).astype(BF16)
    return w, dv.T.astype(BF16)


def _in1_kernel(x_ref, n_ref, w_ref, wvt_ref, cosa_ref, sina_ref, cosb_ref, sinb_ref,
                cq_ref, ck_ref, cv_ref, dq_ref, dk_ref, dvt_ref):
    hb = (_rms(x_ref[...]) * n_ref[...]).astype(BF16)

    def proj(j):
        return _dot(hb, w_ref[:, j * MIX_W:(j + 1) * MIX_W])

    cq_ref[...] = _rope(proj(0), cosa_ref[...], sina_ref[...]) * (DIL_DH ** -0.5 * LOG2E)
    ck_ref[...] = _rope(proj(1), cosa_ref[...], sina_ref[...])
    cv_ref[...] = proj(2)
    dq_ref[...] = (_rope(proj(3), cosb_ref[...], sinb_ref[...]) * (DIFF_DK ** -0.5 * LOG2E)).astype(BF16)
    dk_ref[...] = _rope(proj(4), cosb_ref[...], sinb_ref[...]).astype(BF16)
    vt = _dot_nt(wvt_ref[...], hb).astype(BF16)
    for j in range(dvt_ref.shape[0]):
        dvt_ref[j] = vt[:, j * DIFF_TK:(j + 1) * DIFF_TK]


def _in1(x2d, norm, w_packed, wv_t, cosa, sina, cosb, sinb, seq):
    T = x2d.shape[0]
    tm = PROJ_TM
    spt = seq // tm
    vblk = tm // DIFF_TK
    row = lambda w: pl.BlockSpec((tm, w), lambda i: (i, 0))
    tab = pl.BlockSpec((tm, LANES), lambda i: (i % spt, 0))
    out_shape = [jax.ShapeDtypeStruct((T, MIX_W), dt) for dt in (F32, F32, F32, BF16, BF16)]
    out_shape.append(jax.ShapeDtypeStruct((T // DIFF_TK, MIX_W, DIFF_TK), BF16))
    return pl.pallas_call(
        _in1_kernel,
        grid=(T // tm,),
        in_specs=[row(D_MODEL), _const_spec((1, D_MODEL)), _const_spec((D_MODEL, 5 * MIX_W)),
                  _const_spec((MIX_W, D_MODEL)), tab, tab, tab, tab],
        out_specs=[row(MIX_W)] * 5 + [pl.BlockSpec((vblk, MIX_W, DIFF_TK), lambda i: (i, 0, 0))],
        out_shape=out_shape,
        compiler_params=_params("parallel"),
        name="in_proj1",
    )(x2d, norm.reshape(1, D_MODEL), w_packed, wv_t, cosa, sina, cosb, sinb)


def _dil_kernel(q_ref, kp_ref, kc_ref, vp_ref, vc_ref, o_ref, kcat, vcat, ob, mb, lb):
    i = pl.program_id(2)
    blk = q_ref.shape[0]
    nk = DIL_KEYS
    kcat[0:blk, :] = kp_ref[...]
    kcat[blk:2 * blk, :] = kc_ref[...]
    vcat[0:blk, :] = vp_ref[...]
    vcat[blk:2 * blk, :] = vc_ref[...]

    qi = lax.broadcasted_iota(jnp.int32, (nk, 2 * nk), 0)
    kj = lax.broadcasted_iota(jnp.int32, (nk, 2 * nk), 1)
    dist = qi + nk - kj
    band = (dist >= 0) & (dist <= nk)
    ones_v = jnp.ones((2 * nk, LANES), BF16)

    for gi, (_, d) in enumerate(DIL_PATTERNS):
        span = nk * d

        def units(nb, carry, gi=gi, d=d, span=span):
            pos = []
            for u in range(DIL_GROUP):
                n = nb * DIL_GROUP + u
                sub = n // d
                qs = sub * span + n % d
                pos.append((qs, qs + blk - span, sub))
            qs_ = [q_ref[pl.ds(qs, nk, stride=d), :].astype(BF16) for qs, _, _ in pos]
            ks_ = [kcat[pl.ds(ks, 2 * nk, stride=d), :].astype(BF16) for _, ks, _ in pos]
            vs_ = [jnp.concatenate([vcat[pl.ds(ks, 2 * nk, stride=d), :].astype(BF16), ones_v], axis=1)
                   for _, ks, _ in pos]
            ss = [_dot_nt(q, k) for q, k in zip(qs_, ks_)]
            ps, ms = [], []
            for s, (_, _, sub) in zip(ss, pos):
                first_key = jnp.where((i > 0) | (sub > 0), 0, nk)
                sb = jnp.where(band & (kj >= first_key), s.astype(BF16), -jnp.inf)
                m = jnp.max(sb, axis=-1, keepdims=True)
                ps.append(jnp.exp2(sb - m))
                ms.append(m.astype(F32))
            os_ = [_dot(p, v) for p, v in zip(ps, vs_)]
            for (qs, _, _), o, m in zip(pos, os_, ms):
                ob[gi, pl.ds(qs, nk, stride=d), :] = o[:, :HEAD_V]
                lb[gi, pl.ds(qs, nk, stride=d), :] = o[:, HEAD_V:]
                mb[gi, pl.ds(qs, nk, stride=d), :] = jnp.broadcast_to(m, (nk, LANES))
            return carry

        lax.fori_loop(0, blk // nk // DIL_GROUP, units, 0)

    m_all = jnp.maximum(jnp.maximum(mb[0], mb[1]), mb[2])
    num = jnp.zeros((blk, HEAD_V), F32)
    den = jnp.zeros((blk, LANES), F32)
    for gi in range(len(DIL_PATTERNS)):
        sc = jnp.exp2(mb[gi] - m_all)
        num = num + ob[gi] * sc
        den = den + lb[gi] * sc
    o_ref[...] = (num / den).astype(BF16)


def _dilated(cq, ck, cv, batch, seq):
    blk = DIL_BLK
    q3, k3, v3 = (t.reshape(batch, seq, MIX_W) for t in (cq, ck, cv))
    cur = pl.BlockSpec((None, blk, DIL_DH), lambda b, h, i: (b, i, h))
    prev = pl.BlockSpec((None, blk, DIL_DH), lambda b, h, i: (b, jnp.maximum(i - 1, 0), h))
    nb = len(DIL_PATTERNS)
    out = pl.pallas_call(
        _dil_kernel,
        grid=(batch, N_HEADS, seq // blk),
        in_specs=[cur, prev, cur, prev, cur],
        out_specs=cur,
        out_shape=jax.ShapeDtypeStruct((batch, seq, MIX_W), BF16),
        scratch_shapes=[
            pltpu.VMEM((2 * blk, DIL_DH), F32),
            pltpu.VMEM((2 * blk, DIL_DH), F32),
            pltpu.VMEM((nb, blk, HEAD_V), F32),
            pltpu.VMEM((nb, blk, LANES), F32),
            pltpu.VMEM((nb, blk, LANES), F32),
        ],
        compiler_params=_params("parallel", "parallel", "arbitrary"),
        name="dilated_attention",
    )(q3, k3, k3, v3, v3)
    return out.reshape(batch * seq, MIX_W)


def _diff_kernel(q_ref, k_ref, vt_ref, lq1_ref, lk1_ref, lq2_ref, lk2_ref, dn_ref, o_ref,
                 *stats, lambda_init):
    i = pl.program_id(2)
    tq = q_ref.shape[0]
    tk = DIFF_TK
    gw = DIFF_GW
    ngrp = 2 * tq // gw
    nst = DIFF_HPS * ngrp
    m_s, acc_s = stats[0:nst], stats[nst:2 * nst]
    lane = lax.broadcasted_iota(jnp.int32, (1, LANES), 1)
    first = ((lane >> 5) & 1) == 0
    qgs = []
    for hh in range(DIFF_HPS):
        q = q_ref[:, hh * LANES:(hh + 1) * LANES]
        qq = jnp.concatenate([jnp.where(first, q, 0), jnp.where(first, 0, q)], axis=0)
        qgs.append([qq[g * gw:(g + 1) * gw] for g in range(ngrp)])
    for r in m_s:
        r[...] = jnp.full_like(r, -jnp.inf)
    for r in acc_s:
        r[...] = jnp.zeros_like(r)

    def scores(kt, hh):
        k0 = pl.multiple_of(kt * tk, tk)
        k_rows = k_ref[pl.ds(k0, tk), hh * LANES:(hh + 1) * LANES]
        return [_dot_nt(k_rows, qg) for qg in qgs[hh]]

    def update(ss, kt, hh, diag):
        vt = jnp.concatenate([vt_ref[kt, hh * HEAD_V:(hh + 1) * HEAD_V, :], jnp.ones((8, tk), BF16)], axis=0)
        ps, alphas = [], []
        for g, s in enumerate(ss):
            n = hh * ngrp + g
            sb = s.astype(BF16)
            if diag is not None:
                key = lax.broadcasted_iota(jnp.int32, (tk, gw), 0) + diag * tk
                qry = (lax.broadcasted_iota(jnp.int32, (tk, gw), 1) + g * gw) & (tq - 1)
                sb = jnp.where(qry >= key, sb, -jnp.inf)
            m_prev = m_s[n][...]
            m_new = jnp.maximum(m_prev, jnp.max(sb, axis=0, keepdims=True).astype(F32))
            alphas.append(jnp.exp2(m_prev - m_new))
            ps.append(jnp.exp2(sb - m_new.astype(BF16)))
            m_s[n][...] = m_new
        pvs = [_dot(vt, p) for p in ps]
        for g in range(ngrp):
            n = hh * ngrp + g
            acc_s[n][...] = alphas[g] * acc_s[n][...] + pvs[g]

    per_q = tq // tk

    def step(j, diagonal):
        work = [(j * per_q + t, hh, t if diagonal else None) for t in range(per_q) for hh in range(DIFF_HPS)]
        sss = [scores(kt, hh) for kt, hh, _ in work]
        for ss, (kt, hh, diag) in zip(sss, work):
            update(ss, kt, hh, diag)

    def body(j, carry):
        step(j, False)
        return carry

    lax.fori_loop(0, i, body, 0)
    step(i, True)

    lam = (jnp.exp(jnp.sum(lq1_ref[...] * lk1_ref[...], axis=-1, keepdims=True))
           - jnp.exp(jnp.sum(lq2_ref[...] * lk2_ref[...], axis=-1, keepdims=True)) + lambda_init)
    half = ngrp // 2
    for hh in range(DIFF_HPS):
        for g in range(half):
            n1, n2 = hh * ngrp + g, hh * ngrp + g + half
            a1, a2 = acc_s[n1][...], acc_s[n2][...]
            o = a1[:HEAD_V] / a1[HEAD_V:HEAD_V + 1] - lam * (a2[:HEAD_V] / a2[HEAD_V:HEAD_V + 1])
            o = o * lax.rsqrt(jnp.mean(o * o, axis=0, keepdims=True) + EPS) * dn_ref[...] * (1.0 - lambda_init)
            o_ref[g * gw:(g + 1) * gw, hh * HEAD_V:(hh + 1) * HEAD_V] = o.T.astype(BF16)


def _differential(dq, dk, dvt, lq1, lk1, lq2, lk2, diff_norm, lambda_init, batch, seq):
    tq = DIFF_TQ
    nkt = seq // DIFF_TK
    nst = DIFF_HPS * 2 * tq // DIFF_GW
    hw = DIFF_HPS * LANES
    q3, k3 = (t.reshape(batch, seq, MIX_W) for t in (dq, dk))
    qspec = pl.BlockSpec((None, tq, hw), lambda b, h, i: (b, i, h))
    kspec = pl.BlockSpec((None, seq, hw), lambda b, h, i: (b, 0, h))
    vspec = pl.BlockSpec((nkt, hw, DIFF_TK), lambda b, h, i: (b, h, 0))
    vec = lambda p: p.reshape(1, DIFF_DK).astype(F32)
    out = pl.pallas_call(
        functools.partial(_diff_kernel, lambda_init=lambda_init),
        grid=(batch, N_HEADS // DIFF_HPS, seq // tq),
        in_specs=[qspec, kspec, vspec] + [_const_spec((1, DIFF_DK))] * 4 + [_const_spec((HEAD_V, 1))],
        out_specs=qspec,
        out_shape=jax.ShapeDtypeStruct((batch, seq, MIX_W), BF16),
        scratch_shapes=([pltpu.VMEM((1, DIFF_GW), F32)] * nst
                        + [pltpu.VMEM((HEAD_V + 8, DIFF_GW), F32)] * nst),
        compiler_params=_params("parallel", "parallel", "arbitrary"),
        name="differential_attention",
    )(q3, k3, dvt, vec(lq1), vec(lk1), vec(lq2), vec(lk2), diff_norm.reshape(HEAD_V, 1))
    return out.reshape(batch * seq, MIX_W)


def kernel(x, l0_ffn1_norm, l0_ffn1_w_up, l0_ffn1_w_down, l0_mix_norm, l0_w_in, l0_conv_w, l0_a_log,
           l0_dt_bias, l0_gdn_norm, l0_w_out, l0_ffn2_norm, l0_ffn2_w_up, l0_ffn2_w_down,
           l1_ffn1_norm, l1_ffn1_w_up, l1_ffn1_w_down, l1_mix_norm, l1_w_in, l1_lambda_q1, l1_lambda_k1,
           l1_lambda_q2, l1_lambda_k2, l1_diff_norm, l1_w_out, l1_ffn2_norm, l1_ffn2_w_up, l1_ffn2_w_down,
           final_norm):
    batch, seq, _ = x.shape
    cos64, sin64 = _rope_tables(seq, 64)
    cos128, sin128 = _rope_tables(seq, 128)
    xf = x.reshape(batch * seq, D_MODEL)

    xf = _ffn(xf, l0_ffn1_norm, l0_ffn1_w_up, l0_ffn1_w_down)
    qkv, z, rq, rk, rv, rg, b, a = _in0(xf, l0_mix_norm, _pack_w_in0(l0_w_in), cos64, sin64, seq)
    o_a = _gdn(qkv, b, a, z, l0_conv_w, l0_a_log, l0_dt_bias, l0_gdn_norm, batch, seq)
    o_b = _retention(rq, rk, rv, rg, batch, seq)
    xf = _ffn(xf, l0_ffn2_norm, l0_ffn2_w_up, l0_ffn2_w_down, mix=(o_a, o_b, l0_w_out))

    xf = _ffn(xf, l1_ffn1_norm, l1_ffn1_w_up, l1_ffn1_w_down)
    w1, wv_t = _pack_w_in1(l1_w_in)
    cq, ck, cv, dq, dk, dvt = _in1(xf, l1_mix_norm, w1, wv_t, cos128, sin128, cos64, sin64, seq)
    o_c = _dilated(cq, ck, cv, batch, seq)
    lambda_init = 0.8 - 0.6 * math.exp(-0.3 * 1)
    o_d = _differential(dq, dk, dvt, l1_lambda_q1, l1_lambda_k1, l1_lambda_q2, l1_lambda_k2,
                        l1_diff_norm, lambda_init, batch, seq)
    xf = _ffn(xf, l1_ffn2_norm, l1_ffn2_w_up, l1_ffn2_w_down, mix=(o_c, o_d, l1_w_out), final_norm=final_norm)
    return xf.reshape(batch, seq, D_MODEL)
```

```python
import functools
import math

import numpy as np
import jax
import jax.numpy as jnp
from jax import lax
from jax.experimental import pallas as pl
from jax.experimental.pallas import tpu as pltpu

F32 = jnp.float32
BF16 = jnp.bfloat16
HIGHEST = lax.Precision.HIGHEST

EPS = 1e-6
ROPE_THETA = 10000.0
D_MODEL = 1024
D_FF = 2816
N_HEADS = 4
HEAD_V = 128
GDN_DK = 128
GDN_CONV = 4
GDN_CHUNK = 64
GDN_QK_W = N_HEADS * GDN_DK
GDN_V_W = N_HEADS * HEAD_V
GDN_CONV_W = 2 * GDN_QK_W + GDN_V_W
RET_DK = 64
RET_QK_W = N_HEADS * RET_DK
DIL_DH = 128
DIL_PATTERNS = ((128, 1), (512, 4), (2048, 16))
DIL_KEYS = 128
DIFF_DK = 64
MIX_W = N_HEADS * HEAD_V

LANES = 128
VMEM_LIMIT = 56 * 1024 * 1024

FFN_TM = 1024
FFN_TF = 256
PROJ_TM = 512
GDN_TS = 256
RET_C = 256
DIL_GROUP = 16
DIL_BLK = 2048
DIFF_TQ = 512
DIFF_TK = 512
DIFF_HPS = 4
DIFF_GW = 256
LOG2E = math.log2(math.e)


def _rms(x):
    return x * lax.rsqrt(jnp.mean(x * x, axis=-1, keepdims=True) + EPS)


def _silu(x):
    return x * (1.0 / (1.0 + jnp.exp(-x)))


def _dot(a, b):
    return jnp.dot(a, b, preferred_element_type=F32)


def _dot_nt(a, b):
    return lax.dot_general(a, b, (((1,), (1,)), ((), ())), preferred_element_type=F32)


def _dot_tn(a, b):
    return lax.dot_general(a, b, (((0,), (0,)), ((), ())), preferred_element_type=F32)


def _params(*sem):
    return pltpu.CompilerParams(dimension_semantics=sem, vmem_limit_bytes=VMEM_LIMIT)


def _const_spec(shape):
    nd = len(shape)
    return pl.BlockSpec(shape, lambda *_: (0,) * nd, pipeline_mode=pl.Buffered(1))


def _ffn_kernel(*refs, mixed, final):
    refs = list(refs)
    x_ref = refs.pop(0)
    x = x_ref[...]
    if mixed:
        oa_ref, ob_ref, wo_ref = refs.pop(0), refs.pop(0), refs.pop(0)
        x = x + _dot(oa_ref[...], wo_ref[0:MIX_W, :]) + _dot(ob_ref[...], wo_ref[MIX_W:2 * MIX_W, :])
    n_ref, wup_ref, wd_ref = refs.pop(0), refs.pop(0), refs.pop(0)
    fn_ref = refs.pop(0) if final else None
    o_ref, acc_ref = refs
    hb = (_rms(x) * n_ref[...]).astype(BF16)
    for c in range(D_FF // FFN_TF):
        lo = c * FFN_TF
        g = _dot(hb, wup_ref[:, lo:lo + FFN_TF])
        u = _dot(hb, wup_ref[:, D_FF + lo:D_FF + lo + FFN_TF])
        a = (_silu(g) * u).astype(BF16)
        d = _dot(a, wd_ref[lo:lo + FFN_TF, :])
        if c == 0:
            acc_ref[...] = d
        else:
            acc_ref[...] += d
    y = x + 0.5 * acc_ref[...]
    if final:
        y = _rms(y) * fn_ref[...]
    o_ref[...] = y


def _ffn(x2d, norm, w_up, w_down, mix=None, final_norm=None):
    T = x2d.shape[0]
    row = lambda w: pl.BlockSpec((FFN_TM, w), lambda i: (i, 0))
    args, specs = [x2d], [row(D_MODEL)]
    if mix is not None:
        o_a, o_b, w_out = mix
        args += [o_a, o_b, w_out.astype(BF16)]
        specs += [row(MIX_W), row(MIX_W), _const_spec((2 * MIX_W, D_MODEL))]
    args += [norm.reshape(1, D_MODEL), w_up.astype(BF16), w_down.astype(BF16)]
    specs += [_const_spec((1, D_MODEL)), _const_spec((D_MODEL, 2 * D_FF)), _const_spec((D_FF, D_MODEL))]
    if final_norm is not None:
        args.append(final_norm.reshape(1, D_MODEL))
        specs.append(_const_spec((1, D_MODEL)))
    return pl.pallas_call(
        functools.partial(_ffn_kernel, mixed=mix is not None, final=final_norm is not None),
        grid=(T // FFN_TM,),
        in_specs=specs,
        out_specs=row(D_MODEL),
        out_shape=jax.ShapeDtypeStruct((T, D_MODEL), F32),
        scratch_shapes=[pltpu.VMEM((FFN_TM, D_MODEL), F32)],
        compiler_params=_params("parallel"),
        name="ffn_final" if final_norm is not None else ("ffn_mix" if mix is not None else "ffn"),
    )(*args)


def _rope_tables(seq, dim):
    half = dim // 2
    inv = ROPE_THETA ** (-jnp.arange(0, dim, 2, dtype=F32) / dim)
    ang = jnp.arange(seq, dtype=F32)[:, None] * inv[None, :]
    reps = (LANES // 2) // half
    cos = jnp.tile(jnp.cos(ang), (1, 2 * reps))
    sin = jnp.tile(jnp.sin(ang), (1, reps))
    return cos, jnp.concatenate([-sin, sin], axis=1)


def _rope(x, cos, sin):
    outs = []
    for g in range(x.shape[1] // LANES):
        xg = x[:, g * LANES:(g + 1) * LANES]
        outs.append(xg * cos + pltpu.roll(xg, LANES // 2, 1) * sin)
    return outs[0] if len(outs) == 1 else jnp.concatenate(outs, axis=1)


_P0_QKV = (0, GDN_CONV_W)
_P0_Z = (_P0_QKV[1], _P0_QKV[1] + GDN_V_W)
_P0_RQ = (_P0_Z[1], _P0_Z[1] + RET_QK_W)
_P0_RK = (_P0_RQ[1], _P0_RQ[1] + RET_QK_W)
_P0_RV = (_P0_RK[1], _P0_RK[1] + MIX_W)
_P0_RG = (_P0_RV[1], _P0_RV[1] + MIX_W)
_P0_BA = (_P0_RG[1], _P0_RG[1] + LANES)
_P0_W = _P0_BA[1]


def _ret_perm():
    perm = np.zeros(RET_QK_W, np.int32)
    for h in range(N_HEADS):
        for d in range(RET_DK):
            new = (h // 2) * LANES + (d // 32) * 64 + (h % 2) * 32 + d % 32
            perm[new] = h * RET_DK + d
    return perm


def _diff_perm():
    perm = np.zeros(N_HEADS * 2 * DIFF_DK, np.int32)
    for h in range(N_HEADS):
        for c in range(2):
            for d in range(DIFF_DK):
                new = h * LANES + (d // 32) * 64 + c * 32 + d % 32
                perm[new] = h * 2 * DIFF_DK + c * DIFF_DK + d
    return perm


def _pack_w_in0(w_in):
    o = np.cumsum((0, GDN_CONV_W, GDN_V_W, N_HEADS, N_HEADS, RET_QK_W, RET_QK_W, MIX_W, MIX_W))
    qkv, z, b, a, rq, rk, rv, rg = (w_in[:, o[i]:o[i + 1]] for i in range(8))
    perm = _ret_perm()
    pad = jnp.zeros((D_MODEL, LANES - 2 * N_HEADS), w_in.dtype)
    cols = [qkv, z, rq[:, perm], rk[:, perm], rv, rg, b, a, pad]
    return jnp.concatenate(cols, axis=1).astype(BF16)


def _in0_kernel(x_ref, n_ref, w_ref, cos_ref, sin_ref,
                qkv_ref, z_ref, rq_ref, rk_ref, rv_ref, rg_ref, ba_ref):
    hb = (_rms(x_ref[...]) * n_ref[...]).astype(BF16)

    def proj(seg):
        return _dot(hb, w_ref[:, seg[0]:seg[1]])

    cos = cos_ref[...]
    sin = sin_ref[...]
    qkv_ref[...] = proj(_P0_QKV)
    z_ref[...] = proj(_P0_Z).astype(BF16)
    rq_ref[...] = (_rope(proj(_P0_RQ), cos, sin) * RET_DK ** -0.5).astype(BF16)
    rk_ref[...] = _rope(proj(_P0_RK), cos, sin).astype(BF16)
    rv_ref[...] = proj(_P0_RV).astype(BF16)
    rg_ref[...] = proj(_P0_RG).astype(BF16)
    ba_ref[...] = proj(_P0_BA)


def _in0(x2d, norm, w_packed, cos, sin, seq):
    T = x2d.shape[0]
    tm = PROJ_TM
    spt = seq // tm
    row = lambda w: pl.BlockSpec((tm, w), lambda i: (i, 0))
    tab = pl.BlockSpec((tm, LANES), lambda i: (i % spt, 0))
    widths = (GDN_CONV_W, GDN_V_W, RET_QK_W, RET_QK_W, MIX_W, MIX_W, LANES)
    dtypes = (F32, BF16, BF16, BF16, BF16, BF16, F32)
    return pl.pallas_call(
        _in0_kernel,
        grid=(T // tm,),
        in_specs=[row(D_MODEL), _const_spec((1, D_MODEL)), _const_spec((D_MODEL, _P0_W)), tab, tab],
        out_specs=[row(w) for w in widths],
        out_shape=[jax.ShapeDtypeStruct((T, w), dt) for w, dt in zip(widths, dtypes)],
        compiler_params=_params("parallel"),
        name="in_proj0",
    )(x2d, norm.reshape(1, D_MODEL), w_packed, cos, sin)


def _bmm(a, b):
    return jnp.einsum("gik,gkj->gij", a.astype(BF16), b.astype(BF16), preferred_element_type=F32)


def _unit_lower_inverse(a, ri, ci):
    n = jnp.where((ri >> 3) == (ci >> 3), -a, 0.0)
    t = jnp.where(ri == ci, 1.0, 0.0) + n
    p = _bmm(n, n)
    t = t + _bmm(t, p)
    p = _bmm(p, p)
    t = t + _bmm(t, p)
    for sh in (3, 4, 5):
        e = jnp.where(((ri >> (sh + 1)) == (ci >> (sh + 1))) & ((ri >> sh) != (ci >> sh)), a, 0.0)
        t = t - _bmm(_bmm(t, e), t)
    return t


def _gdn_kernel(qkv_ref, ba_ref, z_ref, cw_ref, alog_ref, dtb_ref, gn_ref, o_ref,
                xe_ref, o_s, state_ref):
    i = pl.program_id(1)
    ts = qkv_ref.shape[0]
    C = GDN_CHUNK
    nch = ts // C
    halo = 8

    @pl.when(i == 0)
    def _():
        xe_ref[0:halo, :] = jnp.zeros((halo, GDN_CONV_W), F32)
        state_ref[...] = jnp.zeros_like(state_ref)

    xe_ref[halo:halo + ts, :] = qkv_ref[...]
    conv = cw_ref[0:1, :] * xe_ref[halo - 3:halo - 3 + ts, :]
    for j in range(1, GDN_CONV):
        conv = conv + cw_ref[j:j + 1, :] * xe_ref[halo - 3 + j:halo - 3 + j + ts, :]
    xe_ref[0:halo, :] = xe_ref[ts:ts + halo, :]
    act = _silu(conv)
    qn, kn, vv = [], [], []
    for h in range(N_HEADS):
        lo = h * GDN_DK
        qh = act[:, lo:lo + GDN_DK]
        kh = act[:, GDN_QK_W + lo:GDN_QK_W + lo + GDN_DK]
        qn.append(qh * (lax.rsqrt(jnp.sum(qh * qh, -1, keepdims=True) + EPS) * GDN_DK ** -0.5))
        kn.append(kh * lax.rsqrt(jnp.sum(kh * kh, -1, keepdims=True) + EPS))
        vv.append(act[:, 2 * GDN_QK_W + h * HEAD_V:2 * GDN_QK_W + (h + 1) * HEAD_V])

    beta = 1.0 / (1.0 + jnp.exp(-ba_ref[...]))
    xa = ba_ref[...] + dtb_ref[...]
    softplus = jnp.maximum(xa, 0.0) + jnp.log(1.0 + jnp.exp(-jnp.abs(xa)))
    g = -jnp.exp(alog_ref[...]) * softplus
    rt = lax.broadcasted_iota(jnp.int32, (ts, ts), 0)
    ct = lax.broadcasted_iota(jnp.int32, (ts, ts), 1)
    tri = jnp.where((rt >= ct) & ((rt >> 6) == (ct >> 6)), 1.0, 0.0)
    gc = jnp.dot(tri, g, precision=HIGHEST, preferred_element_type=F32)

    units = [(c, h) for c in range(nch) for h in range(N_HEADS)]
    rows = lambda x, c: x[c * C:(c + 1) * C]
    stack = lambda f: jnp.stack([f(c, h) for c, h in units], axis=0)
    q = stack(lambda c, h: rows(qn[h], c))
    k = stack(lambda c, h: rows(kn[h], c))
    v = stack(lambda c, h: rows(vv[h], c))
    gct = [rows(gc, c).T for c in range(nch)]
    gcol = stack(lambda c, h: rows(gc, c)[:, N_HEADS + h:N_HEADS + h + 1])
    grow = stack(lambda c, h: gct[c][N_HEADS + h:N_HEADS + h + 1, :])
    bcol = stack(lambda c, h: rows(beta, c)[:, h:h + 1])
    glast = gcol[:, C - 1:C, :]

    ri = lax.broadcasted_iota(jnp.int32, (1, C, C), 1)
    ci = lax.broadcasted_iota(jnp.int32, (1, C, C), 2)
    causal = ri >= ci
    decay = jnp.where(causal, jnp.exp(jnp.where(causal, gcol - grow, 0.0)), 0.0)
    kb = k.astype(BF16)
    kk = jnp.einsum("gid,gjd->gij", kb, kb, preferred_element_type=F32)
    a_mat = jnp.where(ri > ci, kk * decay * bcol, 0.0)
    t_inv = _unit_lower_inverse(a_mat, ri, ci)
    eg = jnp.exp(gcol)
    rhs = jnp.concatenate([v * bcol, k * (bcol * eg)], axis=2).astype(BF16)
    sol = jnp.einsum("gik,gkj->gij", t_inv.astype(BF16), rhs, preferred_element_type=F32)
    solb = sol.astype(BF16)
    qk = (jnp.einsum("gid,gjd->gij", q.astype(BF16), kb, preferred_element_type=F32) * decay).astype(BF16)
    qk_uw = jnp.einsum("gij,gje->gie", qk, solb, preferred_element_type=F32)
    kd = k * jnp.exp(glast - gcol)
    kdt = jnp.stack([kd[n].T for n in range(len(units))], axis=0).astype(BF16)
    kd_uw = jnp.einsum("gdc,gce->gde", kdt, solb, preferred_element_type=F32)
    lhs_all = jnp.concatenate([kd_uw[:, :, HEAD_V:], q * eg - qk_uw[:, :, HEAD_V:]], axis=1).astype(BF16)
    c_all = kd_uw[:, :, :HEAD_V]
    o0_all = qk_uw[:, :, :HEAD_V]
    egl = jnp.exp(glast)

    states = [state_ref[h] for h in range(N_HEADS)]
    for n, (c, h) in enumerate(units):
        r = _dot(lhs_all[n], states[h].astype(BF16))
        o_s[c * C:(c + 1) * C, h * HEAD_V:(h + 1) * HEAD_V] = o0_all[n] + r[GDN_DK:]
        states[h] = states[h] * egl[n] + c_all[n] - r[:GDN_DK]
    for h in range(N_HEADS):
        state_ref[h] = states[h]

    for h in range(N_HEADS):
        lo = h * HEAD_V
        zh = z_ref[:, lo:lo + HEAD_V].astype(F32)
        o_ref[:, lo:lo + HEAD_V] = (_rms(o_s[:, lo:lo + HEAD_V]) * gn_ref[...] * _silu(zh)).astype(BF16)


def _gdn(qkv, ba, z, conv_w, a_log, dt_bias, gdn_norm, batch, seq):
    T = qkv.shape[0]
    ts = GDN_TS
    spt = seq // ts
    row = lambda w: pl.BlockSpec((ts, w), lambda bi, i: (bi * spt + i, 0))
    pad4 = lambda p: jnp.pad(p.astype(F32), (N_HEADS, LANES - 2 * N_HEADS)).reshape(1, LANES)
    return pl.pallas_call(
        _gdn_kernel,
        grid=(batch, spt),
        in_specs=[row(GDN_CONV_W), row(LANES), row(GDN_V_W),
                  _const_spec((GDN_CONV, GDN_CONV_W)), _const_spec((1, LANES)), _const_spec((1, LANES)),
                  _const_spec((1, HEAD_V))],
        out_specs=row(GDN_V_W),
        out_shape=jax.ShapeDtypeStruct((T, GDN_V_W), BF16),
        scratch_shapes=[
            pltpu.VMEM((ts + 8, GDN_CONV_W), F32),
            pltpu.VMEM((ts, GDN_V_W), F32),
            pltpu.VMEM((N_HEADS, GDN_DK, HEAD_V), F32),
        ],
        compiler_params=_params("arbitrary", "arbitrary"),
        name="gated_deltanet",
    )(qkv, ba, z, conv_w.astype(F32), pad4(a_log), pad4(dt_bias), gdn_norm.reshape(1, HEAD_V))


def _ret_kernel(q_ref, k_ref, v_ref, g_ref, o_ref, state_ref, dm_ref):
    i = pl.program_id(1)
    c = q_ref.shape[0]

    @pl.when(i == 0)
    def _():
        state_ref[...] = jnp.zeros_like(state_ref)

    @pl.when((pl.program_id(0) == 0) & (i == 0))
    def _():
        ri = lax.broadcasted_iota(jnp.int32, (c, c), 0)
        ci = lax.broadcasted_iota(jnp.int32, (c, c), 1)
        rel = (ri - ci).astype(F32)
        for h in range(N_HEADS):
            lg = math.log(1.0 - 2.0 ** (-5.0 - h))
            dm_ref[h] = jnp.where(rel >= 0, jnp.exp(jnp.where(rel >= 0, rel, 0.0) * lg), 0.0)

    pos = lax.broadcasted_iota(jnp.int32, (c, 1), 0).astype(F32)
    lane = lax.broadcasted_iota(jnp.int32, (1, LANES), 1)
    for h in range(N_HEADS):
        lg = math.log(1.0 - 2.0 ** (-5.0 - h))
        grp = (h // 2) * LANES
        own = ((lane >> 5) & 1) == h % 2
        qm = jnp.where(own, q_ref[:, grp:grp + LANES], 0)
        km = jnp.where(own, k_ref[:, grp:grp + LANES], 0)
        v = v_ref[:, h * HEAD_V:(h + 1) * HEAD_V]
        intra = _dot((_dot_nt(qm, km) * dm_ref[h]).astype(BF16), v)
        state = state_ref[h]
        inter = _dot(qm, state.astype(BF16)) * jnp.exp((pos + 1.0) * lg)
        k_sc = (km.astype(F32) * jnp.exp((c - 1.0 - pos) * lg)).astype(BF16)
        state_ref[h] = state * math.exp(c * lg) + _dot_tn(k_sc, v)
        gate = _silu(g_ref[:, h * HEAD_V:(h + 1) * HEAD_V].astype(F32))
        o_ref[:, h * HEAD_V:(h + 1) * HEAD_V] = (_rms(intra + inter) * gate).astype(BF16)


def _retention(rq, rk, rv, rg, batch, seq):
    T = rq.shape[0]
    c = RET_C
    spt = seq // c
    row = lambda w: pl.BlockSpec((c, w), lambda bi, i: (bi * spt + i, 0))
    return pl.pallas_call(
        _ret_kernel,
        grid=(batch, spt),
        in_specs=[row(RET_QK_W), row(RET_QK_W), row(MIX_W), row(MIX_W)],
        out_specs=row(MIX_W),
        out_shape=jax.ShapeDtypeStruct((T, MIX_W), BF16),
        scratch_shapes=[pltpu.VMEM((N_HEADS, LANES, HEAD_V), F32), pltpu.VMEM((N_HEADS, c, c), F32)],
        compiler_params=_params("arbitrary", "arbitrary"),
        name="retention",
    )(rq, rk, rv, rg)


def _pack_w_in1(w_in):
    o = np.cumsum((0,) + (MIX_W,) * 6)
    cq, ck, cv, dq, dk, dv = (w_in[:, o[i]:o[i + 1]] for i in range(6))
    perm = _diff_perm()
    w = jnp.concatenate([cq, ck, cv, dq[:, perm], dk[:, perm]], axis=1).astype(BF16)
    return w, dv.T.astype(BF16)


def _in1_kernel(x_ref, n_ref, w_ref, wvt_ref, cosa_ref, sina_ref, cosb_ref, sinb_ref,
                cq_ref, ck_ref, cv_ref, dq_ref, dk_ref, dvt_ref):
    hb = (_rms(x_ref[...]) * n_ref[...]).astype(BF16)

    def proj(j):
        return _dot(hb, w_ref[:, j * MIX_W:(j + 1) * MIX_W])

    cq_ref[...] = _rope(proj(0), cosa_ref[...], sina_ref[...]) * (DIL_DH ** -0.5 * LOG2E)
    ck_ref[...] = _rope(proj(1), cosa_ref[...], sina_ref[...])
    cv_ref[...] = proj(2)
    dq_ref[...] = (_rope(proj(3), cosb_ref[...], sinb_ref[...]) * (DIFF_DK ** -0.5 * LOG2E)).astype(BF16)
    dk_ref[...] = _rope(proj(4), cosb_ref[...], sinb_ref[...]).astype(BF16)
    vt = _dot_nt(wvt_ref[...], hb).astype(BF16)
    for j in range(dvt_ref.shape[0]):
        dvt_ref[j] = vt[:, j * DIFF_TK:(j + 1) * DIFF_TK]


def _in1(x2d, norm, w_packed, wv_t, cosa, sina, cosb, sinb, seq):
    T = x2d.shape[0]
    tm = PROJ_TM
    spt = seq // tm
    vblk = tm // DIFF_TK
    row = lambda w: pl.BlockSpec((tm, w), lambda i: (i, 0))
    tab = pl.BlockSpec((tm, LANES), lambda i: (i % spt, 0))
    out_shape = [jax.ShapeDtypeStruct((T, MIX_W), dt) for dt in (F32, F32, F32, BF16, BF16)]
    out_shape.append(jax.ShapeDtypeStruct((T // DIFF_TK, MIX_W, DIFF_TK), BF16))
    return pl.pallas_call(
        _in1_kernel,
        grid=(T // tm,),
        in_specs=[row(D_MODEL), _const_spec((1, D_MODEL)), _const_spec((D_MODEL, 5 * MIX_W)),
                  _const_spec((MIX_W, D_MODEL)), tab, tab, tab, tab],
        out_specs=[row(MIX_W)] * 5 + [pl.BlockSpec((vblk, MIX_W, DIFF_TK), lambda i: (i, 0, 0))],
        out_shape=out_shape,
        compiler_params=_params("parallel"),
        name="in_proj1",
    )(x2d, norm.reshape(1, D_MODEL), w_packed, wv_t, cosa, sina, cosb, sinb)


def _dil_kernel(q_ref, kp_ref, kc_ref, vp_ref, vc_ref, o_ref, kcat, vcat, ob, mb, lb):
    i = pl.program_id(2)
    blk = q_ref.shape[0]
    nk = DIL_KEYS
    kcat[0:blk, :] = kp_ref[...]
    kcat[blk:2 * blk, :] = kc_ref[...]
    vcat[0:blk, :] = vp_ref[...]
    vcat[blk:2 * blk, :] = vc_ref[...]

    qi = lax.broadcasted_iota(jnp.int32, (nk, 2 * nk), 0)
    kj = lax.broadcasted_iota(jnp.int32, (nk, 2 * nk), 1)
    dist = qi + nk - kj
    band = (dist >= 0) & (dist <= nk)
    ones_v = jnp.ones((2 * nk, LANES), BF16)

    for gi, (_, d) in enumerate(DIL_PATTERNS):
        span = nk * d

        def units(nb, carry, gi=gi, d=d, span=span):
            pos = []
            for u in range(DIL_GROUP):
                n = nb * DIL_GROUP + u
                sub = n // d
                qs = sub * span + n % d
                pos.append((qs, qs + blk - span, sub))
            qs_ = [q_ref[pl.ds(qs, nk, stride=d), :].astype(BF16) for qs, _, _ in pos]
            ks_ = [kcat[pl.ds(ks, 2 * nk, stride=d), :].astype(BF16) for _, ks, _ in pos]
            vs_ = [jnp.concatenate([vcat[pl.ds(ks, 2 * nk, stride=d), :].astype(BF16), ones_v], axis=1)
                   for _, ks, _ in pos]
            ss = [_dot_nt(q, k) for q, k in zip(qs_, ks_)]
            ps, ms = [], []
            for s, (_, _, sub) in zip(ss, pos):
                first_key = jnp.where((i > 0) | (sub > 0), 0, nk)
                sb = jnp.where(band & (kj >= first_key), s.astype(BF16), -jnp.inf)
                m = jnp.max(sb, axis=-1, keepdims=True)
                ps.append(jnp.exp2(sb - m))
                ms.append(m.astype(F32))
            os_ = [_dot(p, v) for p, v in zip(ps, vs_)]
            for (qs, _, _), o, m in zip(pos, os_, ms):
                ob[gi, pl.ds(qs, nk, stride=d), :] = o[:, :HEAD_V]
                lb[gi, pl.ds(qs, nk, stride=d), :] = o[:, HEAD_V:]
                mb[gi, pl.ds(qs, nk, stride=d), :] = jnp.broadcast_to(m, (nk, LANES))
            return carry

        lax.fori_loop(0, blk // nk // DIL_GROUP, units, 0)

    m_all = jnp.maximum(jnp.maximum(mb[0], mb[1]), mb[2])
    num = jnp.zeros((blk, HEAD_V), F32)
    den = jnp.zeros((blk, LANES), F32)
    for gi in range(len(DIL_PATTERNS)):
        sc = jnp.exp2(mb[gi] - m_all)
        num = num + ob[gi] * sc
        den = den + lb[gi] * sc
    o_ref[...] = (num / den).astype(BF16)


def _dilated(cq, ck, cv, batch, seq):
    blk = DIL_BLK
    q3, k3, v3 = (t.reshape(batch, seq, MIX_W) for t in (cq, ck, cv))
    cur = pl.BlockSpec((None, blk, DIL_DH), lambda b, h, i: (b, i, h))
    prev = pl.BlockSpec((None, blk, DIL_DH), lambda b, h, i: (b, jnp.maximum(i - 1, 0), h))
    nb = len(DIL_PATTERNS)
    out = pl.pallas_call(
        _dil_kernel,
        grid=(batch, N_HEADS, seq // blk),
        in_specs=[cur, prev, cur, prev, cur],
        out_specs=cur,
        out_shape=jax.ShapeDtypeStruct((batch, seq, MIX_W), BF16),
        scratch_shapes=[
            pltpu.VMEM((2 * blk, DIL_DH), F32),
            pltpu.VMEM((2 * blk, DIL_DH), F32),
            pltpu.VMEM((nb, blk, HEAD_V), F32),
            pltpu.VMEM((nb, blk, LANES), F32),
            pltpu.VMEM((nb, blk, LANES), F32),
        ],
        compiler_params=_params("parallel", "parallel", "arbitrary"),
        name="dilated_attention",
    )(q3, k3, k3, v3, v3)
    return out.reshape(batch * seq, MIX_W)


def _diff_kernel(q_ref, k_ref, vt_ref, lq1_ref, lk1_ref, lq2_ref, lk2_ref, dn_ref, o_ref,
                 *stats, lambda_init):
    i = pl.program_id(2)
    tq = q_ref.shape[0]
    tk = DIFF_TK
    gw = DIFF_GW
    ngrp = 2 * tq // gw
    nst = DIFF_HPS * ngrp
    m_s, acc_s = stats[0:nst], stats[nst:2 * nst]
    lane = lax.broadcasted_iota(jnp.int32, (1, LANES), 1)
    first = ((lane >> 5) & 1) == 0
    qgs = []
    for hh in range(DIFF_HPS):
        q = q_ref[:, hh * LANES:(hh + 1) * LANES]
        qq = jnp.concatenate([jnp.where(first, q, 0), jnp.where(first, 0, q)], axis=0)
        qgs.append([qq[g * gw:(g + 1) * gw] for g in range(ngrp)])
    for r in m_s:
        r[...] = jnp.full_like(r, -jnp.inf)
    for r in acc_s:
        r[...] = jnp.zeros_like(r)

    def scores(kt, hh):
        k0 = pl.multiple_of(kt * tk, tk)
        k_rows = k_ref[pl.ds(k0, tk), hh * LANES:(hh + 1) * LANES]
        return [_dot_nt(k_rows, qg) for qg in qgs[hh]]

    def update(ss, kt, hh, diag):
        vt = jnp.concatenate([vt_ref[kt, hh * HEAD_V:(hh + 1) * HEAD_V, :], jnp.ones((8, tk), BF16)], axis=0)
        ps, alphas = [], []
        for g, s in enumerate(ss):
            n = hh * ngrp + g
            sb = s.astype(BF16)
            if diag is not None:
                key = lax.broadcasted_iota(jnp.int32, (tk, gw), 0) + diag * tk
                qry = (lax.broadcasted_iota(jnp.int32, (tk, gw), 1) + g * gw) & (tq - 1)
                sb = jnp.where(qry >= key, sb, -jnp.inf)
            m_prev = m_s[n][...]
            m_new = jnp.maximum(m_prev, jnp.max(sb, axis=0, keepdims=True).astype(F32))
            alphas.append(jnp.exp2(m_prev - m_new))
            ps.append(jnp.exp2(sb - m_new.astype(BF16)))
            m_s[n][...] = m_new
        pvs = [_dot(vt, p) for p in ps]
        for g in range(ngrp):
            n = hh * ngrp + g
            acc_s[n][...] = alphas[g] * acc_s[n][...] + pvs[g]

    per_q = tq // tk

    def step(j, diagonal):
        work = [(j * per_q + t, hh, t if diagonal else None) for t in range(per_q) for hh in range(DIFF_HPS)]
        sss = [scores(kt, hh) for kt, hh, _ in work]
        for ss, (kt, hh, diag) in zip(sss, work):
            update(ss, kt, hh, diag)

    def body(j, carry):
        step(j, False)
        return carry

    lax.fori_loop(0, i, body, 0)
    step(i, True)

    lam = (jnp.exp(jnp.sum(lq1_ref[...] * lk1_ref[...], axis=-1, keepdims=True))
           - jnp.exp(jnp.sum(lq2_ref[...] * lk2_ref[...], axis=-1, keepdims=True)) + lambda_init)
    half = ngrp // 2
    for hh in range(DIFF_HPS):
        for g in range(half):
            n1, n2 = hh * ngrp + g, hh * ngrp + g + half
            a1, a2 = acc_s[n1][...], acc_s[n2][...]
            o = a1[:HEAD_V] / a1[HEAD_V:HEAD_V + 1] - lam * (a2[:HEAD_V] / a2[HEAD_V:HEAD_V + 1])
            o = o * lax.rsqrt(jnp.mean(o * o, axis=0, keepdims=True) + EPS) * dn_ref[...] * (1.0 - lambda_init)
            o_ref[g * gw:(g + 1) * gw, hh * HEAD_V:(hh + 1) * HEAD_V] = o.T.astype(BF16)


def _differential(dq, dk, dvt, lq1, lk1, lq2, lk2, diff_norm, lambda_init, batch, seq):
    tq = DIFF_TQ
    nkt = seq // DIFF_TK
    nst = DIFF_HPS * 2 * tq // DIFF_GW
    hw = DIFF_HPS * LANES
    q3, k3 = (t.reshape(batch, seq, MIX_W) for t in (dq, dk))
    qspec = pl.BlockSpec((None, tq, hw), lambda b, h, i: (b, i, h))
    kspec = pl.BlockSpec((None, seq, hw), lambda b, h, i: (b, 0, h))
    vspec = pl.BlockSpec((nkt, hw, DIFF_TK), lambda b, h, i: (b, h, 0))
    vec = lambda p: p.reshape(1, DIFF_DK).astype(F32)
    out = pl.pallas_call(
        functools.partial(_diff_kernel, lambda_init=lambda_init),
        grid=(batch, N_HEADS // DIFF_HPS, seq // tq),
        in_specs=[qspec, kspec, vspec] + [_const_spec((1, DIFF_DK))] * 4 + [_const_spec((HEAD_V, 1))],
        out_specs=qspec,
        out_shape=jax.ShapeDtypeStruct((batch, seq, MIX_W), BF16),
        scratch_shapes=([pltpu.VMEM((1, DIFF_GW), F32)] * nst
                        + [pltpu.VMEM((HEAD_V + 8, DIFF_GW), F32)] * nst),
        compiler_params=_params("parallel", "parallel", "arbitrary"),
        name="differential_attention",
    )(q3, k3, dvt, vec(lq1), vec(lk1), vec(lq2), vec(lk2), diff_norm.reshape(HEAD_V, 1))
    return out.reshape(batch * seq, MIX_W)


def kernel(x, l0_ffn1_norm, l0_ffn1_w_up, l0_ffn1_w_down, l0_mix_norm, l0_w_in, l0_conv_w, l0_a_log,
           l0_dt_bias, l0_gdn_norm, l0_w_out, l0_ffn2_norm, l0_ffn2_w_up, l0_ffn2_w_down,
           l1_ffn1_norm, l1_ffn1_w_up, l1_ffn1_w_down, l1_mix_norm, l1_w_in, l1_lambda_q1, l1_lambda_k1,
           l1_lambda_q2, l1_lambda_k2, l1_diff_norm, l1_w_out, l1_ffn2_norm, l1_ffn2_w_up, l1_ffn2_w_down,
           final_norm):
    batch, seq, _ = x.shape
    cos64, sin64 = _rope_tables(seq, 64)
    cos128, sin128 = _rope_tables(seq, 128)
    xf = x.reshape(batch * seq, D_MODEL)

    xf = _ffn(xf, l0_ffn1_norm, l0_ffn1_w_up, l0_ffn1_w_down)
    qkv, z, rq, rk, rv, rg, ba = _in0(xf, l0_mix_norm, _pack_w_in0(l0_w_in), cos64, sin64, seq)
    o_a = _gdn(qkv, ba, z, l0_conv_w, l0_a_log, l0_dt_bias, l0_gdn_norm, batch, seq)
    o_b = _retention(rq, rk, rv, rg, batch, seq)
    xf = _ffn(xf, l0_ffn2_norm, l0_ffn2_w_up, l0_ffn2_w_down, mix=(o_a, o_b, l0_w_out))

    xf = _ffn(xf, l1_ffn1_norm, l1_ffn1_w_up, l1_ffn1_w_down)
    w1, wv_t = _pack_w_in1(l1_w_in)
    cq, ck, cv, dq, dk, dvt = _in1(xf, l1_mix_norm, w1, wv_t, cos128, sin128, cos64, sin64, seq)
    o_c = _dilated(cq, ck, cv, batch, seq)
    lambda_init = 0.8 - 0.6 * math.exp(-0.3 * 1)
    o_d = _differential(dq, dk, dvt, l1_lambda_q1, l1_lambda_k1, l1_lambda_q2, l1_lambda_k2,
                        l1_diff_norm, lambda_init, batch, seq)
    xf = _ffn(xf, l1_ffn2_norm, l1_ffn2_w_up, l1_ffn2_w_down, mix=(o_c, o_d, l1_w_out), final_norm=final_norm)
    return xf.reshape(batch, seq, D_MODEL)
```

```python
import functools
import math

import numpy as np
import jax
import jax.numpy as jnp
from jax import lax
from jax.experimental import pallas as pl
from jax.experimental.pallas import tpu as pltpu

F32 = jnp.float32
BF16 = jnp.bfloat16
HIGHEST = lax.Precision.HIGHEST

EPS = 1e-6
ROPE_THETA = 10000.0
D_MODEL = 1024
D_FF = 2816
N_HEADS = 4
HEAD_V = 128
GDN_DK = 128
GDN_CONV = 4
GDN_CHUNK = 64
GDN_QK_W = N_HEADS * GDN_DK
GDN_V_W = N_HEADS * HEAD_V
GDN_CONV_W = 2 * GDN_QK_W + GDN_V_W
RET_DK = 64
RET_QK_W = N_HEADS * RET_DK
DIL_DH = 128
DIL_PATTERNS = ((128, 1), (512, 4), (2048, 16))
DIL_KEYS = 128
DIFF_DK = 64
MIX_W = N_HEADS * HEAD_V

LANES = 128
VMEM_LIMIT = 56 * 1024 * 1024

FFN_TM = 1024
FFN_TF = 256
PROJ_TM = 512
GDN_TS = 256
RET_C = 256
DIL_GROUP = 16
DIL_BLK = 2048
DIFF_TQ = 512
DIFF_TK = 512
DIFF_HPS = 4
DIFF_GW = 256
LOG2E = math.log2(math.e)


def _rms(x):
    return x * lax.rsqrt(jnp.mean(x * x, axis=-1, keepdims=True) + EPS)


def _silu(x):
    return x * (1.0 / (1.0 + jnp.exp(-x)))


def _dot(a, b):
    return jnp.dot(a, b, preferred_element_type=F32)


def _dot_nt(a, b):
    return lax.dot_general(a, b, (((1,), (1,)), ((), ())), preferred_element_type=F32)


def _dot_tn(a, b):
    return lax.dot_general(a, b, (((0,), (0,)), ((), ())), preferred_element_type=F32)


def _params(*sem):
    return pltpu.CompilerParams(dimension_semantics=sem, vmem_limit_bytes=VMEM_LIMIT)


def _const_spec(shape):
    nd = len(shape)
    return pl.BlockSpec(shape, lambda *_: (0,) * nd, pipeline_mode=pl.Buffered(1))


def _ffn_kernel(*refs, mixed, final):
    refs = list(refs)
    x_ref = refs.pop(0)
    x = x_ref[...]
    if mixed:
        oa_ref, ob_ref, wo_ref = refs.pop(0), refs.pop(0), refs.pop(0)
        x = x + _dot(oa_ref[...], wo_ref[0:MIX_W, :]) + _dot(ob_ref[...], wo_ref[MIX_W:2 * MIX_W, :])
    n_ref, wup_ref, wd_ref = refs.pop(0), refs.pop(0), refs.pop(0)
    fn_ref = refs.pop(0) if final else None
    o_ref, acc_ref = refs
    hb = (_rms(x) * n_ref[...]).astype(BF16)
    for c in range(D_FF // FFN_TF):
        lo = c * FFN_TF
        g = _dot(hb, wup_ref[:, lo:lo + FFN_TF])
        u = _dot(hb, wup_ref[:, D_FF + lo:D_FF + lo + FFN_TF])
        a = (_silu(g) * u).astype(BF16)
        d = _dot(a, wd_ref[lo:lo + FFN_TF, :])
        if c == 0:
            acc_ref[...] = d
        else:
            acc_ref[...] += d
    y = x + 0.5 * acc_ref[...]
    if final:
        y = _rms(y) * fn_ref[...]
    o_ref[...] = y


def _ffn(x2d, norm, w_up, w_down, mix=None, final_norm=None):
    T = x2d.shape[0]
    row = lambda w: pl.BlockSpec((FFN_TM, w), lambda i: (i, 0))
    args, specs = [x2d], [row(D_MODEL)]
    if mix is not None:
        o_a, o_b, w_out = mix
        args += [o_a, o_b, w_out.astype(BF16)]
        specs += [row(MIX_W), row(MIX_W), _const_spec((2 * MIX_W, D_MODEL))]
    args += [norm.reshape(1, D_MODEL), w_up.astype(BF16), w_down.astype(BF16)]
    specs += [_const_spec((1, D_MODEL)), _const_spec((D_MODEL, 2 * D_FF)), _const_spec((D_FF, D_MODEL))]
    if final_norm is not None:
        args.append(final_norm.reshape(1, D_MODEL))
        specs.append(_const_spec((1, D_MODEL)))
    return pl.pallas_call(
        functools.partial(_ffn_kernel, mixed=mix is not None, final=final_norm is not None),
        grid=(T // FFN_TM,),
        in_specs=specs,
        out_specs=row(D_MODEL),
        out_shape=jax.ShapeDtypeStruct((T, D_MODEL), F32),
        scratch_shapes=[pltpu.VMEM((FFN_TM, D_MODEL), F32)],
        compiler_params=_params("parallel"),
        name="ffn_final" if final_norm is not None else ("ffn_mix" if mix is not None else "ffn"),
    )(*args)


def _rope_tables(seq, dim):
    half = dim // 2
    inv = ROPE_THETA ** (-jnp.arange(0, dim, 2, dtype=F32) / dim)
    ang = jnp.arange(seq, dtype=F32)[:, None] * inv[None, :]
    reps = (LANES // 2) // half
    cos = jnp.tile(jnp.cos(ang), (1, 2 * reps))
    sin = jnp.tile(jnp.sin(ang), (1, reps))
    return cos, jnp.concatenate([-sin, sin], axis=1)


def _rope(x, cos, sin):
    outs = []
    for g in range(x.shape[1] // LANES):
        xg = x[:, g * LANES:(g + 1) * LANES]
        outs.append(xg * cos + pltpu.roll(xg, LANES // 2, 1) * sin)
    return outs[0] if len(outs) == 1 else jnp.concatenate(outs, axis=1)


_P0_QKV = (0, GDN_CONV_W)
_P0_Z = (_P0_QKV[1], _P0_QKV[1] + GDN_V_W)
_P0_RQ = (_P0_Z[1], _P0_Z[1] + RET_QK_W)
_P0_RK = (_P0_RQ[1], _P0_RQ[1] + RET_QK_W)
_P0_RV = (_P0_RK[1], _P0_RK[1] + MIX_W)
_P0_RG = (_P0_RV[1], _P0_RV[1] + MIX_W)
_P0_BA = (_P0_RG[1], _P0_RG[1] + LANES)
_P0_W = _P0_BA[1]


def _ret_perm():
    perm = np.zeros(RET_QK_W, np.int32)
    for h in range(N_HEADS):
        for d in range(RET_DK):
            new = (h // 2) * LANES + (d // 32) * 64 + (h % 2) * 32 + d % 32
            perm[new] = h * RET_DK + d
    return perm


def _diff_perm():
    perm = np.zeros(N_HEADS * 2 * DIFF_DK, np.int32)
    for h in range(N_HEADS):
        for c in range(2):
            for d in range(DIFF_DK):
                new = h * LANES + (d // 32) * 64 + c * 32 + d % 32
                perm[new] = h * 2 * DIFF_DK + c * DIFF_DK + d
    return perm


def _pack_w_in0(w_in):
    o = np.cumsum((0, GDN_CONV_W, GDN_V_W, N_HEADS, N_HEADS, RET_QK_W, RET_QK_W, MIX_W, MIX_W))
    qkv, z, b, a, rq, rk, rv, rg = (w_in[:, o[i]:o[i + 1]] for i in range(8))
    perm = _ret_perm()
    pad = jnp.zeros((D_MODEL, LANES - 2 * N_HEADS), w_in.dtype)
    cols = [qkv, z, rq[:, perm], rk[:, perm], rv, rg, b, a, pad]
    return jnp.concatenate(cols, axis=1).astype(BF16)


def _in0_kernel(x_ref, n_ref, w_ref, cos_ref, sin_ref,
                qkv_ref, z_ref, rq_ref, rk_ref, rv_ref, rg_ref, ba_ref):
    hb = (_rms(x_ref[...]) * n_ref[...]).astype(BF16)

    def proj(seg):
        return _dot(hb, w_ref[:, seg[0]:seg[1]])

    cos = cos_ref[...]
    sin = sin_ref[...]
    qkv_ref[...] = proj(_P0_QKV)
    z_ref[...] = proj(_P0_Z).astype(BF16)
    rq_ref[...] = (_rope(proj(_P0_RQ), cos, sin) * RET_DK ** -0.5).astype(BF16)
    rk_ref[...] = _rope(proj(_P0_RK), cos, sin).astype(BF16)
    rv_ref[...] = proj(_P0_RV).astype(BF16)
    rg_ref[...] = proj(_P0_RG).astype(BF16)
    ba_ref[...] = proj(_P0_BA)


def _in0(x2d, norm, w_packed, cos, sin, seq):
    T = x2d.shape[0]
    tm = PROJ_TM
    spt = seq // tm
    row = lambda w: pl.BlockSpec((tm, w), lambda i: (i, 0))
    tab = pl.BlockSpec((tm, LANES), lambda i: (i % spt, 0))
    widths = (GDN_CONV_W, GDN_V_W, RET_QK_W, RET_QK_W, MIX_W, MIX_W, LANES)
    dtypes = (F32, BF16, BF16, BF16, BF16, BF16, F32)
    return pl.pallas_call(
        _in0_kernel,
        grid=(T // tm,),
        in_specs=[row(D_MODEL), _const_spec((1, D_MODEL)), _const_spec((D_MODEL, _P0_W)), tab, tab],
        out_specs=[row(w) for w in widths],
        out_shape=[jax.ShapeDtypeStruct((T, w), dt) for w, dt in zip(widths, dtypes)],
        compiler_params=_params("parallel"),
        name="in_proj0",
    )(x2d, norm.reshape(1, D_MODEL), w_packed, cos, sin)


def _bmm(a, b):
    return jnp.einsum("gik,gkj->gij", a.astype(BF16), b.astype(BF16), preferred_element_type=F32)


def _unit_lower_inverse(a, ri, ci):
    n = jnp.where((ri >> 3) == (ci >> 3), -a, 0.0)
    t = jnp.where(ri == ci, 1.0, 0.0) + n
    p = _bmm(n, n)
    t = t + _bmm(t, p)
    p = _bmm(p, p)
    t = t + _bmm(t, p)
    for sh in (3, 4, 5):
        e = jnp.where(((ri >> (sh + 1)) == (ci >> (sh + 1))) & ((ri >> sh) != (ci >> sh)), a, 0.0)
        t = t - _bmm(_bmm(t, e), t)
    return t


def _gdn_kernel(qkv_ref, ba_ref, z_ref, cw_ref, alog_ref, dtb_ref, gn_ref, o_ref,
                xe_ref, o_s, state_ref):
    i = pl.program_id(1)
    ts = qkv_ref.shape[0]
    C = GDN_CHUNK
    nch = ts // C
    halo = 8

    @pl.when(i == 0)
    def _():
        xe_ref[...] = jnp.zeros_like(xe_ref)
        state_ref[...] = jnp.zeros_like(state_ref)

    beta = 1.0 / (1.0 + jnp.exp(-ba_ref[...]))
    xa = ba_ref[...] + dtb_ref[...]
    softplus = jnp.maximum(xa, 0.0) + jnp.log(1.0 + jnp.exp(-jnp.abs(xa)))
    g = -jnp.exp(alog_ref[...]) * softplus
    rt = lax.broadcasted_iota(jnp.int32, (ts, ts), 0)
    ct = lax.broadcasted_iota(jnp.int32, (ts, ts), 1)
    tri = jnp.where((rt >= ct) & ((rt >> 6) == (ct >> 6)), 1.0, 0.0)
    gc = jnp.dot(tri, g, precision=HIGHEST, preferred_element_type=F32)

    x = qkv_ref[...]
    xb = x.astype(BF16)
    shift = jnp.concatenate([jnp.where(rt - ct == k, 1.0, 0.0) for k in range(1, GDN_CONV)], axis=0).astype(BF16)
    shifted = _dot(shift, xb)
    prev = xe_ref[...]
    row8 = lax.broadcasted_iota(jnp.int32, (halo, 1), 0)
    conv = cw_ref[GDN_CONV - 1:GDN_CONV, :] * x
    for k in range(1, GDN_CONV):
        sh = shifted[(k - 1) * ts:k * ts]
        head = jnp.where(row8 < k, pltpu.roll(prev, k, 0), sh[0:halo])
        conv = conv + cw_ref[GDN_CONV - 1 - k:GDN_CONV - k, :] * jnp.concatenate([head, sh[halo:]], axis=0)
    xe_ref[...] = x[ts - halo:ts]
    act = _silu(conv)
    qn, kn, vv = [], [], []
    for h in range(N_HEADS):
        lo = h * GDN_DK
        qh = act[:, lo:lo + GDN_DK]
        kh = act[:, GDN_QK_W + lo:GDN_QK_W + lo + GDN_DK]
        qn.append(qh * (lax.rsqrt(jnp.sum(qh * qh, -1, keepdims=True) + EPS) * GDN_DK ** -0.5))
        kn.append(kh * lax.rsqrt(jnp.sum(kh * kh, -1, keepdims=True) + EPS))
        vv.append(act[:, 2 * GDN_QK_W + h * HEAD_V:2 * GDN_QK_W + (h + 1) * HEAD_V])

    units = [(c, h) for c in range(nch) for h in range(N_HEADS)]
    rows = lambda x, c: x[c * C:(c + 1) * C]
    stack = lambda f: jnp.stack([f(c, h) for c, h in units], axis=0)
    q = stack(lambda c, h: rows(qn[h], c))
    k = stack(lambda c, h: rows(kn[h], c))
    v = stack(lambda c, h: rows(vv[h], c))
    gct = [rows(gc, c).T for c in range(nch)]
    gcol = stack(lambda c, h: rows(gc, c)[:, N_HEADS + h:N_HEADS + h + 1])
    grow = stack(lambda c, h: gct[c][N_HEADS + h:N_HEADS + h + 1, :])
    bcol = stack(lambda c, h: rows(beta, c)[:, h:h + 1])
    glast = gcol[:, C - 1:C, :]

    ri = lax.broadcasted_iota(jnp.int32, (1, C, C), 1)
    ci = lax.broadcasted_iota(jnp.int32, (1, C, C), 2)
    causal = ri >= ci
    decay = jnp.where(causal, jnp.exp(jnp.where(causal, gcol - grow, 0.0)), 0.0)
    kb = k.astype(BF16)
    kk = jnp.einsum("gid,gjd->gij", kb, kb, preferred_element_type=F32)
    a_mat = jnp.where(ri > ci, kk * decay * bcol, 0.0)
    t_inv = _unit_lower_inverse(a_mat, ri, ci)
    eg = jnp.exp(gcol)
    rhs = jnp.concatenate([v * bcol, k * (bcol * eg)], axis=2).astype(BF16)
    sol = jnp.einsum("gik,gkj->gij", t_inv.astype(BF16), rhs, preferred_element_type=F32)
    solb = sol.astype(BF16)
    qk = (jnp.einsum("gid,gjd->gij", q.astype(BF16), kb, preferred_element_type=F32) * decay).astype(BF16)
    qk_uw = jnp.einsum("gij,gje->gie", qk, solb, preferred_element_type=F32)
    kd = k * jnp.exp(glast - gcol)
    kdt = jnp.stack([kd[n].T for n in range(len(units))], axis=0).astype(BF16)
    kd_uw = jnp.einsum("gdc,gce->gde", kdt, solb, preferred_element_type=F32)
    lhs_all = jnp.concatenate([kd_uw[:, :, HEAD_V:], q * eg - qk_uw[:, :, HEAD_V:]], axis=1).astype(BF16)
    c_all = kd_uw[:, :, :HEAD_V]
    o0_all = qk_uw[:, :, :HEAD_V]
    egl = jnp.exp(glast)

    states = [state_ref[h] for h in range(N_HEADS)]
    for n, (c, h) in enumerate(units):
        r = _dot(lhs_all[n], states[h].astype(BF16))
        o_s[c * C:(c + 1) * C, h * HEAD_V:(h + 1) * HEAD_V] = o0_all[n] + r[GDN_DK:]
        states[h] = states[h] * egl[n] + c_all[n] - r[:GDN_DK]
    for h in range(N_HEADS):
        state_ref[h] = states[h]

    for h in range(N_HEADS):
        lo = h * HEAD_V
        zh = z_ref[:, lo:lo + HEAD_V].astype(F32)
        o_ref[:, lo:lo + HEAD_V] = (_rms(o_s[:, lo:lo + HEAD_V]) * gn_ref[...] * _silu(zh)).astype(BF16)


def _gdn(qkv, ba, z, conv_w, a_log, dt_bias, gdn_norm, batch, seq):
    T = qkv.shape[0]
    ts = GDN_TS
    spt = seq // ts
    row = lambda w: pl.BlockSpec((ts, w), lambda bi, i: (bi * spt + i, 0))
    pad4 = lambda p: jnp.pad(p.astype(F32), (N_HEADS, LANES - 2 * N_HEADS)).reshape(1, LANES)
    return pl.pallas_call(
        _gdn_kernel,
        grid=(batch, spt),
        in_specs=[row(GDN_CONV_W), row(LANES), row(GDN_V_W),
                  _const_spec((GDN_CONV, GDN_CONV_W)), _const_spec((1, LANES)), _const_spec((1, LANES)),
                  _const_spec((1, HEAD_V))],
        out_specs=row(GDN_V_W),
        out_shape=jax.ShapeDtypeStruct((T, GDN_V_W), BF16),
        scratch_shapes=[
            pltpu.VMEM((8, GDN_CONV_W), F32),
            pltpu.VMEM((ts, GDN_V_W), F32),
            pltpu.VMEM((N_HEADS, GDN_DK, HEAD_V), F32),
        ],
        compiler_params=_params("arbitrary", "arbitrary"),
        name="gated_deltanet",
    )(qkv, ba, z, conv_w.astype(F32), pad4(a_log), pad4(dt_bias), gdn_norm.reshape(1, HEAD_V))


def _ret_kernel(q_ref, k_ref, v_ref, g_ref, o_ref, state_ref, dm_ref):
    i = pl.program_id(1)
    c = q_ref.shape[0]

    @pl.when(i == 0)
    def _():
        state_ref[...] = jnp.zeros_like(state_ref)

    @pl.when((pl.program_id(0) == 0) & (i == 0))
    def _():
        ri = lax.broadcasted_iota(jnp.int32, (c, c), 0)
        ci = lax.broadcasted_iota(jnp.int32, (c, c), 1)
        rel = (ri - ci).astype(F32)
        for h in range(N_HEADS):
            lg = math.log(1.0 - 2.0 ** (-5.0 - h))
            dm_ref[h] = jnp.where(rel >= 0, jnp.exp(jnp.where(rel >= 0, rel, 0.0) * lg), 0.0)

    pos = lax.broadcasted_iota(jnp.int32, (c, 1), 0).astype(F32)
    lane = lax.broadcasted_iota(jnp.int32, (1, LANES), 1)
    for h in range(N_HEADS):
        lg = math.log(1.0 - 2.0 ** (-5.0 - h))
        grp = (h // 2) * LANES
        own = ((lane >> 5) & 1) == h % 2
        qm = jnp.where(own, q_ref[:, grp:grp + LANES], 0)
        km = jnp.where(own, k_ref[:, grp:grp + LANES], 0)
        v = v_ref[:, h * HEAD_V:(h + 1) * HEAD_V]
        intra = _dot((_dot_nt(qm, km) * dm_ref[h]).astype(BF16), v)
        state = state_ref[h]
        inter = _dot(qm, state.astype(BF16)) * jnp.exp((pos + 1.0) * lg)
        k_sc = (km.astype(F32) * jnp.exp((c - 1.0 - pos) * lg)).astype(BF16)
        state_ref[h] = state * math.exp(c * lg) + _dot_tn(k_sc, v)
        gate = _silu(g_ref[:, h * HEAD_V:(h + 1) * HEAD_V].astype(F32))
        o_ref[:, h * HEAD_V:(h + 1) * HEAD_V] = (_rms(intra + inter) * gate).astype(BF16)


def _retention(rq, rk, rv, rg, batch, seq):
    T = rq.shape[0]
    c = RET_C
    spt = seq // c
    row = lambda w: pl.BlockSpec((c, w), lambda bi, i: (bi * spt + i, 0))
    return pl.pallas_call(
        _ret_kernel,
        grid=(batch, spt),
        in_specs=[row(RET_QK_W), row(RET_QK_W), row(MIX_W), row(MIX_W)],
        out_specs=row(MIX_W),
        out_shape=jax.ShapeDtypeStruct((T, MIX_W), BF16),
        scratch_shapes=[pltpu.VMEM((N_HEADS, LANES, HEAD_V), F32), pltpu.VMEM((N_HEADS, c, c), F32)],
        compiler_params=_params("arbitrary", "arbitrary"),
        name="retention",
    )(rq, rk, rv, rg)


def _pack_w_in1(w_in):
    o = np.cumsum((0,) + (MIX_W,) * 6)
    cq, ck, cv, dq, dk, dv = (w_in[:, o[i]:o[i + 1]] for i in range(6))
    perm = _diff_perm()
    w = jnp.concatenate([cq, ck, cv, dq[:, perm], dk[:, perm]], axis=1).astype(BF16)
    return w, dv.T.astype(BF16)


def _in1_kernel(x_ref, n_ref, w_ref, wvt_ref, cosa_ref, sina_ref, cosb_ref, sinb_ref,
                cq_ref, ck_ref, cv_ref, dq_ref, dk_ref, dvt_ref):
    hb = (_rms(x_ref[...]) * n_ref[...]).astype(BF16)

    def proj(j):
        return _dot(hb, w_ref[:, j * MIX_W:(j + 1) * MIX_W])

    cq_ref[...] = _rope(proj(0), cosa_ref[...], sina_ref[...]) * (DIL_DH ** -0.5 * LOG2E)
    ck_ref[...] = _rope(proj(1), cosa_ref[...], sina_ref[...])
    cv_ref[...] = proj(2)
    dq_ref[...] = (_rope(proj(3), cosb_ref[...], sinb_ref[...]) * (DIFF_DK ** -0.5 * LOG2E)).astype(BF16)
    dk_ref[...] = _rope(proj(4), cosb_ref[...], sinb_ref[...]).astype(BF16)
    vt = _dot_nt(wvt_ref[...], hb).astype(BF16)
    for j in range(dvt_ref.shape[0]):
        dvt_ref[j] = vt[:, j * DIFF_TK:(j + 1) * DIFF_TK]


def _in1(x2d, norm, w_packed, wv_t, cosa, sina, cosb, sinb, seq):
    T = x2d.shape[0]
    tm = PROJ_TM
    spt = seq // tm
    vblk = tm // DIFF_TK
    row = lambda w: pl.BlockSpec((tm, w), lambda i: (i, 0))
    tab = pl.BlockSpec((tm, LANES), lambda i: (i % spt, 0))
    out_shape = [jax.ShapeDtypeStruct((T, MIX_W), dt) for dt in (F32, F32, F32, BF16, BF16)]
    out_shape.append(jax.ShapeDtypeStruct((T // DIFF_TK, MIX_W, DIFF_TK), BF16))
    return pl.pallas_call(
        _in1_kernel,
        grid=(T // tm,),
        in_specs=[row(D_MODEL), _const_spec((1, D_MODEL)), _const_spec((D_MODEL, 5 * MIX_W)),
                  _const_spec((MIX_W, D_MODEL)), tab, tab, tab, tab],
        out_specs=[row(MIX_W)] * 5 + [pl.BlockSpec((vblk, MIX_W, DIFF_TK), lambda i: (i, 0, 0))],
        out_shape=out_shape,
        compiler_params=_params("parallel"),
        name="in_proj1",
    )(x2d, norm.reshape(1, D_MODEL), w_packed, wv_t, cosa, sina, cosb, sinb)


def _dil_kernel(q_ref, kp_ref, kc_ref, vp_ref, vc_ref, o_ref, kcat, vcat, ob, mb, lb):
    i = pl.program_id(2)
    blk = q_ref.shape[0]
    nk = DIL_KEYS
    kcat[0:blk, :] = kp_ref[...]
    kcat[blk:2 * blk, :] = kc_ref[...]
    vcat[0:blk, :] = vp_ref[...]
    vcat[blk:2 * blk, :] = vc_ref[...]

    qi = lax.broadcasted_iota(jnp.int32, (nk, 2 * nk), 0)
    kj = lax.broadcasted_iota(jnp.int32, (nk, 2 * nk), 1)
    dist = qi + nk - kj
    band = (dist >= 0) & (dist <= nk)
    ones_v = jnp.ones((2 * nk, LANES), BF16)

    for gi, (_, d) in enumerate(DIL_PATTERNS):
        span = nk * d

        def units(nb, carry, gi=gi, d=d, span=span):
            pos = []
            for u in range(DIL_GROUP):
                n = nb * DIL_GROUP + u
                sub = n // d
                qs = sub * span + n % d
                pos.append((qs, qs + blk - span, sub))
            qs_ = [q_ref[pl.ds(qs, nk, stride=d), :].astype(BF16) for qs, _, _ in pos]
            ks_ = [kcat[pl.ds(ks, 2 * nk, stride=d), :].astype(BF16) for _, ks, _ in pos]
            vs_ = [jnp.concatenate([vcat[pl.ds(ks, 2 * nk, stride=d), :].astype(BF16), ones_v], axis=1)
                   for _, ks, _ in pos]
            ss = [_dot_nt(q, k) for q, k in zip(qs_, ks_)]
            ps, ms = [], []
            for s, (_, _, sub) in zip(ss, pos):
                first_key = jnp.where((i > 0) | (sub > 0), 0, nk)
                sb = jnp.where(band & (kj >= first_key), s.astype(BF16), -jnp.inf)
                m = jnp.max(sb, axis=-1, keepdims=True)
                ps.append(jnp.exp2(sb - m))
                ms.append(m.astype(F32))
            os_ = [_dot(p, v) for p, v in zip(ps, vs_)]
            for (qs, _, _), o, m in zip(pos, os_, ms):
                ob[gi, pl.ds(qs, nk, stride=d), :] = o[:, :HEAD_V]
                lb[gi, pl.ds(qs, nk, stride=d), :] = o[:, HEAD_V:]
                mb[gi, pl.ds(qs, nk, stride=d), :] = jnp.broadcast_to(m, (nk, LANES))
            return carry

        lax.fori_loop(0, blk // nk // DIL_GROUP, units, 0)

    m_all = jnp.maximum(jnp.maximum(mb[0], mb[1]), mb[2])
    num = jnp.zeros((blk, HEAD_V), F32)
    den = jnp.zeros((blk, LANES), F32)
    for gi in range(len(DIL_PATTERNS)):
        sc = jnp.exp2(mb[gi] - m_all)
        num = num + ob[gi] * sc
        den = den + lb[gi] * sc
    o_ref[...] = (num / den).astype(BF16)


def _dilated(cq, ck, cv, batch, seq):
    blk = DIL_BLK
    q3, k3, v3 = (t.reshape(batch, seq, MIX_W) for t in (cq, ck, cv))
    cur = pl.BlockSpec((None, blk, DIL_DH), lambda b, h, i: (b, i, h))
    prev = pl.BlockSpec((None, blk, DIL_DH), lambda b, h, i: (b, jnp.maximum(i - 1, 0), h))
    nb = len(DIL_PATTERNS)
    out = pl.pallas_call(
        _dil_kernel,
        grid=(batch, N_HEADS, seq // blk),
        in_specs=[cur, prev, cur, prev, cur],
        out_specs=cur,
        out_shape=jax.ShapeDtypeStruct((batch, seq, MIX_W), BF16),
        scratch_shapes=[
            pltpu.VMEM((2 * blk, DIL_DH), F32),
            pltpu.VMEM((2 * blk, DIL_DH), F32),
            pltpu.VMEM((nb, blk, HEAD_V), F32),
            pltpu.VMEM((nb, blk, LANES), F32),
            pltpu.VMEM((nb, blk, LANES), F32),
        ],
        compiler_params=_params("parallel", "parallel", "arbitrary"),
        name="dilated_attention",
    )(q3, k3, k3, v3, v3)
    return out.reshape(batch * seq, MIX_W)


def _diff_kernel(q_ref, k_ref, vt_ref, lq1_ref, lk1_ref, lq2_ref, lk2_ref, dn_ref, o_ref,
                 *stats, lambda_init):
    i = pl.program_id(2)
    tq = q_ref.shape[0]
    tk = DIFF_TK
    assert tq == tk, "the diagonal handling assumes square tiles"
    gw = DIFF_GW
    ngrp = 2 * tq // gw
    nst = DIFF_HPS * ngrp
    m_s, acc_s = stats[0:nst], stats[nst:2 * nst]
    lane = lax.broadcasted_iota(jnp.int32, (1, LANES), 1)
    first = ((lane >> 5) & 1) == 0
    qgs = []
    for hh in range(DIFF_HPS):
        q = q_ref[:, hh * LANES:(hh + 1) * LANES]
        qq = jnp.concatenate([jnp.where(first, q, 0), jnp.where(first, 0, q)], axis=0)
        qgs.append([qq[g * gw:(g + 1) * gw] for g in range(ngrp)])
    for r in m_s:
        r[...] = jnp.full_like(r, -jnp.inf)
    for r in acc_s:
        r[...] = jnp.zeros_like(r)

    def scores(kt, c0, nr, hh, groups):
        k0 = pl.multiple_of(kt * tk, tk) + c0
        k_rows = k_ref[pl.ds(k0, nr), hh * LANES:(hh + 1) * LANES]
        return [_dot_nt(k_rows, qgs[hh][g]) for g in groups]

    def update(ss, kt, c0, nr, hh, groups, tri):
        vt = jnp.concatenate([vt_ref[kt, hh * HEAD_V:(hh + 1) * HEAD_V, c0:c0 + nr],
                              jnp.ones((8, nr), BF16)], axis=0)
        ps, alphas = [], []
        for g, s in zip(groups, ss):
            n = hh * ngrp + g
            sb = s.astype(BF16)
            if g in tri:
                key = lax.broadcasted_iota(jnp.int32, (nr, gw), 0)
                qry = lax.broadcasted_iota(jnp.int32, (nr, gw), 1)
                sb = jnp.where(qry >= key, sb, -jnp.inf)
            m_prev = m_s[n][...]
            m_new = jnp.maximum(m_prev, jnp.max(sb, axis=0, keepdims=True).astype(F32))
            alphas.append(jnp.exp2(m_prev - m_new))
            ps.append(jnp.exp2(sb - m_new.astype(BF16)))
            m_s[n][...] = m_new
        pvs = [_dot(vt, p) for p in ps]
        for g, alpha, pv in zip(groups, alphas, pvs):
            n = hh * ngrp + g
            acc_s[n][...] = alpha * acc_s[n][...] + pv

    def run(kt, units):
        sss = [scores(kt, c0, nr, hh, groups) for c0, nr, hh, groups, _ in units]
        for ss, (c0, nr, hh, groups, tri) in zip(sss, units):
            update(ss, kt, c0, nr, hh, groups, tri)

    every = list(range(ngrp))
    full = [(0, tk, hh, every, ()) for hh in range(DIFF_HPS)]

    def body(kt, carry):
        run(kt, full)
        return carry

    lax.fori_loop(0, i, body, 0)

    diagonal = []
    for sub in range(tk // gw):
        keep = [g for g in every if (g * gw) % tq >= sub * gw]
        on_diag = tuple(g for g in keep if (g * gw) % tq == sub * gw)
        diagonal += [(sub * gw, gw, hh, keep, on_diag) for hh in range(DIFF_HPS)]
    run(i, diagonal)

    lam = (jnp.exp(jnp.sum(lq1_ref[...] * lk1_ref[...], axis=-1, keepdims=True))
           - jnp.exp(jnp.sum(lq2_ref[...] * lk2_ref[...], axis=-1, keepdims=True)) + lambda_init)
    half = ngrp // 2
    for hh in range(DIFF_HPS):
        for g in range(half):
            n1, n2 = hh * ngrp + g, hh * ngrp + g + half
            a1, a2 = acc_s[n1][...], acc_s[n2][...]
            o = a1[:HEAD_V] / a1[HEAD_V:HEAD_V + 1] - lam * (a2[:HEAD_V] / a2[HEAD_V:HEAD_V + 1])
            o = o * lax.rsqrt(jnp.mean(o * o, axis=0, keepdims=True) + EPS) * dn_ref[...] * (1.0 - lambda_init)
            o_ref[g * gw:(g + 1) * gw, hh * HEAD_V:(hh + 1) * HEAD_V] = o.T.astype(BF16)


def _differential(dq, dk, dvt, lq1, lk1, lq2, lk2, diff_norm, lambda_init, batch, seq):
    tq = DIFF_TQ
    nkt = seq // DIFF_TK
    nst = DIFF_HPS * 2 * tq // DIFF_GW
    hw = DIFF_HPS * LANES
    q3, k3 = (t.reshape(batch, seq, MIX_W) for t in (dq, dk))
    qspec = pl.BlockSpec((None, tq, hw), lambda b, h, i: (b, i, h))
    kspec = pl.BlockSpec((None, seq, hw), lambda b, h, i: (b, 0, h))
    vspec = pl.BlockSpec((nkt, hw, DIFF_TK), lambda b, h, i: (b, h, 0))
    vec = lambda p: p.reshape(1, DIFF_DK).astype(F32)
    out = pl.pallas_call(
        functools.partial(_diff_kernel, lambda_init=lambda_init),
        grid=(batch, N_HEADS // DIFF_HPS, seq // tq),
        in_specs=[qspec, kspec, vspec] + [_const_spec((1, DIFF_DK))] * 4 + [_const_spec((HEAD_V, 1))],
        out_specs=qspec,
        out_shape=jax.ShapeDtypeStruct((batch, seq, MIX_W), BF16),
        scratch_shapes=([pltpu.VMEM((1, DIFF_GW), F32)] * nst
                        + [pltpu.VMEM((HEAD_V + 8, DIFF_GW), F32)] * nst),
        compiler_params=_params("parallel", "parallel", "arbitrary"),
        name="differential_attention",
    )(q3, k3, dvt, vec(lq1), vec(lk1), vec(lq2), vec(lk2), diff_norm.reshape(HEAD_V, 1))
    return out.reshape(batch * seq, MIX_W)


def kernel(x, l0_ffn1_norm, l0_ffn1_w_up, l0_ffn1_w_down, l0_mix_norm, l0_w_in, l0_conv_w, l0_a_log,
           l0_dt_bias, l0_gdn_norm, l0_w_out, l0_ffn2_norm, l0_ffn2_w_up, l0_ffn2_w_down,
           l1_ffn1_norm, l1_ffn1_w_up, l1_ffn1_w_down, l1_mix_norm, l1_w_in, l1_lambda_q1, l1_lambda_k1,
           l1_lambda_q2, l1_lambda_k2, l1_diff_norm, l1_w_out, l1_ffn2_norm, l1_ffn2_w_up, l1_ffn2_w_down,
           final_norm):
    batch, seq, _ = x.shape
    cos64, sin64 = _rope_tables(seq, 64)
    cos128, sin128 = _rope_tables(seq, 128)
    xf = x.reshape(batch * seq, D_MODEL)

    xf = _ffn(xf, l0_ffn1_norm, l0_ffn1_w_up, l0_ffn1_w_down)
    qkv, z, rq, rk, rv, rg, ba = _in0(xf, l0_mix_norm, _pack_w_in0(l0_w_in), cos64, sin64, seq)
    o_a = _gdn(qkv, ba, z, l0_conv_w, l0_a_log, l0_dt_bias, l0_gdn_norm, batch, seq)
    o_b = _retention(rq, rk, rv, rg, batch, seq)
    xf = _ffn(xf, l0_ffn2_norm, l0_ffn2_w_up, l0_ffn2_w_down, mix=(o_a, o_b, l0_w_out))

    xf = _ffn(xf, l1_ffn1_norm, l1_ffn1_w_up, l1_ffn1_w_down)
    w1, wv_t = _pack_w_in1(l1_w_in)
    cq, ck, cv, dq, dk, dvt = _in1(xf, l1_mix_norm, w1, wv_t, cos128, sin128, cos64, sin64, seq)
    o_c = _dilated(cq, ck, cv, batch, seq)
    lambda_init = 0.8 - 0.6 * math.exp(-0.3 * 1)
    o_d = _differential(dq, dk, dvt, l1_lambda_q1, l1_lambda_k1, l1_lambda_q2, l1_lambda_k2,
                        l1_diff_norm, lambda_init, batch, seq)
    xf = _ffn(xf, l1_ffn2_norm, l1_ffn2_w_up, l1_ffn2_w_down, mix=(o_c, o_d, l1_w_out), final_norm=final_norm)
    return xf.reshape(batch, seq, D_MODEL)
```

```python
import functools
import math

import numpy as np
import jax
import jax.numpy as jnp
from jax import lax
from jax.experimental import pallas as pl
from jax.experimental.pallas import tpu as pltpu

F32 = jnp.float32
BF16 = jnp.bfloat16
HIGHEST = lax.Precision.HIGHEST

EPS = 1e-6
ROPE_THETA = 10000.0
D_MODEL = 1024
D_FF = 2816
N_HEADS = 4
HEAD_V = 128
GDN_DK = 128
GDN_CONV = 4
GDN_CHUNK = 64
GDN_QK_W = N_HEADS * GDN_DK
GDN_V_W = N_HEADS * HEAD_V
GDN_CONV_W = 2 * GDN_QK_W + GDN_V_W
RET_DK = 64
RET_QK_W = N_HEADS * RET_DK
DIL_DH = 128
DIL_PATTERNS = ((128, 1), (512, 4), (2048, 16))
DIL_KEYS = 128
DIFF_DK = 64
MIX_W = N_HEADS * HEAD_V

LANES = 128
VMEM_LIMIT = 56 * 1024 * 1024

FFN_TM = 1024
FFN_TF = 256
PROJ_TM = 512
GDN_TS = 256
RET_C = 256
RET_TILE = 512
DIL_BLK = 2048
DIFF_TQ = 512
DIFF_TK = 512
DIFF_HPS = 4
DIFF_GW = 256
LOG2E = math.log2(math.e)


def _rms(x):
    return x * lax.rsqrt(jnp.mean(x * x, axis=-1, keepdims=True) + EPS)


def _silu(x):
    return x * (1.0 / (1.0 + jnp.exp(-x)))


def _dot(a, b):
    return jnp.dot(a, b, preferred_element_type=F32)


def _dot_nt(a, b):
    return lax.dot_general(a, b, (((1,), (1,)), ((), ())), preferred_element_type=F32)


def _dot_tn(a, b):
    return lax.dot_general(a, b, (((0,), (0,)), ((), ())), preferred_element_type=F32)


def _params(*sem):
    return pltpu.CompilerParams(dimension_semantics=sem, vmem_limit_bytes=VMEM_LIMIT)


def _const_spec(shape):
    nd = len(shape)
    return pl.BlockSpec(shape, lambda *_: (0,) * nd, pipeline_mode=pl.Buffered(1))


def _ffn_kernel(*refs, mixed, final):
    refs = list(refs)
    x_ref = refs.pop(0)
    x = x_ref[...]
    if mixed:
        oa_ref, ob_ref, wo_ref = refs.pop(0), refs.pop(0), refs.pop(0)
        x = x + _dot(oa_ref[...], wo_ref[0:MIX_W, :]) + _dot(ob_ref[...], wo_ref[MIX_W:2 * MIX_W, :])
    n_ref, wup_ref, wd_ref = refs.pop(0), refs.pop(0), refs.pop(0)
    fn_ref = refs.pop(0) if final else None
    o_ref, acc_ref = refs
    hb = (_rms(x) * n_ref[...]).astype(BF16)
    for c in range(D_FF // FFN_TF):
        lo = c * FFN_TF
        g = _dot(hb, wup_ref[:, lo:lo + FFN_TF])
        u = _dot(hb, wup_ref[:, D_FF + lo:D_FF + lo + FFN_TF])
        a = (_silu(g) * u).astype(BF16)
        d = _dot(a, wd_ref[lo:lo + FFN_TF, :])
        if c == 0:
            acc_ref[...] = d
        else:
            acc_ref[...] += d
    y = x + 0.5 * acc_ref[...]
    if final:
        y = _rms(y) * fn_ref[...]
    o_ref[...] = y


def _ffn(x2d, norm, w_up, w_down, mix=None, final_norm=None):
    T = x2d.shape[0]
    row = lambda w: pl.BlockSpec((FFN_TM, w), lambda i: (i, 0))
    args, specs = [x2d], [row(D_MODEL)]
    if mix is not None:
        o_a, o_b, w_out = mix
        args += [o_a, o_b, w_out.astype(BF16)]
        specs += [row(MIX_W), row(MIX_W), _const_spec((2 * MIX_W, D_MODEL))]
    args += [norm.reshape(1, D_MODEL), w_up.astype(BF16), w_down.astype(BF16)]
    specs += [_const_spec((1, D_MODEL)), _const_spec((D_MODEL, 2 * D_FF)), _const_spec((D_FF, D_MODEL))]
    if final_norm is not None:
        args.append(final_norm.reshape(1, D_MODEL))
        specs.append(_const_spec((1, D_MODEL)))
    return pl.pallas_call(
        functools.partial(_ffn_kernel, mixed=mix is not None, final=final_norm is not None),
        grid=(T // FFN_TM,),
        in_specs=specs,
        out_specs=row(D_MODEL),
        out_shape=jax.ShapeDtypeStruct((T, D_MODEL), F32),
        scratch_shapes=[pltpu.VMEM((FFN_TM, D_MODEL), F32)],
        compiler_params=_params("parallel"),
        name="ffn_final" if final_norm is not None else ("ffn_mix" if mix is not None else "ffn"),
    )(*args)


def _rope_tables(seq, dim):
    half = dim // 2
    inv = ROPE_THETA ** (-jnp.arange(0, dim, 2, dtype=F32) / dim)
    ang = jnp.arange(seq, dtype=F32)[:, None] * inv[None, :]
    reps = (LANES // 2) // half
    cos = jnp.tile(jnp.cos(ang), (1, 2 * reps))
    sin = jnp.tile(jnp.sin(ang), (1, reps))
    return cos, jnp.concatenate([-sin, sin], axis=1)


def _rope(x, cos, sin):
    outs = []
    for g in range(x.shape[1] // LANES):
        xg = x[:, g * LANES:(g + 1) * LANES]
        outs.append(xg * cos + pltpu.roll(xg, LANES // 2, 1) * sin)
    return outs[0] if len(outs) == 1 else jnp.concatenate(outs, axis=1)


_P0_QKV = (0, GDN_CONV_W)
_P0_Z = (_P0_QKV[1], _P0_QKV[1] + GDN_V_W)
_P0_RQ = (_P0_Z[1], _P0_Z[1] + RET_QK_W)
_P0_RK = (_P0_RQ[1], _P0_RQ[1] + RET_QK_W)
_P0_RV = (_P0_RK[1], _P0_RK[1] + MIX_W)
_P0_RG = (_P0_RV[1], _P0_RV[1] + MIX_W)
_P0_BA = (_P0_RG[1], _P0_RG[1] + LANES)
_P0_W = _P0_BA[1]


def _ret_perm():
    perm = np.zeros(RET_QK_W, np.int32)
    for h in range(N_HEADS):
        for d in range(RET_DK):
            new = (h // 2) * LANES + (d // 32) * 64 + (h % 2) * 32 + d % 32
            perm[new] = h * RET_DK + d
    return perm


def _diff_perm():
    perm = np.zeros(N_HEADS * 2 * DIFF_DK, np.int32)
    for h in range(N_HEADS):
        for c in range(2):
            for d in range(DIFF_DK):
                new = h * LANES + (d // 32) * 64 + c * 32 + d % 32
                perm[new] = h * 2 * DIFF_DK + c * DIFF_DK + d
    return perm


def _pack_w_in0(w_in):
    o = np.cumsum((0, GDN_CONV_W, GDN_V_W, N_HEADS, N_HEADS, RET_QK_W, RET_QK_W, MIX_W, MIX_W))
    qkv, z, b, a, rq, rk, rv, rg = (w_in[:, o[i]:o[i + 1]] for i in range(8))
    perm = _ret_perm()
    pad = jnp.zeros((D_MODEL, LANES - 2 * N_HEADS), w_in.dtype)
    cols = [qkv, z, rq[:, perm], rk[:, perm], rv, rg, b, a, pad]
    return jnp.concatenate(cols, axis=1).astype(BF16)


def _in0_kernel(x_ref, n_ref, w_ref, cos_ref, sin_ref,
                qkv_ref, z_ref, rq_ref, rk_ref, rv_ref, rg_ref, ba_ref):
    hb = (_rms(x_ref[...]) * n_ref[...]).astype(BF16)

    def proj(seg):
        return _dot(hb, w_ref[:, seg[0]:seg[1]])

    cos = cos_ref[...]
    sin = sin_ref[...]
    qkv_ref[...] = proj(_P0_QKV)
    z_ref[...] = proj(_P0_Z).astype(BF16)
    rq_ref[...] = (_rope(proj(_P0_RQ), cos, sin) * RET_DK ** -0.5).astype(BF16)
    rk_ref[...] = _rope(proj(_P0_RK), cos, sin).astype(BF16)
    rv_ref[...] = proj(_P0_RV).astype(BF16)
    rg_ref[...] = proj(_P0_RG).astype(BF16)
    ba_ref[...] = proj(_P0_BA)


def _in0(x2d, norm, w_packed, cos, sin, seq):
    T = x2d.shape[0]
    tm = PROJ_TM
    spt = seq // tm
    row = lambda w: pl.BlockSpec((tm, w), lambda i: (i, 0))
    tab = pl.BlockSpec((tm, LANES), lambda i: (i % spt, 0))
    widths = (GDN_CONV_W, GDN_V_W, RET_QK_W, RET_QK_W, MIX_W, MIX_W, LANES)
    dtypes = (F32, BF16, BF16, BF16, BF16, BF16, F32)
    return pl.pallas_call(
        _in0_kernel,
        grid=(T // tm,),
        in_specs=[row(D_MODEL), _const_spec((1, D_MODEL)), _const_spec((D_MODEL, _P0_W)), tab, tab],
        out_specs=[row(w) for w in widths],
        out_shape=[jax.ShapeDtypeStruct((T, w), dt) for w, dt in zip(widths, dtypes)],
        compiler_params=_params("parallel"),
        name="in_proj0",
    )(x2d, norm.reshape(1, D_MODEL), w_packed, cos, sin)


def _bmm(a, b):
    return jnp.einsum("gik,gkj->gij", a.astype(BF16), b.astype(BF16), preferred_element_type=F32)


def _unit_lower_inverse(a, ri, ci):
    n = jnp.where((ri >> 3) == (ci >> 3), -a, 0.0)
    t = jnp.where(ri == ci, 1.0, 0.0) + n
    p = _bmm(n, n)
    t = t + _bmm(t, p)
    p = _bmm(p, p)
    t = t + _bmm(t, p)
    for sh in (3, 4, 5):
        e = jnp.where(((ri >> (sh + 1)) == (ci >> (sh + 1))) & ((ri >> sh) != (ci >> sh)), a, 0.0)
        t = t - _bmm(_bmm(t, e), t)
    return t


def _gdn_kernel(qkv_ref, ba_ref, z_ref, cw_ref, alog_ref, dtb_ref, gn_ref, o_ref,
                xe_ref, o_s, state_ref):
    i = pl.program_id(1)
    ts = qkv_ref.shape[0]
    C = GDN_CHUNK
    nch = ts // C
    halo = 8

    @pl.when(i == 0)
    def _():
        xe_ref[...] = jnp.zeros_like(xe_ref)
        state_ref[...] = jnp.zeros_like(state_ref)

    beta = 1.0 / (1.0 + jnp.exp(-ba_ref[...]))
    xa = ba_ref[...] + dtb_ref[...]
    softplus = jnp.maximum(xa, 0.0) + jnp.log(1.0 + jnp.exp(-jnp.abs(xa)))
    g = -jnp.exp(alog_ref[...]) * softplus
    rt = lax.broadcasted_iota(jnp.int32, (ts, ts), 0)
    ct = lax.broadcasted_iota(jnp.int32, (ts, ts), 1)
    tri = jnp.where((rt >= ct) & ((rt >> 6) == (ct >> 6)), 1.0, 0.0)
    gc = jnp.dot(tri, g, precision=HIGHEST, preferred_element_type=F32)

    x = qkv_ref[...]
    xb = x.astype(BF16)
    shift = jnp.concatenate([jnp.where(rt - ct == k, 1.0, 0.0) for k in range(1, GDN_CONV)], axis=0).astype(BF16)
    shifted = _dot(shift, xb)
    prev = xe_ref[...]
    row8 = lax.broadcasted_iota(jnp.int32, (halo, 1), 0)
    conv = cw_ref[GDN_CONV - 1:GDN_CONV, :] * x
    for k in range(1, GDN_CONV):
        sh = shifted[(k - 1) * ts:k * ts]
        head = jnp.where(row8 < k, pltpu.roll(prev, k, 0), sh[0:halo])
        conv = conv + cw_ref[GDN_CONV - 1 - k:GDN_CONV - k, :] * jnp.concatenate([head, sh[halo:]], axis=0)
    xe_ref[...] = x[ts - halo:ts]
    act = _silu(conv)
    qn, kn, vv = [], [], []
    for h in range(N_HEADS):
        lo = h * GDN_DK
        qh = act[:, lo:lo + GDN_DK]
        kh = act[:, GDN_QK_W + lo:GDN_QK_W + lo + GDN_DK]
        qn.append(qh * (lax.rsqrt(jnp.sum(qh * qh, -1, keepdims=True) + EPS) * GDN_DK ** -0.5))
        kn.append(kh * lax.rsqrt(jnp.sum(kh * kh, -1, keepdims=True) + EPS))
        vv.append(act[:, 2 * GDN_QK_W + h * HEAD_V:2 * GDN_QK_W + (h + 1) * HEAD_V])

    units = [(c, h) for c in range(nch) for h in range(N_HEADS)]
    rows = lambda x, c: x[c * C:(c + 1) * C]
    stack = lambda f: jnp.stack([f(c, h) for c, h in units], axis=0)
    q = stack(lambda c, h: rows(qn[h], c))
    k = stack(lambda c, h: rows(kn[h], c))
    v = stack(lambda c, h: rows(vv[h], c))
    gct = [rows(gc, c).T for c in range(nch)]
    gcol = stack(lambda c, h: rows(gc, c)[:, N_HEADS + h:N_HEADS + h + 1])
    grow = stack(lambda c, h: gct[c][N_HEADS + h:N_HEADS + h + 1, :])
    bcol = stack(lambda c, h: rows(beta, c)[:, h:h + 1])
    glast = gcol[:, C - 1:C, :]

    ri = lax.broadcasted_iota(jnp.int32, (1, C, C), 1)
    ci = lax.broadcasted_iota(jnp.int32, (1, C, C), 2)
    causal = ri >= ci
    decay = jnp.where(causal, jnp.exp(jnp.where(causal, gcol - grow, 0.0)), 0.0)
    kb = k.astype(BF16)
    kk = jnp.einsum("gid,gjd->gij", kb, kb, preferred_element_type=F32)
    a_mat = jnp.where(ri > ci, kk * decay * bcol, 0.0)
    t_inv = _unit_lower_inverse(a_mat, ri, ci)
    eg = jnp.exp(gcol)
    rhs = jnp.concatenate([v * bcol, k * (bcol * eg)], axis=2).astype(BF16)
    sol = jnp.einsum("gik,gkj->gij", t_inv.astype(BF16), rhs, preferred_element_type=F32)
    solb = sol.astype(BF16)
    qk = (jnp.einsum("gid,gjd->gij", q.astype(BF16), kb, preferred_element_type=F32) * decay).astype(BF16)
    qk_uw = jnp.einsum("gij,gje->gie", qk, solb, preferred_element_type=F32)
    kd = k * jnp.exp(glast - gcol)
    kdt = jnp.stack([kd[n].T for n in range(len(units))], axis=0).astype(BF16)
    kd_uw = jnp.einsum("gdc,gce->gde", kdt, solb, preferred_element_type=F32)
    lhs_all = jnp.concatenate([kd_uw[:, :, HEAD_V:], q * eg - qk_uw[:, :, HEAD_V:]], axis=1).astype(BF16)
    c_all = kd_uw[:, :, :HEAD_V]
    o0_all = qk_uw[:, :, :HEAD_V]
    egl = jnp.exp(glast)

    states = [state_ref[h] for h in range(N_HEADS)]
    for n, (c, h) in enumerate(units):
        r = _dot(lhs_all[n], states[h].astype(BF16))
        o_s[c * C:(c + 1) * C, h * HEAD_V:(h + 1) * HEAD_V] = o0_all[n] + r[GDN_DK:]
        states[h] = states[h] * egl[n] + c_all[n] - r[:GDN_DK]
    for h in range(N_HEADS):
        state_ref[h] = states[h]

    for h in range(N_HEADS):
        lo = h * HEAD_V
        zh = z_ref[:, lo:lo + HEAD_V].astype(F32)
        o_ref[:, lo:lo + HEAD_V] = (_rms(o_s[:, lo:lo + HEAD_V]) * gn_ref[...] * _silu(zh)).astype(BF16)


def _gdn(qkv, ba, z, conv_w, a_log, dt_bias, gdn_norm, batch, seq):
    T = qkv.shape[0]
    ts = GDN_TS
    spt = seq // ts
    row = lambda w: pl.BlockSpec((ts, w), lambda bi, i: (bi * spt + i, 0))
    pad4 = lambda p: jnp.pad(p.astype(F32), (N_HEADS, LANES - 2 * N_HEADS)).reshape(1, LANES)
    return pl.pallas_call(
        _gdn_kernel,
        grid=(batch, spt),
        in_specs=[row(GDN_CONV_W), row(LANES), row(GDN_V_W),
                  _const_spec((GDN_CONV, GDN_CONV_W)), _const_spec((1, LANES)), _const_spec((1, LANES)),
                  _const_spec((1, HEAD_V))],
        out_specs=row(GDN_V_W),
        out_shape=jax.ShapeDtypeStruct((T, GDN_V_W), BF16),
        scratch_shapes=[
            pltpu.VMEM((8, GDN_CONV_W), F32),
            pltpu.VMEM((ts, GDN_V_W), F32),
            pltpu.VMEM((N_HEADS, GDN_DK, HEAD_V), F32),
        ],
        compiler_params=_params("arbitrary", "arbitrary"),
        name="gated_deltanet",
    )(qkv, ba, z, conv_w.astype(F32), pad4(a_log), pad4(dt_bias), gdn_norm.reshape(1, HEAD_V))


def _ret_kernel(q_ref, k_ref, v_ref, g_ref, o_ref, state_ref, dm_ref):
    i = pl.program_id(1)
    c = RET_C

    @pl.when(i == 0)
    def _():
        state_ref[...] = jnp.zeros_like(state_ref)

    @pl.when((pl.program_id(0) == 0) & (i == 0))
    def _():
        ri = lax.broadcasted_iota(jnp.int32, (c, c), 0)
        ci = lax.broadcasted_iota(jnp.int32, (c, c), 1)
        rel = (ri - ci).astype(F32)
        for h in range(N_HEADS):
            lg = math.log(1.0 - 2.0 ** (-5.0 - h))
            dm_ref[h] = jnp.where(rel >= 0, jnp.exp(jnp.where(rel >= 0, rel, 0.0) * lg), 0.0)

    pos = lax.broadcasted_iota(jnp.int32, (c, 1), 0).astype(F32)
    lane = lax.broadcasted_iota(jnp.int32, (1, LANES), 1)
    for t in range(q_ref.shape[0] // c):
        rows = slice(t * c, (t + 1) * c)
        for h in range(N_HEADS):
            lg = math.log(1.0 - 2.0 ** (-5.0 - h))
            grp = (h // 2) * LANES
            own = ((lane >> 5) & 1) == h % 2
            qm = jnp.where(own, q_ref[rows, grp:grp + LANES], 0)
            km = jnp.where(own, k_ref[rows, grp:grp + LANES], 0)
            v = v_ref[rows, h * HEAD_V:(h + 1) * HEAD_V]
            intra = _dot((_dot_nt(qm, km) * dm_ref[h]).astype(BF16), v)
            state = state_ref[h]
            inter = _dot(qm, state.astype(BF16)) * jnp.exp((pos + 1.0) * lg)
            k_sc = (km.astype(F32) * jnp.exp((c - 1.0 - pos) * lg)).astype(BF16)
            state_ref[h] = state * math.exp(c * lg) + _dot_tn(k_sc, v)
            gate = _silu(g_ref[rows, h * HEAD_V:(h + 1) * HEAD_V].astype(F32))
            o_ref[rows, h * HEAD_V:(h + 1) * HEAD_V] = (_rms(intra + inter) * gate).astype(BF16)


def _retention(rq, rk, rv, rg, batch, seq):
    T = rq.shape[0]
    c = RET_C
    tr = RET_TILE
    spt = seq // tr
    row = lambda w: pl.BlockSpec((tr, w), lambda bi, i: (bi * spt + i, 0))
    return pl.pallas_call(
        _ret_kernel,
        grid=(batch, spt),
        in_specs=[row(RET_QK_W), row(RET_QK_W), row(MIX_W), row(MIX_W)],
        out_specs=row(MIX_W),
        out_shape=jax.ShapeDtypeStruct((T, MIX_W), BF16),
        scratch_shapes=[pltpu.VMEM((N_HEADS, LANES, HEAD_V), F32), pltpu.VMEM((N_HEADS, c, c), F32)],
        compiler_params=_params("arbitrary", "arbitrary"),
        name="retention",
    )(rq, rk, rv, rg)


def _pack_w_in1(w_in):
    o = np.cumsum((0,) + (MIX_W,) * 6)
    cq, ck, cv, dq, dk, dv = (w_in[:, o[i]:o[i + 1]] for i in range(6))
    perm = _diff_perm()
    w = jnp.concatenate([cq, ck, cv, dq[:, perm], dk[:, perm]], axis=1).astype(BF16)
    return w, dv.T.astype(BF16)


def _in1_kernel(x_ref, n_ref, w_ref, wvt_ref, cosa_ref, sina_ref, cosb_ref, sinb_ref,
                cq_ref, ck_ref, cv_ref, dq_ref, dk_ref, dvt_ref):
    hb = (_rms(x_ref[...]) * n_ref[...]).astype(BF16)

    def proj(j):
        return _dot(hb, w_ref[:, j * MIX_W:(j + 1) * MIX_W])

    cq_ref[...] = _rope(proj(0), cosa_ref[...], sina_ref[...]) * (DIL_DH ** -0.5 * LOG2E)
    ck_ref[...] = _rope(proj(1), cosa_ref[...], sina_ref[...])
    cv_ref[...] = proj(2)
    dq_ref[...] = (_rope(proj(3), cosb_ref[...], sinb_ref[...]) * (DIFF_DK ** -0.5 * LOG2E)).astype(BF16)
    dk_ref[...] = _rope(proj(4), cosb_ref[...], sinb_ref[...]).astype(BF16)
    vt = _dot_nt(wvt_ref[...], hb).astype(BF16)
    for j in range(dvt_ref.shape[0]):
        dvt_ref[j] = vt[:, j * DIFF_TK:(j + 1) * DIFF_TK]


def _in1(x2d, norm, w_packed, wv_t, cosa, sina, cosb, sinb, seq):
    T = x2d.shape[0]
    tm = PROJ_TM
    spt = seq // tm
    vblk = tm // DIFF_TK
    row = lambda w: pl.BlockSpec((tm, w), lambda i: (i, 0))
    tab = pl.BlockSpec((tm, LANES), lambda i: (i % spt, 0))
    out_shape = [jax.ShapeDtypeStruct((T, MIX_W), dt) for dt in (F32, F32, F32, BF16, BF16)]
    out_shape.append(jax.ShapeDtypeStruct((T // DIFF_TK, MIX_W, DIFF_TK), BF16))
    return pl.pallas_call(
        _in1_kernel,
        grid=(T // tm,),
        in_specs=[row(D_MODEL), _const_spec((1, D_MODEL)), _const_spec((D_MODEL, 5 * MIX_W)),
                  _const_spec((MIX_W, D_MODEL)), tab, tab, tab, tab],
        out_specs=[row(MIX_W)] * 5 + [pl.BlockSpec((vblk, MIX_W, DIFF_TK), lambda i: (i, 0, 0))],
        out_shape=out_shape,
        compiler_params=_params("parallel"),
        name="in_proj1",
    )(x2d, norm.reshape(1, D_MODEL), w_packed, wv_t, cosa, sina, cosb, sinb)


def _dil_kernel(q_ref, kp_ref, kc_ref, vp_ref, vc_ref, o_ref, ob, mb, lb):
    i = pl.program_id(2)
    blk = q_ref.shape[0]
    nk = DIL_KEYS

    qi = lax.broadcasted_iota(jnp.int32, (nk, 2 * nk), 0)
    kj = lax.broadcasted_iota(jnp.int32, (nk, 2 * nk), 1)
    dist = qi + nk - kj
    band = (dist >= 0) & (dist <= nk)
    band_first = band & (kj >= jnp.where(i > 0, 0, nk))
    ones_v = jnp.ones((2 * nk, LANES), BF16)

    def window(prev_ref, cur_ref, start, d):
        if start >= blk:
            return cur_ref[pl.ds(start - blk, 2 * nk, stride=d), :]
        return jnp.concatenate([prev_ref[pl.ds(start, nk, stride=d), :],
                                cur_ref[pl.ds(start + nk * d - blk, nk, stride=d), :]], axis=0)

    for gi, (_, d) in enumerate(DIL_PATTERNS):
        span = nk * d
        pos = [((n // d) * span + n % d, n // d) for n in range(blk // nk)]
        qs_ = [q_ref[pl.ds(qs, nk, stride=d), :].astype(BF16) for qs, _ in pos]
        ks_ = [window(kp_ref, kc_ref, qs + blk - span, d).astype(BF16) for qs, _ in pos]
        vs_ = [jnp.concatenate([window(vp_ref, vc_ref, qs + blk - span, d).astype(BF16), ones_v], axis=1)
               for qs, _ in pos]
        ss = [_dot_nt(q, k) for q, k in zip(qs_, ks_)]
        ps, ms = [], []
        for s, (_, sub) in zip(ss, pos):
            sb = jnp.where(band if sub > 0 else band_first, s.astype(BF16), -jnp.inf)
            m = jnp.max(sb, axis=-1, keepdims=True)
            ps.append(jnp.exp2(sb - m))
            ms.append(m.astype(F32))
        os_ = [_dot(p, v) for p, v in zip(ps, vs_)]
        for (qs, _), o, m in zip(pos, os_, ms):
            ob[gi, pl.ds(qs, nk, stride=d), :] = o[:, :HEAD_V]
            lb[gi, pl.ds(qs, nk, stride=d), :] = o[:, HEAD_V:]
            mb[gi, pl.ds(qs, nk, stride=d), :] = jnp.broadcast_to(m, (nk, LANES))

    m_all = jnp.maximum(jnp.maximum(mb[0], mb[1]), mb[2])
    num = jnp.zeros((blk, HEAD_V), F32)
    den = jnp.zeros((blk, LANES), F32)
    for gi in range(len(DIL_PATTERNS)):
        sc = jnp.exp2(mb[gi] - m_all)
        num = num + ob[gi] * sc
        den = den + lb[gi] * sc
    o_ref[...] = (num / den).astype(BF16)


def _dilated(cq, ck, cv, batch, seq):
    blk = DIL_BLK
    q3, k3, v3 = (t.reshape(batch, seq, MIX_W) for t in (cq, ck, cv))
    cur = pl.BlockSpec((None, blk, DIL_DH), lambda b, h, i: (b, i, h))
    prev = pl.BlockSpec((None, blk, DIL_DH), lambda b, h, i: (b, jnp.maximum(i - 1, 0), h))
    nb = len(DIL_PATTERNS)
    out = pl.pallas_call(
        _dil_kernel,
        grid=(batch, N_HEADS, seq // blk),
        in_specs=[cur, prev, cur, prev, cur],
        out_specs=cur,
        out_shape=jax.ShapeDtypeStruct((batch, seq, MIX_W), BF16),
        scratch_shapes=[
            pltpu.VMEM((nb, blk, HEAD_V), F32),
            pltpu.VMEM((nb, blk, LANES), F32),
            pltpu.VMEM((nb, blk, LANES), F32),
        ],
        compiler_params=_params("parallel", "parallel", "arbitrary"),
        name="dilated_attention",
    )(q3, k3, k3, v3, v3)
    return out.reshape(batch * seq, MIX_W)


def _diff_kernel(q_ref, k_ref, vt_ref, lq1_ref, lk1_ref, lq2_ref, lk2_ref, dn_ref, o_ref,
                 *stats, lambda_init):
    i = pl.program_id(2)
    tq = q_ref.shape[0]
    tk = DIFF_TK
    assert tq == tk, "the diagonal handling assumes square tiles"
    gw = DIFF_GW
    ngrp = 2 * tq // gw
    nst = DIFF_HPS * ngrp
    m_s, acc_s = stats[0:nst], stats[nst:2 * nst]
    lane = lax.broadcasted_iota(jnp.int32, (1, LANES), 1)
    first = ((lane >> 5) & 1) == 0
    qgs = []
    for hh in range(DIFF_HPS):
        q = q_ref[:, hh * LANES:(hh + 1) * LANES]
        qq = jnp.concatenate([jnp.where(first, q, 0), jnp.where(first, 0, q)], axis=0)
        qgs.append([qq[g * gw:(g + 1) * gw] for g in range(ngrp)])
    for r in m_s:
        r[...] = jnp.full_like(r, -jnp.inf)
    for r in acc_s:
        r[...] = jnp.zeros_like(r)

    def scores(kt, c0, nr, hh, groups):
        k0 = pl.multiple_of(kt * tk, tk) + c0
        k_rows = k_ref[pl.ds(k0, nr), hh * LANES:(hh + 1) * LANES]
        return [_dot_nt(k_rows, qgs[hh][g]) for g in groups]

    def update(ss, kt, c0, nr, hh, groups, tri):
        vt = jnp.concatenate([vt_ref[kt, hh * HEAD_V:(hh + 1) * HEAD_V, c0:c0 + nr],
                              jnp.ones((8, nr), BF16)], axis=0)
        ps, alphas = [], []
        for g, s in zip(groups, ss):
            n = hh * ngrp + g
            sb = s.astype(BF16)
            if g in tri:
                key = lax.broadcasted_iota(jnp.int32, (nr, gw), 0)
                qry = lax.broadcasted_iota(jnp.int32, (nr, gw), 1)
                sb = jnp.where(qry >= key, sb, -jnp.inf)
            m_prev = m_s[n][...]
            m_new = jnp.maximum(m_prev, jnp.max(sb, axis=0, keepdims=True).astype(F32))
            alphas.append(jnp.exp2(m_prev - m_new))
            ps.append(jnp.exp2(sb - m_new.astype(BF16)))
            m_s[n][...] = m_new
        pvs = [_dot(vt, p) for p in ps]
        for g, alpha, pv in zip(groups, alphas, pvs):
            n = hh * ngrp + g
            acc_s[n][...] = alpha * acc_s[n][...] + pv

    def run(kt, units):
        sss = [scores(kt, c0, nr, hh, groups) for c0, nr, hh, groups, _ in units]
        for ss, (c0, nr, hh, groups, tri) in zip(sss, units):
            update(ss, kt, c0, nr, hh, groups, tri)

    every = list(range(ngrp))
    full = [(0, tk, hh, every, ()) for hh in range(DIFF_HPS)]

    def body(kt, carry):
        run(kt, full)
        return carry

    lax.fori_loop(0, i, body, 0)

    diagonal = []
    for sub in range(tk // gw):
        keep = [g for g in every if (g * gw) % tq >= sub * gw]
        on_diag = tuple(g for g in keep if (g * gw) % tq == sub * gw)
        diagonal += [(sub * gw, gw, hh, keep, on_diag) for hh in range(DIFF_HPS)]
    run(i, diagonal)

    lam = (jnp.exp(jnp.sum(lq1_ref[...] * lk1_ref[...], axis=-1, keepdims=True))
           - jnp.exp(jnp.sum(lq2_ref[...] * lk2_ref[...], axis=-1, keepdims=True)) + lambda_init)
    half = ngrp // 2
    for hh in range(DIFF_HPS):
        for g in range(half):
            n1, n2 = hh * ngrp + g, hh * ngrp + g + half
            a1, a2 = acc_s[n1][...], acc_s[n2][...]
            o = a1[:HEAD_V] / a1[HEAD_V:HEAD_V + 1] - lam * (a2[:HEAD_V] / a2[HEAD_V:HEAD_V + 1])
            o = o * lax.rsqrt(jnp.mean(o * o, axis=0, keepdims=True) + EPS) * dn_ref[...] * (1.0 - lambda_init)
            o_ref[g * gw:(g + 1) * gw, hh * HEAD_V:(hh + 1) * HEAD_V] = o.T.astype(BF16)


def _differential(dq, dk, dvt, lq1, lk1, lq2, lk2, diff_norm, lambda_init, batch, seq):
    tq = DIFF_TQ
    nkt = seq // DIFF_TK
    nst = DIFF_HPS * 2 * tq // DIFF_GW
    hw = DIFF_HPS * LANES
    q3, k3 = (t.reshape(batch, seq, MIX_W) for t in (dq, dk))
    qspec = pl.BlockSpec((None, tq, hw), lambda b, h, i: (b, i, h))
    kspec = pl.BlockSpec((None, seq, hw), lambda b, h, i: (b, 0, h))
    vspec = pl.BlockSpec((nkt, hw, DIFF_TK), lambda b, h, i: (b, h, 0))
    vec = lambda p: p.reshape(1, DIFF_DK).astype(F32)
    out = pl.pallas_call(
        functools.partial(_diff_kernel, lambda_init=lambda_init),
        grid=(batch, N_HEADS // DIFF_HPS, seq // tq),
        in_specs=[qspec, kspec, vspec] + [_const_spec((1, DIFF_DK))] * 4 + [_const_spec((HEAD_V, 1))],
        out_specs=qspec,
        out_shape=jax.ShapeDtypeStruct((batch, seq, MIX_W), BF16),
        scratch_shapes=([pltpu.VMEM((1, DIFF_GW), F32)] * nst
                        + [pltpu.VMEM((HEAD_V + 8, DIFF_GW), F32)] * nst),
        compiler_params=_params("parallel", "parallel", "arbitrary"),
        name="differential_attention",
    )(q3, k3, dvt, vec(lq1), vec(lk1), vec(lq2), vec(lk2), diff_norm.reshape(HEAD_V, 1))
    return out.reshape(batch * seq, MIX_W)


def kernel(x, l0_ffn1_norm, l0_ffn1_w_up, l0_ffn1_w_down, l0_mix_norm, l0_w_in, l0_conv_w, l0_a_log,
           l0_dt_bias, l0_gdn_norm, l0_w_out, l0_ffn2_norm, l0_ffn2_w_up, l0_ffn2_w_down,
           l1_ffn1_norm, l1_ffn1_w_up, l1_ffn1_w_down, l1_mix_norm, l1_w_in, l1_lambda_q1, l1_lambda_k1,
           l1_lambda_q2, l1_lambda_k2, l1_diff_norm, l1_w_out, l1_ffn2_norm, l1_ffn2_w_up, l1_ffn2_w_down,
           final_norm):
    batch, seq, _ = x.shape
    cos64, sin64 = _rope_tables(seq, 64)
    cos128, sin128 = _rope_tables(seq, 128)
    xf = x.reshape(batch * seq, D_MODEL)

    xf = _ffn(xf, l0_ffn1_norm, l0_ffn1_w_up, l0_ffn1_w_down)
    qkv, z, rq, rk, rv, rg, ba = _in0(xf, l0_mix_norm, _pack_w_in0(l0_w_in), cos64, sin64, seq)
    o_a = _gdn(qkv, ba, z, l0_conv_w, l0_a_log, l0_dt_bias, l0_gdn_norm, batch, seq)
    o_b = _retention(rq, rk, rv, rg, batch, seq)
    xf = _ffn(xf, l0_ffn2_norm, l0_ffn2_w_up, l0_ffn2_w_down, mix=(o_a, o_b, l0_w_out))

    xf = _ffn(xf, l1_ffn1_norm, l1_ffn1_w_up, l1_ffn1_w_down)
    w1, wv_t = _pack_w_in1(l1_w_in)
    cq, ck, cv, dq, dk, dvt = _in1(xf, l1_mix_norm, w1, wv_t, cos128, sin128, cos64, sin64, seq)
    o_c = _dilated(cq, ck, cv, batch, seq)
    lambda_init = 0.8 - 0.6 * math.exp(-0.3 * 1)
    o_d = _differential(dq, dk, dvt, l1_lambda_q1, l1_lambda_k1, l1_lambda_q2, l1_lambda_k2,
                        l1_diff_norm, lambda_init, batch, seq)
    xf = _ffn(xf, l1_ffn2_norm, l1_ffn2_w_up, l1_ffn2_w_down, mix=(o_c, o_d, l1_w_out), final_norm=final_norm)
    return xf.reshape(batch, seq, D_MODEL)
```

```python
import functools
import math

import numpy as np
import jax
import jax.numpy as jnp
from jax import lax
from jax.experimental import pallas as pl
from jax.experimental.pallas import tpu as pltpu

F32 = jnp.float32
BF16 = jnp.bfloat16
HIGHEST = lax.Precision.HIGHEST

EPS = 1e-6
ROPE_THETA = 10000.0
D_MODEL = 1024
D_FF = 2816
N_HEADS = 4
HEAD_V = 128
GDN_DK = 128
GDN_CONV = 4
GDN_CHUNK = 64
GDN_QK_W = N_HEADS * GDN_DK
GDN_V_W = N_HEADS * HEAD_V
GDN_CONV_W = 2 * GDN_QK_W + GDN_V_W
RET_DK = 64
RET_QK_W = N_HEADS * RET_DK
DIL_DH = 128
DIL_PATTERNS = ((128, 1), (512, 4), (2048, 16))
DIL_KEYS = 128
DIFF_DK = 64
MIX_W = N_HEADS * HEAD_V

LANES = 128
VMEM_LIMIT = 56 * 1024 * 1024

FFN_TM = 1024
FFN_TF = 256
PROJ_TM = 1024
GDN_TS = 256
GDN_TILE = 512
RET_C = 256
RET_TILE = 512
DIL_BLK = 2048
DIFF_TQ = 512
DIFF_TK = 512
DIFF_HPS = 4
DIFF_GW = 256
LOG2E = math.log2(math.e)


def _rms(x):
    return x * lax.rsqrt(jnp.mean(x * x, axis=-1, keepdims=True) + EPS)


def _silu(x):
    return x * (1.0 / (1.0 + jnp.exp(-x)))


def _dot(a, b):
    return jnp.dot(a, b, preferred_element_type=F32)


def _dot_nt(a, b):
    return lax.dot_general(a, b, (((1,), (1,)), ((), ())), preferred_element_type=F32)


def _dot_tn(a, b):
    return lax.dot_general(a, b, (((0,), (0,)), ((), ())), preferred_element_type=F32)


def _params(*sem):
    return pltpu.CompilerParams(dimension_semantics=sem, vmem_limit_bytes=VMEM_LIMIT)


def _const_spec(shape):
    nd = len(shape)
    return pl.BlockSpec(shape, lambda *_: (0,) * nd, pipeline_mode=pl.Buffered(1))


def _ffn_kernel(*refs, mixed, final):
    refs = list(refs)
    x_ref = refs.pop(0)
    x = x_ref[...]
    if mixed:
        oa_ref, ob_ref, wo_ref = refs.pop(0), refs.pop(0), refs.pop(0)
        x = x + _dot(oa_ref[...], wo_ref[0:MIX_W, :]) + _dot(ob_ref[...], wo_ref[MIX_W:2 * MIX_W, :])
    n_ref, wup_ref, wd_ref = refs.pop(0), refs.pop(0), refs.pop(0)
    fn_ref = refs.pop(0) if final else None
    o_ref, acc_ref = refs
    hb = (_rms(x) * n_ref[...]).astype(BF16)
    for c in range(D_FF // FFN_TF):
        lo = c * FFN_TF
        g = _dot(hb, wup_ref[:, lo:lo + FFN_TF])
        u = _dot(hb, wup_ref[:, D_FF + lo:D_FF + lo + FFN_TF])
        a = (_silu(g) * u).astype(BF16)
        d = _dot(a, wd_ref[lo:lo + FFN_TF, :])
        if c == 0:
            acc_ref[...] = d
        else:
            acc_ref[...] += d
    y = x + 0.5 * acc_ref[...]
    if final:
        y = _rms(y) * fn_ref[...]
    o_ref[...] = y


def _ffn(x2d, norm, w_up, w_down, mix=None, final_norm=None):
    T = x2d.shape[0]
    row = lambda w: pl.BlockSpec((FFN_TM, w), lambda i: (i, 0))
    args, specs = [x2d], [row(D_MODEL)]
    if mix is not None:
        o_a, o_b, w_out = mix
        args += [o_a, o_b, w_out.astype(BF16)]
        specs += [row(MIX_W), row(MIX_W), _const_spec((2 * MIX_W, D_MODEL))]
    args += [norm.reshape(1, D_MODEL), w_up.astype(BF16), w_down.astype(BF16)]
    specs += [_const_spec((1, D_MODEL)), _const_spec((D_MODEL, 2 * D_FF)), _const_spec((D_FF, D_MODEL))]
    if final_norm is not None:
        args.append(final_norm.reshape(1, D_MODEL))
        specs.append(_const_spec((1, D_MODEL)))
    return pl.pallas_call(
        functools.partial(_ffn_kernel, mixed=mix is not None, final=final_norm is not None),
        grid=(T // FFN_TM,),
        in_specs=specs,
        out_specs=row(D_MODEL),
        out_shape=jax.ShapeDtypeStruct((T, D_MODEL), F32),
        scratch_shapes=[pltpu.VMEM((FFN_TM, D_MODEL), F32)],
        compiler_params=_params("parallel"),
        name="ffn_final" if final_norm is not None else ("ffn_mix" if mix is not None else "ffn"),
    )(*args)


def _rope_tables(seq, dim):
    half = dim // 2
    inv = ROPE_THETA ** (-jnp.arange(0, dim, 2, dtype=F32) / dim)
    ang = jnp.arange(seq, dtype=F32)[:, None] * inv[None, :]
    reps = (LANES // 2) // half
    cos = jnp.tile(jnp.cos(ang), (1, 2 * reps))
    sin = jnp.tile(jnp.sin(ang), (1, reps))
    return cos, jnp.concatenate([-sin, sin], axis=1)


def _rope(x, cos, sin):
    outs = []
    for g in range(x.shape[1] // LANES):
        xg = x[:, g * LANES:(g + 1) * LANES]
        outs.append(xg * cos + pltpu.roll(xg, LANES // 2, 1) * sin)
    return outs[0] if len(outs) == 1 else jnp.concatenate(outs, axis=1)


_P0_QKV = (0, GDN_CONV_W)
_P0_Z = (_P0_QKV[1], _P0_QKV[1] + GDN_V_W)
_P0_RQ = (_P0_Z[1], _P0_Z[1] + RET_QK_W)
_P0_RK = (_P0_RQ[1], _P0_RQ[1] + RET_QK_W)
_P0_RV = (_P0_RK[1], _P0_RK[1] + MIX_W)
_P0_RG = (_P0_RV[1], _P0_RV[1] + MIX_W)
_P0_BA = (_P0_RG[1], _P0_RG[1] + LANES)
_P0_W = _P0_BA[1]


def _ret_perm():
    perm = np.zeros(RET_QK_W, np.int32)
    for h in range(N_HEADS):
        for d in range(RET_DK):
            new = (h // 2) * LANES + (d // 32) * 64 + (h % 2) * 32 + d % 32
            perm[new] = h * RET_DK + d
    return perm


def _diff_perm():
    perm = np.zeros(N_HEADS * 2 * DIFF_DK, np.int32)
    for h in range(N_HEADS):
        for c in range(2):
            for d in range(DIFF_DK):
                new = h * LANES + (d // 32) * 64 + c * 32 + d % 32
                perm[new] = h * 2 * DIFF_DK + c * DIFF_DK + d
    return perm


def _pack_w_in0(w_in):
    o = np.cumsum((0, GDN_CONV_W, GDN_V_W, N_HEADS, N_HEADS, RET_QK_W, RET_QK_W, MIX_W, MIX_W))
    qkv, z, b, a, rq, rk, rv, rg = (w_in[:, o[i]:o[i + 1]] for i in range(8))
    perm = _ret_perm()
    pad = jnp.zeros((D_MODEL, LANES - 2 * N_HEADS), w_in.dtype)
    cols = [qkv, z, rq[:, perm], rk[:, perm], rv, rg, b, a, pad]
    return jnp.concatenate(cols, axis=1).astype(BF16)


def _in0_kernel(x_ref, n_ref, w_ref, cos_ref, sin_ref,
                qkv_ref, z_ref, rq_ref, rk_ref, rv_ref, rg_ref, ba_ref):
    hb = (_rms(x_ref[...]) * n_ref[...]).astype(BF16)

    def proj(seg):
        return _dot(hb, w_ref[:, seg[0]:seg[1]])

    cos = cos_ref[...]
    sin = sin_ref[...]
    qkv_ref[...] = proj(_P0_QKV)
    z_ref[...] = proj(_P0_Z).astype(BF16)
    rq_ref[...] = (_rope(proj(_P0_RQ), cos, sin) * RET_DK ** -0.5).astype(BF16)
    rk_ref[...] = _rope(proj(_P0_RK), cos, sin).astype(BF16)
    rv_ref[...] = proj(_P0_RV).astype(BF16)
    rg_ref[...] = proj(_P0_RG).astype(BF16)
    ba_ref[...] = proj(_P0_BA)


def _in0(x2d, norm, w_packed, cos, sin, seq):
    T = x2d.shape[0]
    tm = PROJ_TM
    spt = seq // tm
    row = lambda w: pl.BlockSpec((tm, w), lambda i: (i, 0))
    tab = pl.BlockSpec((tm, LANES), lambda i: (i % spt, 0))
    widths = (GDN_CONV_W, GDN_V_W, RET_QK_W, RET_QK_W, MIX_W, MIX_W, LANES)
    dtypes = (F32, BF16, BF16, BF16, BF16, BF16, F32)
    return pl.pallas_call(
        _in0_kernel,
        grid=(T // tm,),
        in_specs=[row(D_MODEL), _const_spec((1, D_MODEL)), _const_spec((D_MODEL, _P0_W)), tab, tab],
        out_specs=[row(w) for w in widths],
        out_shape=[jax.ShapeDtypeStruct((T, w), dt) for w, dt in zip(widths, dtypes)],
        compiler_params=_params("parallel"),
        name="in_proj0",
    )(x2d, norm.reshape(1, D_MODEL), w_packed, cos, sin)


def _bmm(a, b):
    return jnp.einsum("gik,gkj->gij", a.astype(BF16), b.astype(BF16), preferred_element_type=F32)


def _unit_lower_inverse(a, ri, ci):
    n = jnp.where((ri >> 3) == (ci >> 3), -a, 0.0)
    t = jnp.where(ri == ci, 1.0, 0.0) + n
    p = _bmm(n, n)
    t = t + _bmm(t, p)
    p = _bmm(p, p)
    t = t + _bmm(t, p)
    for sh in (3, 4, 5):
        e = jnp.where(((ri >> (sh + 1)) == (ci >> (sh + 1))) & ((ri >> sh) != (ci >> sh)), a, 0.0)
        t = t - _bmm(_bmm(t, e), t)
    return t


def _gdn_kernel(qkv_ref, ba_ref, z_ref, cw_ref, alog_ref, dtb_ref, gn_ref, o_ref,
                xe_ref, o_s, state_ref):
    @pl.when(pl.program_id(1) == 0)
    def _():
        xe_ref[...] = jnp.zeros_like(xe_ref)
        state_ref[...] = jnp.zeros_like(state_ref)

    for t in range(qkv_ref.shape[0] // GDN_TS):
        rows = slice(t * GDN_TS, (t + 1) * GDN_TS)
        _gdn_tile(qkv_ref.at[rows], ba_ref.at[rows], z_ref.at[rows], cw_ref, alog_ref, dtb_ref, gn_ref,
                  o_ref.at[rows], xe_ref, o_s, state_ref)


def _gdn_tile(qkv_ref, ba_ref, z_ref, cw_ref, alog_ref, dtb_ref, gn_ref, o_ref, xe_ref, o_s, state_ref):
    ts = qkv_ref.shape[0]
    C = GDN_CHUNK
    nch = ts // C
    halo = 8

    beta = 1.0 / (1.0 + jnp.exp(-ba_ref[...]))
    xa = ba_ref[...] + dtb_ref[...]
    softplus = jnp.maximum(xa, 0.0) + jnp.log(1.0 + jnp.exp(-jnp.abs(xa)))
    g = -jnp.exp(alog_ref[...]) * softplus
    rt = lax.broadcasted_iota(jnp.int32, (ts, ts), 0)
    ct = lax.broadcasted_iota(jnp.int32, (ts, ts), 1)
    tri = jnp.where((rt >= ct) & ((rt >> 6) == (ct >> 6)), 1.0, 0.0)
    gc = jnp.dot(tri, g, precision=HIGHEST, preferred_element_type=F32)

    x = qkv_ref[...]
    xb = x.astype(BF16)
    shift = jnp.concatenate([jnp.where(rt - ct == k, 1.0, 0.0) for k in range(1, GDN_CONV)], axis=0).astype(BF16)
    shifted = _dot(shift, xb)
    prev = xe_ref[...]
    row8 = lax.broadcasted_iota(jnp.int32, (halo, 1), 0)
    conv = cw_ref[GDN_CONV - 1:GDN_CONV, :] * x
    for k in range(1, GDN_CONV):
        sh = shifted[(k - 1) * ts:k * ts]
        head = jnp.where(row8 < k, pltpu.roll(prev, k, 0), sh[0:halo])
        conv = conv + cw_ref[GDN_CONV - 1 - k:GDN_CONV - k, :] * jnp.concatenate([head, sh[halo:]], axis=0)
    xe_ref[...] = x[ts - halo:ts]
    act = _silu(conv)
    qn, kn, vv = [], [], []
    for h in range(N_HEADS):
        lo = h * GDN_DK
        qh = act[:, lo:lo + GDN_DK]
        kh = act[:, GDN_QK_W + lo:GDN_QK_W + lo + GDN_DK]
        qn.append(qh * (lax.rsqrt(jnp.sum(qh * qh, -1, keepdims=True) + EPS) * GDN_DK ** -0.5))
        kn.append(kh * lax.rsqrt(jnp.sum(kh * kh, -1, keepdims=True) + EPS))
        vv.append(act[:, 2 * GDN_QK_W + h * HEAD_V:2 * GDN_QK_W + (h + 1) * HEAD_V])

    units = [(c, h) for c in range(nch) for h in range(N_HEADS)]
    rows = lambda x, c: x[c * C:(c + 1) * C]
    stack = lambda f: jnp.stack([f(c, h) for c, h in units], axis=0)
    q = stack(lambda c, h: rows(qn[h], c))
    k = stack(lambda c, h: rows(kn[h], c))
    v = stack(lambda c, h: rows(vv[h], c))
    gct = [rows(gc, c).T for c in range(nch)]
    gcol = stack(lambda c, h: rows(gc, c)[:, N_HEADS + h:N_HEADS + h + 1])
    grow = stack(lambda c, h: gct[c][N_HEADS + h:N_HEADS + h + 1, :])
    bcol = stack(lambda c, h: rows(beta, c)[:, h:h + 1])
    glast = gcol[:, C - 1:C, :]

    ri = lax.broadcasted_iota(jnp.int32, (1, C, C), 1)
    ci = lax.broadcasted_iota(jnp.int32, (1, C, C), 2)
    causal = ri >= ci
    decay = jnp.where(causal, jnp.exp(jnp.where(causal, gcol - grow, 0.0)), 0.0)
    kb = k.astype(BF16)
    kk = jnp.einsum("gid,gjd->gij", kb, kb, preferred_element_type=F32)
    a_mat = jnp.where(ri > ci, kk * decay * bcol, 0.0)
    t_inv = _unit_lower_inverse(a_mat, ri, ci)
    eg = jnp.exp(gcol)
    rhs = jnp.concatenate([v * bcol, k * (bcol * eg)], axis=2).astype(BF16)
    sol = jnp.einsum("gik,gkj->gij", t_inv.astype(BF16), rhs, preferred_element_type=F32)
    solb = sol.astype(BF16)
    qk = (jnp.einsum("gid,gjd->gij", q.astype(BF16), kb, preferred_element_type=F32) * decay).astype(BF16)
    qk_uw = jnp.einsum("gij,gje->gie", qk, solb, preferred_element_type=F32)
    kd = k * jnp.exp(glast - gcol)
    kdt = jnp.stack([kd[n].T for n in range(len(units))], axis=0).astype(BF16)
    kd_uw = jnp.einsum("gdc,gce->gde", kdt, solb, preferred_element_type=F32)
    lhs_all = jnp.concatenate([kd_uw[:, :, HEAD_V:], q * eg - qk_uw[:, :, HEAD_V:]], axis=1).astype(BF16)
    c_all = kd_uw[:, :, :HEAD_V]
    o0_all = qk_uw[:, :, :HEAD_V]
    egl = jnp.exp(glast)

    states = [state_ref[h] for h in range(N_HEADS)]
    for n, (c, h) in enumerate(units):
        r = _dot(lhs_all[n], states[h].astype(BF16))
        o_s[c * C:(c + 1) * C, h * HEAD_V:(h + 1) * HEAD_V] = o0_all[n] + r[GDN_DK:]
        states[h] = states[h] * egl[n] + c_all[n] - r[:GDN_DK]
    for h in range(N_HEADS):
        state_ref[h] = states[h]

    for h in range(N_HEADS):
        lo = h * HEAD_V
        zh = z_ref[:, lo:lo + HEAD_V].astype(F32)
        o_ref[:, lo:lo + HEAD_V] = (_rms(o_s[:, lo:lo + HEAD_V]) * gn_ref[...] * _silu(zh)).astype(BF16)


def _gdn(qkv, ba, z, conv_w, a_log, dt_bias, gdn_norm, batch, seq):
    T = qkv.shape[0]
    ts = GDN_TS
    spt = seq // GDN_TILE
    row = lambda w: pl.BlockSpec((GDN_TILE, w), lambda bi, i: (bi * spt + i, 0))
    pad4 = lambda p: jnp.pad(p.astype(F32), (N_HEADS, LANES - 2 * N_HEADS)).reshape(1, LANES)
    return pl.pallas_call(
        _gdn_kernel,
        grid=(batch, spt),
        in_specs=[row(GDN_CONV_W), row(LANES), row(GDN_V_W),
                  _const_spec((GDN_CONV, GDN_CONV_W)), _const_spec((1, LANES)), _const_spec((1, LANES)),
                  _const_spec((1, HEAD_V))],
        out_specs=row(GDN_V_W),
        out_shape=jax.ShapeDtypeStruct((T, GDN_V_W), BF16),
        scratch_shapes=[
            pltpu.VMEM((8, GDN_CONV_W), F32),
            pltpu.VMEM((ts, GDN_V_W), F32),
            pltpu.VMEM((N_HEADS, GDN_DK, HEAD_V), F32),
        ],
        compiler_params=_params("arbitrary", "arbitrary"),
        name="gated_deltanet",
    )(qkv, ba, z, conv_w.astype(F32), pad4(a_log), pad4(dt_bias), gdn_norm.reshape(1, HEAD_V))


def _ret_kernel(q_ref, k_ref, v_ref, g_ref, o_ref, state_ref, dm_ref):
    i = pl.program_id(1)
    c = RET_C

    @pl.when(i == 0)
    def _():
        state_ref[...] = jnp.zeros_like(state_ref)

    @pl.when((pl.program_id(0) == 0) & (i == 0))
    def _():
        ri = lax.broadcasted_iota(jnp.int32, (c, c), 0)
        ci = lax.broadcasted_iota(jnp.int32, (c, c), 1)
        rel = (ri - ci).astype(F32)
        for h in range(N_HEADS):
            lg = math.log(1.0 - 2.0 ** (-5.0 - h))
            dm_ref[h] = jnp.where(rel >= 0, jnp.exp(jnp.where(rel >= 0, rel, 0.0) * lg), 0.0)

    pos = lax.broadcasted_iota(jnp.int32, (c, 1), 0).astype(F32)
    lane = lax.broadcasted_iota(jnp.int32, (1, LANES), 1)
    for t in range(q_ref.shape[0] // c):
        rows = slice(t * c, (t + 1) * c)
        for h in range(N_HEADS):
            lg = math.log(1.0 - 2.0 ** (-5.0 - h))
            grp = (h // 2) * LANES
            own = ((lane >> 5) & 1) == h % 2
            qm = jnp.where(own, q_ref[rows, grp:grp + LANES], 0)
            km = jnp.where(own, k_ref[rows, grp:grp + LANES], 0)
            v = v_ref[rows, h * HEAD_V:(h + 1) * HEAD_V]
            intra = _dot((_dot_nt(qm, km) * dm_ref[h]).astype(BF16), v)
            state = state_ref[h]
            inter = _dot(qm, state.astype(BF16)) * jnp.exp((pos + 1.0) * lg)
            k_sc = (km.astype(F32) * jnp.exp((c - 1.0 - pos) * lg)).astype(BF16)
            state_ref[h] = state * math.exp(c * lg) + _dot_tn(k_sc, v)
            gate = _silu(g_ref[rows, h * HEAD_V:(h + 1) * HEAD_V].astype(F32))
            o_ref[rows, h * HEAD_V:(h + 1) * HEAD_V] = (_rms(intra + inter) * gate).astype(BF16)


def _retention(rq, rk, rv, rg, batch, seq):
    T = rq.shape[0]
    c = RET_C
    tr = RET_TILE
    spt = seq // tr
    row = lambda w: pl.BlockSpec((tr, w), lambda bi, i: (bi * spt + i, 0))
    return pl.pallas_call(
        _ret_kernel,
        grid=(batch, spt),
        in_specs=[row(RET_QK_W), row(RET_QK_W), row(MIX_W), row(MIX_W)],
        out_specs=row(MIX_W),
        out_shape=jax.ShapeDtypeStruct((T, MIX_W), BF16),
        scratch_shapes=[pltpu.VMEM((N_HEADS, LANES, HEAD_V), F32), pltpu.VMEM((N_HEADS, c, c), F32)],
        compiler_params=_params("arbitrary", "arbitrary"),
        name="retention",
    )(rq, rk, rv, rg)


def _pack_w_in1(w_in):
    o = np.cumsum((0,) + (MIX_W,) * 6)
    cq, ck, cv, dq, dk, dv = (w_in[:, o[i]:o[i + 1]] for i in range(6))
    perm = _diff_perm()
    w = jnp.concatenate([cq, ck, cv, dq[:, perm], dk[:, perm]], axis=1).astype(BF16)
    return w, dv.T.astype(BF16)


def _in1_kernel(x_ref, n_ref, w_ref, wvt_ref, cosa_ref, sina_ref, cosb_ref, sinb_ref,
                cq_ref, ck_ref, cv_ref, dq_ref, dk_ref, dvt_ref):
    hb = (_rms(x_ref[...]) * n_ref[...]).astype(BF16)

    def proj(j):
        return _dot(hb, w_ref[:, j * MIX_W:(j + 1) * MIX_W])

    cq_ref[...] = _rope(proj(0), cosa_ref[...], sina_ref[...]) * (DIL_DH ** -0.5 * LOG2E)
    ck_ref[...] = _rope(proj(1), cosa_ref[...], sina_ref[...])
    cv_ref[...] = proj(2)
    dq_ref[...] = (_rope(proj(3), cosb_ref[...], sinb_ref[...]) * (DIFF_DK ** -0.5 * LOG2E)).astype(BF16)
    dk_ref[...] = _rope(proj(4), cosb_ref[...], sinb_ref[...]).astype(BF16)
    vt = _dot_nt(wvt_ref[...], hb).astype(BF16)
    for j in range(dvt_ref.shape[0]):
        dvt_ref[j] = vt[:, j * DIFF_TK:(j + 1) * DIFF_TK]


def _in1(x2d, norm, w_packed, wv_t, cosa, sina, cosb, sinb, seq):
    T = x2d.shape[0]
    tm = PROJ_TM
    spt = seq // tm
    vblk = tm // DIFF_TK
    row = lambda w: pl.BlockSpec((tm, w), lambda i: (i, 0))
    tab = pl.BlockSpec((tm, LANES), lambda i: (i % spt, 0))
    out_shape = [jax.ShapeDtypeStruct((T, MIX_W), dt) for dt in (F32, F32, F32, BF16, BF16)]
    out_shape.append(jax.ShapeDtypeStruct((T // DIFF_TK, MIX_W, DIFF_TK), BF16))
    return pl.pallas_call(
        _in1_kernel,
        grid=(T // tm,),
        in_specs=[row(D_MODEL), _const_spec((1, D_MODEL)), _const_spec((D_MODEL, 5 * MIX_W)),
                  _const_spec((MIX_W, D_MODEL)), tab, tab, tab, tab],
        out_specs=[row(MIX_W)] * 5 + [pl.BlockSpec((vblk, MIX_W, DIFF_TK), lambda i: (i, 0, 0))],
        out_shape=out_shape,
        compiler_params=_params("parallel"),
        name="in_proj1",
    )(x2d, norm.reshape(1, D_MODEL), w_packed, wv_t, cosa, sina, cosb, sinb)


def _dil_kernel(q_ref, kp_ref, kc_ref, vp_ref, vc_ref, o_ref, ob, mb, lb):
    i = pl.program_id(2)
    blk = q_ref.shape[0]
    nk = DIL_KEYS

    qi = lax.broadcasted_iota(jnp.int32, (nk, 2 * nk), 0)
    kj = lax.broadcasted_iota(jnp.int32, (nk, 2 * nk), 1)
    dist = qi + nk - kj
    band = (dist >= 0) & (dist <= nk)
    band_first = band & (kj >= jnp.where(i > 0, 0, nk))
    ones_v = jnp.ones((2 * nk, LANES), BF16)

    def window(prev_ref, cur_ref, start, d):
        if start >= blk:
            return cur_ref[pl.ds(start - blk, 2 * nk, stride=d), :]
        return jnp.concatenate([prev_ref[pl.ds(start, nk, stride=d), :],
                                cur_ref[pl.ds(start + nk * d - blk, nk, stride=d), :]], axis=0)

    for gi, (_, d) in enumerate(DIL_PATTERNS):
        span = nk * d
        pos = [((n // d) * span + n % d, n // d) for n in range(blk // nk)]
        qs_ = [q_ref[pl.ds(qs, nk, stride=d), :].astype(BF16) for qs, _ in pos]
        ks_ = [window(kp_ref, kc_ref, qs + blk - span, d).astype(BF16) for qs, _ in pos]
        vs_ = [jnp.concatenate([window(vp_ref, vc_ref, qs + blk - span, d).astype(BF16), ones_v], axis=1)
               for qs, _ in pos]
        ss = [_dot_nt(q, k) for q, k in zip(qs_, ks_)]
        ps, ms = [], []
        for s, (_, sub) in zip(ss, pos):
            sb = jnp.where(band if sub > 0 else band_first, s.astype(BF16), -jnp.inf)
            m = jnp.max(sb, axis=-1, keepdims=True)
            ps.append(jnp.exp2(sb - m))
            ms.append(m.astype(F32))
        os_ = [_dot(p, v) for p, v in zip(ps, vs_)]
        for (qs, _), o, m in zip(pos, os_, ms):
            ob[gi, pl.ds(qs, nk, stride=d), :] = o[:, :HEAD_V]
            lb[gi, pl.ds(qs, nk, stride=d), :] = o[:, HEAD_V:]
            mb[gi, pl.ds(qs, nk, stride=d), :] = jnp.broadcast_to(m, (nk, LANES))

    m_all = jnp.maximum(jnp.maximum(mb[0], mb[1]), mb[2])
    num = jnp.zeros((blk, HEAD_V), F32)
    den = jnp.zeros((blk, LANES), F32)
    for gi in range(len(DIL_PATTERNS)):
        sc = jnp.exp2(mb[gi] - m_all)
        num = num + ob[gi] * sc
        den = den + lb[gi] * sc
    o_ref[...] = (num / den).astype(BF16)


def _dilated(cq, ck, cv, batch, seq):
    blk = DIL_BLK
    q3, k3, v3 = (t.reshape(batch, seq, MIX_W) for t in (cq, ck, cv))
    cur = pl.BlockSpec((None, blk, DIL_DH), lambda b, h, i: (b, i, h))
    prev = pl.BlockSpec((None, blk, DIL_DH), lambda b, h, i: (b, jnp.maximum(i - 1, 0), h))
    nb = len(DIL_PATTERNS)
    out = pl.pallas_call(
        _dil_kernel,
        grid=(batch, N_HEADS, seq // blk),
        in_specs=[cur, prev, cur, prev, cur],
        out_specs=cur,
        out_shape=jax.ShapeDtypeStruct((batch, seq, MIX_W), BF16),
        scratch_shapes=[
            pltpu.VMEM((nb, blk, HEAD_V), F32),
            pltpu.VMEM((nb, blk, LANES), F32),
            pltpu.VMEM((nb, blk, LANES), F32),
        ],
        compiler_params=_params("parallel", "parallel", "arbitrary"),
        name="dilated_attention",
    )(q3, k3, k3, v3, v3)
    return out.reshape(batch * seq, MIX_W)


def _diff_kernel(q_ref, k_ref, vt_ref, lq1_ref, lk1_ref, lq2_ref, lk2_ref, dn_ref, o_ref,
                 *stats, lambda_init):
    i = pl.program_id(2)
    tq = q_ref.shape[0]
    tk = DIFF_TK
    assert tq == tk, "the diagonal handling assumes square tiles"
    gw = DIFF_GW
    ngrp = 2 * tq // gw
    nst = DIFF_HPS * ngrp
    m_s, acc_s = stats[0:nst], stats[nst:2 * nst]
    lane = lax.broadcasted_iota(jnp.int32, (1, LANES), 1)
    first = ((lane >> 5) & 1) == 0
    qgs = []
    for hh in range(DIFF_HPS):
        q = q_ref[:, hh * LANES:(hh + 1) * LANES]
        qq = jnp.concatenate([jnp.where(first, q, 0), jnp.where(first, 0, q)], axis=0)
        qgs.append([qq[g * gw:(g + 1) * gw] for g in range(ngrp)])
    for r in m_s:
        r[...] = jnp.full_like(r, -jnp.inf)
    for r in acc_s:
        r[...] = jnp.zeros_like(r)

    def scores(kt, c0, nr, hh, groups):
        k0 = pl.multiple_of(kt * tk, tk) + c0
        k_rows = k_ref[pl.ds(k0, nr), hh * LANES:(hh + 1) * LANES]
        return [_dot_nt(k_rows, qgs[hh][g]) for g in groups]

    def update(ss, kt, c0, nr, hh, groups, tri):
        vt = jnp.concatenate([vt_ref[kt, hh * HEAD_V:(hh + 1) * HEAD_V, c0:c0 + nr],
                              jnp.ones((8, nr), BF16)], axis=0)
        ps, alphas = [], []
        for g, s in zip(groups, ss):
            n = hh * ngrp + g
            sb = s.astype(BF16)
            if g in tri:
                key = lax.broadcasted_iota(jnp.int32, (nr, gw), 0)
                qry = lax.broadcasted_iota(jnp.int32, (nr, gw), 1)
                sb = jnp.where(qry >= key, sb, -jnp.inf)
            m_prev = m_s[n][...]
            m_new = jnp.maximum(m_prev, jnp.max(sb, axis=0, keepdims=True).astype(F32))
            alphas.append(jnp.exp2(m_prev - m_new))
            ps.append(jnp.exp2(sb - m_new.astype(BF16)))
            m_s[n][...] = m_new
        pvs = [_dot(vt, p) for p in ps]
        for g, alpha, pv in zip(groups, alphas, pvs):
            n = hh * ngrp + g
            acc_s[n][...] = alpha * acc_s[n][...] + pv

    def run(kt, units):
        sss = [scores(kt, c0, nr, hh, groups) for c0, nr, hh, groups, _ in units]
        for ss, (c0, nr, hh, groups, tri) in zip(sss, units):
            update(ss, kt, c0, nr, hh, groups, tri)

    every = list(range(ngrp))
    full = [(0, tk, hh, every, ()) for hh in range(DIFF_HPS)]

    def body(kt, carry):
        run(kt, full)
        return carry

    lax.fori_loop(0, i, body, 0)

    diagonal = []
    for sub in range(tk // gw):
        keep = [g for g in every if (g * gw) % tq >= sub * gw]
        on_diag = tuple(g for g in keep if (g * gw) % tq == sub * gw)
        diagonal += [(sub * gw, gw, hh, keep, on_diag) for hh in range(DIFF_HPS)]
    run(i, diagonal)

    lam = (jnp.exp(jnp.sum(lq1_ref[...] * lk1_ref[...], axis=-1, keepdims=True))
           - jnp.exp(jnp.sum(lq2_ref[...] * lk2_ref[...], axis=-1, keepdims=True)) + lambda_init)
    half = ngrp // 2
    for hh in range(DIFF_HPS):
        for g in range(half):
            n1, n2 = hh * ngrp + g, hh * ngrp + g + half
            a1, a2 = acc_s[n1][...], acc_s[n2][...]
            o = a1[:HEAD_V] / a1[HEAD_V:HEAD_V + 1] - lam * (a2[:HEAD_V] / a2[HEAD_V:HEAD_V + 1])
            o = o * lax.rsqrt(jnp.mean(o * o, axis=0, keepdims=True) + EPS) * dn_ref[...] * (1.0 - lambda_init)
            o_ref[g * gw:(g + 1) * gw, hh * HEAD_V:(hh + 1) * HEAD_V] = o.T.astype(BF16)


def _differential(dq, dk, dvt, lq1, lk1, lq2, lk2, diff_norm, lambda_init, batch, seq):
    tq = DIFF_TQ
    nkt = seq // DIFF_TK
    nst = DIFF_HPS * 2 * tq // DIFF_GW
    hw = DIFF_HPS * LANES
    q3, k3 = (t.reshape(batch, seq, MIX_W) for t in (dq, dk))
    qspec = pl.BlockSpec((None, tq, hw), lambda b, h, i: (b, i, h))
    kspec = pl.BlockSpec((None, seq, hw), lambda b, h, i: (b, 0, h))
    vspec = pl.BlockSpec((nkt, hw, DIFF_TK), lambda b, h, i: (b, h, 0))
    vec = lambda p: p.reshape(1, DIFF_DK).astype(F32)
    out = pl.pallas_call(
        functools.partial(_diff_kernel, lambda_init=lambda_init),
        grid=(batch, N_HEADS // DIFF_HPS, seq // tq),
        in_specs=[qspec, kspec, vspec] + [_const_spec((1, DIFF_DK))] * 4 + [_const_spec((HEAD_V, 1))],
        out_specs=qspec,
        out_shape=jax.ShapeDtypeStruct((batch, seq, MIX_W), BF16),
        scratch_shapes=([pltpu.VMEM((1, DIFF_GW), F32)] * nst
                        + [pltpu.VMEM((HEAD_V + 8, DIFF_GW), F32)] * nst),
        compiler_params=_params("parallel", "parallel", "arbitrary"),
        name="differential_attention",
    )(q3, k3, dvt, vec(lq1), vec(lk1), vec(lq2), vec(lk2), diff_norm.reshape(HEAD_V, 1))
    return out.reshape(batch * seq, MIX_W)


def kernel(x, l0_ffn1_norm, l0_ffn1_w_up, l0_ffn1_w_down, l0_mix_norm, l0_w_in, l0_conv_w, l0_a_log,
           l0_dt_bias, l0_gdn_norm, l0_w_out, l0_ffn2_norm, l0_ffn2_w_up, l0_ffn2_w_down,
           l1_ffn1_norm, l1_ffn1_w_up, l1_ffn1_w_down, l1_mix_norm, l1_w_in, l1_lambda_q1, l1_lambda_k1,
           l1_lambda_q2, l1_lambda_k2, l1_diff_norm, l1_w_out, l1_ffn2_norm, l1_ffn2_w_up, l1_ffn2_w_down,
           final_norm):
    batch, seq, _ = x.shape
    cos64, sin64 = _rope_tables(seq, 64)
    cos128, sin128 = _rope_tables(seq, 128)
    xf = x.reshape(batch * seq, D_MODEL)

    xf = _ffn(xf, l0_ffn1_norm, l0_ffn1_w_up, l0_ffn1_w_down)
    qkv, z, rq, rk, rv, rg, ba = _in0(xf, l0_mix_norm, _pack_w_in0(l0_w_in), cos64, sin64, seq)
    o_a = _gdn(qkv, ba, z, l0_conv_w, l0_a_log, l0_dt_bias, l0_gdn_norm, batch, seq)
    o_b = _retention(rq, rk, rv, rg, batch, seq)
    xf = _ffn(xf, l0_ffn2_norm, l0_ffn2_w_up, l0_ffn2_w_down, mix=(o_a, o_b, l0_w_out))

    xf = _ffn(xf, l1_ffn1_norm, l1_ffn1_w_up, l1_ffn1_w_down)
    w1, wv_t = _pack_w_in1(l1_w_in)
    cq, ck, cv, dq, dk, dvt = _in1(xf, l1_mix_norm, w1, wv_t, cos128, sin128, cos64, sin64, seq)
    o_c = _dilated(cq, ck, cv, batch, seq)
    lambda_init = 0.8 - 0.6 * math.exp(-0.3 * 1)
    o_d = _differential(dq, dk, dvt, l1_lambda_q1, l1_lambda_k1, l1_lambda_q2, l1_lambda_k2,
                        l1_diff_norm, lambda_init, batch, seq)
    xf = _ffn(xf, l1_ffn2_norm, l1_ffn2_w_up, l1_ffn2_w_down, mix=(o_c, o_d, l1_w_out), final_norm=final_norm)
    return xf.reshape(batch, seq, D_MODEL)
```

```python
import functools
import math

import numpy as np
import jax
import jax.numpy as jnp
from jax import lax
from jax.experimental import pallas as pl
from jax.experimental.pallas import tpu as pltpu

F32 = jnp.float32
BF16 = jnp.bfloat16
HIGHEST = lax.Precision.HIGHEST

EPS = 1e-6
ROPE_THETA = 10000.0
D_MODEL = 1024
D_FF = 2816
N_HEADS = 4
HEAD_V = 128
GDN_DK = 128
GDN_CONV = 4
GDN_CHUNK = 64
GDN_QK_W = N_HEADS * GDN_DK
GDN_V_W = N_HEADS * HEAD_V
GDN_CONV_W = 2 * GDN_QK_W + GDN_V_W
RET_DK = 64
RET_QK_W = N_HEADS * RET_DK
DIL_DH = 128
DIL_PATTERNS = ((128, 1), (512, 4), (2048, 16))
DIL_KEYS = 128
DIFF_DK = 64
MIX_W = N_HEADS * HEAD_V

LANES = 128
VMEM_LIMIT = 56 * 1024 * 1024

FFN_TM = 1024
FFN_TF = 256
PROJ_TM = 1024
GDN_TS = 256
GDN_TILE = 1024
RET_C = 256
RET_TILE = 1024
DIL_BLK = 2048
DIFF_TQ = 512
DIFF_TK = 512
DIFF_HPS = 4
DIFF_GW = 256
LOG2E = math.log2(math.e)


def _rms(x):
    return x * lax.rsqrt(jnp.mean(x * x, axis=-1, keepdims=True) + EPS)


def _silu(x):
    return x * (1.0 / (1.0 + jnp.exp(-x)))


def _dot(a, b):
    return jnp.dot(a, b, preferred_element_type=F32)


def _dot_nt(a, b):
    return lax.dot_general(a, b, (((1,), (1,)), ((), ())), preferred_element_type=F32)


def _dot_tn(a, b):
    return lax.dot_general(a, b, (((0,), (0,)), ((), ())), preferred_element_type=F32)


def _params(*sem):
    return pltpu.CompilerParams(dimension_semantics=sem, vmem_limit_bytes=VMEM_LIMIT)


def _const_spec(shape):
    nd = len(shape)
    return pl.BlockSpec(shape, lambda *_: (0,) * nd, pipeline_mode=pl.Buffered(1))


def _ffn_kernel(*refs, mixed, final):
    refs = list(refs)
    x_ref = refs.pop(0)
    x = x_ref[...]
    if mixed:
        oa_ref, ob_ref, wo_ref = refs.pop(0), refs.pop(0), refs.pop(0)
        x = x + _dot(oa_ref[...], wo_ref[0:MIX_W, :]) + _dot(ob_ref[...], wo_ref[MIX_W:2 * MIX_W, :])
    n_ref, wup_ref, wd_ref = refs.pop(0), refs.pop(0), refs.pop(0)
    fn_ref = refs.pop(0) if final else None
    o_ref, acc_ref = refs
    hb = (_rms(x) * n_ref[...]).astype(BF16)
    for c in range(D_FF // FFN_TF):
        lo = c * FFN_TF
        g = _dot(hb, wup_ref[:, lo:lo + FFN_TF])
        u = _dot(hb, wup_ref[:, D_FF + lo:D_FF + lo + FFN_TF])
        a = (_silu(g) * u).astype(BF16)
        d = _dot(a, wd_ref[lo:lo + FFN_TF, :])
        if c == 0:
            acc_ref[...] = d
        else:
            acc_ref[...] += d
    y = x + 0.5 * acc_ref[...]
    if final:
        y = _rms(y) * fn_ref[...]
    o_ref[...] = y


def _ffn(x2d, norm, w_up, w_down, mix=None, final_norm=None):
    T = x2d.shape[0]
    row = lambda w: pl.BlockSpec((FFN_TM, w), lambda i: (i, 0))
    args, specs = [x2d], [row(D_MODEL)]
    if mix is not None:
        o_a, o_b, w_out = mix
        args += [o_a, o_b, w_out.astype(BF16)]
        specs += [row(MIX_W), row(MIX_W), _const_spec((2 * MIX_W, D_MODEL))]
    args += [norm.reshape(1, D_MODEL), w_up.astype(BF16), w_down.astype(BF16)]
    specs += [_const_spec((1, D_MODEL)), _const_spec((D_MODEL, 2 * D_FF)), _const_spec((D_FF, D_MODEL))]
    if final_norm is not None:
        args.append(final_norm.reshape(1, D_MODEL))
        specs.append(_const_spec((1, D_MODEL)))
    return pl.pallas_call(
        functools.partial(_ffn_kernel, mixed=mix is not None, final=final_norm is not None),
        grid=(T // FFN_TM,),
        in_specs=specs,
        out_specs=row(D_MODEL),
        out_shape=jax.ShapeDtypeStruct((T, D_MODEL), F32),
        scratch_shapes=[pltpu.VMEM((FFN_TM, D_MODEL), F32)],
        compiler_params=_params("parallel"),
        name="ffn_final" if final_norm is not None else ("ffn_mix" if mix is not None else "ffn"),
    )(*args)


def _rope_tables(seq, dim):
    half = dim // 2
    inv = (np.float32(ROPE_THETA) ** (-np.arange(0, dim, 2, dtype=np.float32) / np.float32(dim))).astype(np.float32)
    ang = np.arange(seq, dtype=np.float32)[:, None] * inv[None, :]
    reps = (LANES // 2) // half
    cos = np.tile(np.cos(ang), (1, 2 * reps))
    sin = np.tile(np.sin(ang), (1, reps))
    return jnp.asarray(cos), jnp.asarray(np.concatenate([-sin, sin], axis=1))


def _rope(x, cos, sin):
    outs = []
    for g in range(x.shape[1] // LANES):
        xg = x[:, g * LANES:(g + 1) * LANES]
        outs.append(xg * cos + pltpu.roll(xg, LANES // 2, 1) * sin)
    return outs[0] if len(outs) == 1 else jnp.concatenate(outs, axis=1)


_P0_QKV = (0, GDN_CONV_W)
_P0_Z = (_P0_QKV[1], _P0_QKV[1] + GDN_V_W)
_P0_RQ = (_P0_Z[1], _P0_Z[1] + RET_QK_W)
_P0_RK = (_P0_RQ[1], _P0_RQ[1] + RET_QK_W)
_P0_RV = (_P0_RK[1], _P0_RK[1] + MIX_W)
_P0_RG = (_P0_RV[1], _P0_RV[1] + MIX_W)
_P0_BA = (_P0_RG[1], _P0_RG[1] + LANES)
_P0_W = _P0_BA[1]


def _ret_perm():
    perm = np.zeros(RET_QK_W, np.int32)
    for h in range(N_HEADS):
        for d in range(RET_DK):
            new = (h // 2) * LANES + (d // 32) * 64 + (h % 2) * 32 + d % 32
            perm[new] = h * RET_DK + d
    return perm


def _diff_perm():
    perm = np.zeros(N_HEADS * 2 * DIFF_DK, np.int32)
    for h in range(N_HEADS):
        for c in range(2):
            for d in range(DIFF_DK):
                new = h * LANES + (d // 32) * 64 + c * 32 + d % 32
                perm[new] = h * 2 * DIFF_DK + c * DIFF_DK + d
    return perm


def _pack_w_in0(w_in):
    o = np.cumsum((0, GDN_CONV_W, GDN_V_W, N_HEADS, N_HEADS, RET_QK_W, RET_QK_W, MIX_W, MIX_W))
    qkv, z, b, a, rq, rk, rv, rg = (w_in[:, o[i]:o[i + 1]] for i in range(8))
    perm = _ret_perm()
    pad = jnp.zeros((D_MODEL, LANES - 2 * N_HEADS), w_in.dtype)
    cols = [qkv, z, rq[:, perm], rk[:, perm], rv, rg, b, a, pad]
    return jnp.concatenate(cols, axis=1).astype(BF16)


def _in0_kernel(x_ref, n_ref, w_ref, cos_ref, sin_ref,
                qkv_ref, z_ref, rq_ref, rk_ref, rv_ref, rg_ref, ba_ref):
    hb = (_rms(x_ref[...]) * n_ref[...]).astype(BF16)

    def proj(seg):
        return _dot(hb, w_ref[:, seg[0]:seg[1]])

    cos = cos_ref[...]
    sin = sin_ref[...]
    qkv_ref[...] = proj(_P0_QKV)
    z_ref[...] = proj(_P0_Z).astype(BF16)
    rq_ref[...] = (_rope(proj(_P0_RQ), cos, sin) * RET_DK ** -0.5).astype(BF16)
    rk_ref[...] = _rope(proj(_P0_RK), cos, sin).astype(BF16)
    rv_ref[...] = proj(_P0_RV).astype(BF16)
    rg_ref[...] = proj(_P0_RG).astype(BF16)
    ba_ref[...] = proj(_P0_BA)


def _in0(x2d, norm, w_packed, cos, sin, seq):
    T = x2d.shape[0]
    tm = PROJ_TM
    spt = seq // tm
    row = lambda w: pl.BlockSpec((tm, w), lambda i: (i, 0))
    tab = pl.BlockSpec((tm, LANES), lambda i: (i % spt, 0))
    widths = (GDN_CONV_W, GDN_V_W, RET_QK_W, RET_QK_W, MIX_W, MIX_W, LANES)
    dtypes = (F32, BF16, BF16, BF16, BF16, BF16, F32)
    return pl.pallas_call(
        _in0_kernel,
        grid=(T // tm,),
        in_specs=[row(D_MODEL), _const_spec((1, D_MODEL)), _const_spec((D_MODEL, _P0_W)), tab, tab],
        out_specs=[row(w) for w in widths],
        out_shape=[jax.ShapeDtypeStruct((T, w), dt) for w, dt in zip(widths, dtypes)],
        compiler_params=_params("parallel"),
        name="in_proj0",
    )(x2d, norm.reshape(1, D_MODEL), w_packed, cos, sin)


def _bmm(a, b):
    return jnp.einsum("gik,gkj->gij", a.astype(BF16), b.astype(BF16), preferred_element_type=F32)


def _unit_lower_inverse(a, ri, ci):
    n = jnp.where((ri >> 3) == (ci >> 3), -a, 0.0)
    t = jnp.where(ri == ci, 1.0, 0.0) + n
    p = _bmm(n, n)
    t = t + _bmm(t, p)
    p = _bmm(p, p)
    t = t + _bmm(t, p)
    for sh in (3, 4, 5):
        e = jnp.where(((ri >> (sh + 1)) == (ci >> (sh + 1))) & ((ri >> sh) != (ci >> sh)), a, 0.0)
        t = t - _bmm(_bmm(t, e), t)
    return t


def _gdn_kernel(qkv_ref, ba_ref, z_ref, cw_ref, alog_ref, dtb_ref, gn_ref, o_ref,
                xe_ref, o_s, state_ref):
    @pl.when(pl.program_id(1) == 0)
    def _():
        xe_ref[...] = jnp.zeros_like(xe_ref)
        state_ref[...] = jnp.zeros_like(state_ref)

    for t in range(qkv_ref.shape[0] // GDN_TS):
        rows = slice(t * GDN_TS, (t + 1) * GDN_TS)
        _gdn_tile(qkv_ref.at[rows], ba_ref.at[rows], z_ref.at[rows], cw_ref, alog_ref, dtb_ref, gn_ref,
                  o_ref.at[rows], xe_ref, o_s, state_ref)


def _gdn_tile(qkv_ref, ba_ref, z_ref, cw_ref, alog_ref, dtb_ref, gn_ref, o_ref, xe_ref, o_s, state_ref):
    ts = qkv_ref.shape[0]
    C = GDN_CHUNK
    nch = ts // C
    halo = 8

    beta = 1.0 / (1.0 + jnp.exp(-ba_ref[...]))
    xa = ba_ref[...] + dtb_ref[...]
    softplus = jnp.maximum(xa, 0.0) + jnp.log(1.0 + jnp.exp(-jnp.abs(xa)))
    g = -jnp.exp(alog_ref[...]) * softplus
    rt = lax.broadcasted_iota(jnp.int32, (ts, ts), 0)
    ct = lax.broadcasted_iota(jnp.int32, (ts, ts), 1)
    tri = jnp.where((rt >= ct) & ((rt >> 6) == (ct >> 6)), 1.0, 0.0)
    gc = jnp.dot(tri, g, precision=HIGHEST, preferred_element_type=F32)

    x = qkv_ref[...]
    xb = x.astype(BF16)
    shift = jnp.concatenate([jnp.where(rt - ct == k, 1.0, 0.0) for k in range(1, GDN_CONV)], axis=0).astype(BF16)
    shifted = _dot(shift, xb)
    prev = xe_ref[...]
    row8 = lax.broadcasted_iota(jnp.int32, (halo, 1), 0)
    conv = cw_ref[GDN_CONV - 1:GDN_CONV, :] * x
    for k in range(1, GDN_CONV):
        sh = shifted[(k - 1) * ts:k * ts]
        head = jnp.where(row8 < k, pltpu.roll(prev, k, 0), sh[0:halo])
        conv = conv + cw_ref[GDN_CONV - 1 - k:GDN_CONV - k, :] * jnp.concatenate([head, sh[halo:]], axis=0)
    xe_ref[...] = x[ts - halo:ts]
    act = _silu(conv)
    qn, kn, vv = [], [], []
    for h in range(N_HEADS):
        lo = h * GDN_DK
        qh = act[:, lo:lo + GDN_DK]
        kh = act[:, GDN_QK_W + lo:GDN_QK_W + lo + GDN_DK]
        qn.append(qh * (lax.rsqrt(jnp.sum(qh * qh, -1, keepdims=True) + EPS) * GDN_DK ** -0.5))
        kn.append(kh * lax.rsqrt(jnp.sum(kh * kh, -1, keepdims=True) + EPS))
        vv.append(act[:, 2 * GDN_QK_W + h * HEAD_V:2 * GDN_QK_W + (h + 1) * HEAD_V])

    units = [(c, h) for c in range(nch) for h in range(N_HEADS)]
    rows = lambda x, c: x[c * C:(c + 1) * C]
    stack = lambda f: jnp.stack([f(c, h) for c, h in units], axis=0)
    q = stack(lambda c, h: rows(qn[h], c))
    k = stack(lambda c, h: rows(kn[h], c))
    v = stack(lambda c, h: rows(vv[h], c))
    gct = [rows(gc, c).T for c in range(nch)]
    gcol = stack(lambda c, h: rows(gc, c)[:, N_HEADS + h:N_HEADS + h + 1])
    grow = stack(lambda c, h: gct[c][N_HEADS + h:N_HEADS + h + 1, :])
    bcol = stack(lambda c, h: rows(beta, c)[:, h:h + 1])
    glast = gcol[:, C - 1:C, :]

    ri = lax.broadcasted_iota(jnp.int32, (1, C, C), 1)
    ci = lax.broadcasted_iota(jnp.int32, (1, C, C), 2)
    causal = ri >= ci
    decay = jnp.where(causal, jnp.exp(jnp.where(causal, gcol - grow, 0.0)), 0.0)
    kb = k.astype(BF16)
    kk = jnp.einsum("gid,gjd->gij", kb, kb, preferred_element_type=F32)
    a_mat = jnp.where(ri > ci, kk * decay * bcol, 0.0)
    t_inv = _unit_lower_inverse(a_mat, ri, ci)
    eg = jnp.exp(gcol)
    rhs = jnp.concatenate([v * bcol, k * (bcol * eg)], axis=2).astype(BF16)
    sol = jnp.einsum("gik,gkj->gij", t_inv.astype(BF16), rhs, preferred_element_type=F32)
    solb = sol.astype(BF16)
    qk = (jnp.einsum("gid,gjd->gij", q.astype(BF16), kb, preferred_element_type=F32) * decay).astype(BF16)
    qk_uw = jnp.einsum("gij,gje->gie", qk, solb, preferred_element_type=F32)
    kd = k * jnp.exp(glast - gcol)
    kdt = jnp.stack([kd[n].T for n in range(len(units))], axis=0).astype(BF16)
    kd_uw = jnp.einsum("gdc,gce->gde", kdt, solb, preferred_element_type=F32)
    lhs_all = jnp.concatenate([kd_uw[:, :, HEAD_V:], q * eg - qk_uw[:, :, HEAD_V:]], axis=1).astype(BF16)
    c_all = kd_uw[:, :, :HEAD_V]
    o0_all = qk_uw[:, :, :HEAD_V]
    egl = jnp.exp(glast)

    states = [state_ref[h] for h in range(N_HEADS)]
    for n, (c, h) in enumerate(units):
        r = _dot(lhs_all[n], states[h].astype(BF16))
        o_s[c * C:(c + 1) * C, h * HEAD_V:(h + 1) * HEAD_V] = o0_all[n] + r[GDN_DK:]
        states[h] = states[h] * egl[n] + c_all[n] - r[:GDN_DK]
    for h in range(N_HEADS):
        state_ref[h] = states[h]

    for h in range(N_HEADS):
        lo = h * HEAD_V
        zh = z_ref[:, lo:lo + HEAD_V].astype(F32)
        o_ref[:, lo:lo + HEAD_V] = (_rms(o_s[:, lo:lo + HEAD_V]) * gn_ref[...] * _silu(zh)).astype(BF16)


def _gdn(qkv, ba, z, conv_w, a_log, dt_bias, gdn_norm, batch, seq):
    T = qkv.shape[0]
    ts = GDN_TS
    spt = seq // GDN_TILE
    row = lambda w: pl.BlockSpec((GDN_TILE, w), lambda bi, i: (bi * spt + i, 0))
    pad4 = lambda p: jnp.pad(p.astype(F32), (N_HEADS, LANES - 2 * N_HEADS)).reshape(1, LANES)
    return pl.pallas_call(
        _gdn_kernel,
        grid=(batch, spt),
        in_specs=[row(GDN_CONV_W), row(LANES), row(GDN_V_W),
                  _const_spec((GDN_CONV, GDN_CONV_W)), _const_spec((1, LANES)), _const_spec((1, LANES)),
                  _const_spec((1, HEAD_V))],
        out_specs=row(GDN_V_W),
        out_shape=jax.ShapeDtypeStruct((T, GDN_V_W), BF16),
        scratch_shapes=[
            pltpu.VMEM((8, GDN_CONV_W), F32),
            pltpu.VMEM((ts, GDN_V_W), F32),
            pltpu.VMEM((N_HEADS, GDN_DK, HEAD_V), F32),
        ],
        compiler_params=_params("arbitrary", "arbitrary"),
        name="gated_deltanet",
    )(qkv, ba, z, conv_w.astype(F32), pad4(a_log), pad4(dt_bias), gdn_norm.reshape(1, HEAD_V))


def _ret_kernel(q_ref, k_ref, v_ref, g_ref, o_ref, state_ref, dm_ref):
    i = pl.program_id(1)
    c = RET_C

    @pl.when(i == 0)
    def _():
        state_ref[...] = jnp.zeros_like(state_ref)

    @pl.when((pl.program_id(0) == 0) & (i == 0))
    def _():
        ri = lax.broadcasted_iota(jnp.int32, (c, c), 0)
        ci = lax.broadcasted_iota(jnp.int32, (c, c), 1)
        rel = (ri - ci).astype(F32)
        for h in range(N_HEADS):
            lg = math.log(1.0 - 2.0 ** (-5.0 - h))
            dm_ref[h] = jnp.where(rel >= 0, jnp.exp(jnp.where(rel >= 0, rel, 0.0) * lg), 0.0)

    pos = lax.broadcasted_iota(jnp.int32, (c, 1), 0).astype(F32)
    lane = lax.broadcasted_iota(jnp.int32, (1, LANES), 1)
    for t in range(q_ref.shape[0] // c):
        rows = slice(t * c, (t + 1) * c)
        for h in range(N_HEADS):
            lg = math.log(1.0 - 2.0 ** (-5.0 - h))
            grp = (h // 2) * LANES
            own = ((lane >> 5) & 1) == h % 2
            qm = jnp.where(own, q_ref[rows, grp:grp + LANES], 0)
            km = jnp.where(own, k_ref[rows, grp:grp + LANES], 0)
            v = v_ref[rows, h * HEAD_V:(h + 1) * HEAD_V]
            intra = _dot((_dot_nt(qm, km) * dm_ref[h]).astype(BF16), v)
            state = state_ref[h]
            inter = _dot(qm, state.astype(BF16)) * jnp.exp((pos + 1.0) * lg)
            k_sc = (km.astype(F32) * jnp.exp((c - 1.0 - pos) * lg)).astype(BF16)
            state_ref[h] = state * math.exp(c * lg) + _dot_tn(k_sc, v)
            gate = _silu(g_ref[rows, h * HEAD_V:(h + 1) * HEAD_V].astype(F32))
            o_ref[rows, h * HEAD_V:(h + 1) * HEAD_V] = (_rms(intra + inter) * gate).astype(BF16)


def _retention(rq, rk, rv, rg, batch, seq):
    T = rq.shape[0]
    c = RET_C
    tr = RET_TILE
    spt = seq // tr
    row = lambda w: pl.BlockSpec((tr, w), lambda bi, i: (bi * spt + i, 0))
    return pl.pallas_call(
        _ret_kernel,
        grid=(batch, spt),
        in_specs=[row(RET_QK_W), row(RET_QK_W), row(MIX_W), row(MIX_W)],
        out_specs=row(MIX_W),
        out_shape=jax.ShapeDtypeStruct((T, MIX_W), BF16),
        scratch_shapes=[pltpu.VMEM((N_HEADS, LANES, HEAD_V), F32), pltpu.VMEM((N_HEADS, c, c), F32)],
        compiler_params=_params("arbitrary", "arbitrary"),
        name="retention",
    )(rq, rk, rv, rg)


def _pack_w_in1(w_in):
    o = np.cumsum((0,) + (MIX_W,) * 6)
    cq, ck, cv, dq, dk, dv = (w_in[:, o[i]:o[i + 1]] for i in range(6))
    perm = _diff_perm()
    w = jnp.concatenate([cq, ck, cv, dq[:, perm], dk[:, perm]], axis=1).astype(BF16)
    return w, dv.T.astype(BF16)


def _in1_kernel(x_ref, n_ref, w_ref, wvt_ref, cosa_ref, sina_ref, cosb_ref, sinb_ref,
                cq_ref, ck_ref, cv_ref, dq_ref, dk_ref, dvt_ref):
    hb = (_rms(x_ref[...]) * n_ref[...]).astype(BF16)

    def proj(j):
        return _dot(hb, w_ref[:, j * MIX_W:(j + 1) * MIX_W])

    cq_ref[...] = _rope(proj(0), cosa_ref[...], sina_ref[...]) * (DIL_DH ** -0.5 * LOG2E)
    ck_ref[...] = _rope(proj(1), cosa_ref[...], sina_ref[...])
    cv_ref[...] = proj(2)
    dq_ref[...] = (_rope(proj(3), cosb_ref[...], sinb_ref[...]) * (DIFF_DK ** -0.5 * LOG2E)).astype(BF16)
    dk_ref[...] = _rope(proj(4), cosb_ref[...], sinb_ref[...]).astype(BF16)
    vt = _dot_nt(wvt_ref[...], hb).astype(BF16)
    for j in range(dvt_ref.shape[0]):
        dvt_ref[j] = vt[:, j * DIFF_TK:(j + 1) * DIFF_TK]


def _in1(x2d, norm, w_packed, wv_t, cosa, sina, cosb, sinb, seq):
    T = x2d.shape[0]
    tm = PROJ_TM
    spt = seq // tm
    vblk = tm // DIFF_TK
    row = lambda w: pl.BlockSpec((tm, w), lambda i: (i, 0))
    tab = pl.BlockSpec((tm, LANES), lambda i: (i % spt, 0))
    out_shape = [jax.ShapeDtypeStruct((T, MIX_W), dt) for dt in (F32, F32, F32, BF16, BF16)]
    out_shape.append(jax.ShapeDtypeStruct((T // DIFF_TK, MIX_W, DIFF_TK), BF16))
    return pl.pallas_call(
        _in1_kernel,
        grid=(T // tm,),
        in_specs=[row(D_MODEL), _const_spec((1, D_MODEL)), _const_spec((D_MODEL, 5 * MIX_W)),
                  _const_spec((MIX_W, D_MODEL)), tab, tab, tab, tab],
        out_specs=[row(MIX_W)] * 5 + [pl.BlockSpec((vblk, MIX_W, DIFF_TK), lambda i: (i, 0, 0))],
        out_shape=out_shape,
        compiler_params=_params("parallel"),
        name="in_proj1",
    )(x2d, norm.reshape(1, D_MODEL), w_packed, wv_t, cosa, sina, cosb, sinb)


def _dil_kernel(q_ref, kp_ref, kc_ref, vp_ref, vc_ref, o_ref, ob, mb, lb):
    i = pl.program_id(2)
    blk = q_ref.shape[0]
    nk = DIL_KEYS

    qi = lax.broadcasted_iota(jnp.int32, (nk, 2 * nk), 0)
    kj = lax.broadcasted_iota(jnp.int32, (nk, 2 * nk), 1)
    dist = qi + nk - kj
    band = (dist >= 0) & (dist <= nk)
    band_first = band & (kj >= jnp.where(i > 0, 0, nk))
    ones_v = jnp.ones((2 * nk, LANES), BF16)

    def window(prev_ref, cur_ref, start, d):
        if start >= blk:
            return cur_ref[pl.ds(start - blk, 2 * nk, stride=d), :]
        return jnp.concatenate([prev_ref[pl.ds(start, nk, stride=d), :],
                                cur_ref[pl.ds(start + nk * d - blk, nk, stride=d), :]], axis=0)

    for gi, (_, d) in enumerate(DIL_PATTERNS):
        span = nk * d
        pos = [((n // d) * span + n % d, n // d) for n in range(blk // nk)]
        qs_ = [q_ref[pl.ds(qs, nk, stride=d), :].astype(BF16) for qs, _ in pos]
        ks_ = [window(kp_ref, kc_ref, qs + blk - span, d).astype(BF16) for qs, _ in pos]
        vs_ = [jnp.concatenate([window(vp_ref, vc_ref, qs + blk - span, d).astype(BF16), ones_v], axis=1)
               for qs, _ in pos]
        ss = [_dot_nt(q, k) for q, k in zip(qs_, ks_)]
        ps, ms = [], []
        for s, (_, sub) in zip(ss, pos):
            sb = jnp.where(band if sub > 0 else band_first, s.astype(BF16), -jnp.inf)
            m = jnp.max(sb, axis=-1, keepdims=True)
            ps.append(jnp.exp2(sb - m))
            ms.append(m.astype(F32))
        os_ = [_dot(p, v) for p, v in zip(ps, vs_)]
        for (qs, _), o, m in zip(pos, os_, ms):
            ob[gi, pl.ds(qs, nk, stride=d), :] = o[:, :HEAD_V]
            lb[gi, pl.ds(qs, nk, stride=d), :] = o[:, HEAD_V:]
            mb[gi, pl.ds(qs, nk, stride=d), :] = jnp.broadcast_to(m, (nk, LANES))

    m_all = jnp.maximum(jnp.maximum(mb[0], mb[1]), mb[2])
    num = jnp.zeros((blk, HEAD_V), F32)
    den = jnp.zeros((blk, LANES), F32)
    for gi in range(len(DIL_PATTERNS)):
        sc = jnp.exp2(mb[gi] - m_all)
        num = num + ob[gi] * sc
        den = den + lb[gi] * sc
    o_ref[...] = (num / den).astype(BF16)


def _dilated(cq, ck, cv, batch, seq):
    blk = DIL_BLK
    q3, k3, v3 = (t.reshape(batch, seq, MIX_W) for t in (cq, ck, cv))
    cur = pl.BlockSpec((None, blk, DIL_DH), lambda b, h, i: (b, i, h))
    prev = pl.BlockSpec((None, blk, DIL_DH), lambda b, h, i: (b, jnp.maximum(i - 1, 0), h))
    nb = len(DIL_PATTERNS)
    out = pl.pallas_call(
        _dil_kernel,
        grid=(batch, N_HEADS, seq // blk),
        in_specs=[cur, prev, cur, prev, cur],
        out_specs=cur,
        out_shape=jax.ShapeDtypeStruct((batch, seq, MIX_W), BF16),
        scratch_shapes=[
            pltpu.VMEM((nb, blk, HEAD_V), F32),
            pltpu.VMEM((nb, blk, LANES), F32),
            pltpu.VMEM((nb, blk, LANES), F32),
        ],
        compiler_params=_params("parallel", "parallel", "arbitrary"),
        name="dilated_attention",
    )(q3, k3, k3, v3, v3)
    return out.reshape(batch * seq, MIX_W)


def _diff_kernel(q_ref, k_ref, vt_ref, lq1_ref, lk1_ref, lq2_ref, lk2_ref, dn_ref, o_ref,
                 *stats, lambda_init):
    i = pl.program_id(2)
    tq = q_ref.shape[0]
    tk = DIFF_TK
    assert tq == tk, "the diagonal handling assumes square tiles"
    gw = DIFF_GW
    ngrp = 2 * tq // gw
    nst = DIFF_HPS * ngrp
    m_s, acc_s = stats[0:nst], stats[nst:2 * nst]
    lane = lax.broadcasted_iota(jnp.int32, (1, LANES), 1)
    first = ((lane >> 5) & 1) == 0
    qgs = []
    for hh in range(DIFF_HPS):
        q = q_ref[:, hh * LANES:(hh + 1) * LANES]
        qq = jnp.concatenate([jnp.where(first, q, 0), jnp.where(first, 0, q)], axis=0)
        qgs.append([qq[g * gw:(g + 1) * gw] for g in range(ngrp)])
    for r in m_s:
        r[...] = jnp.full_like(r, -jnp.inf)
    for r in acc_s:
        r[...] = jnp.zeros_like(r)

    def scores(kt, c0, nr, hh, groups):
        k0 = pl.multiple_of(kt * tk, tk) + c0
        k_rows = k_ref[pl.ds(k0, nr), hh * LANES:(hh + 1) * LANES]
        return [_dot_nt(k_rows, qgs[hh][g]) for g in groups]

    def update(ss, kt, c0, nr, hh, groups, tri):
        vt = jnp.concatenate([vt_ref[kt, hh * HEAD_V:(hh + 1) * HEAD_V, c0:c0 + nr],
                              jnp.ones((8, nr), BF16)], axis=0)
        ps, alphas = [], []
        for g, s in zip(groups, ss):
            n = hh * ngrp + g
            sb = s.astype(BF16)
            if g in tri:
                key = lax.broadcasted_iota(jnp.int32, (nr, gw), 0)
                qry = lax.broadcasted_iota(jnp.int32, (nr, gw), 1)
                sb = jnp.where(qry >= key, sb, -jnp.inf)
            m_prev = m_s[n][...]
            m_new = jnp.maximum(m_prev, jnp.max(sb, axis=0, keepdims=True).astype(F32))
            alphas.append(jnp.exp2(m_prev - m_new))
            ps.append(jnp.exp2(sb - m_new.astype(BF16)))
            m_s[n][...] = m_new
        pvs = [_dot(vt, p) for p in ps]
        for g, alpha, pv in zip(groups, alphas, pvs):
            n = hh * ngrp + g
            acc_s[n][...] = alpha * acc_s[n][...] + pv

    def run(kt, units):
        sss = [scores(kt, c0, nr, hh, groups) for c0, nr, hh, groups, _ in units]
        for ss, (c0, nr, hh, groups, tri) in zip(sss, units):
            update(ss, kt, c0, nr, hh, groups, tri)

    every = list(range(ngrp))
    full = [(0, tk, hh, every, ()) for hh in range(DIFF_HPS)]

    def body(kt, carry):
        run(kt, full)
        return carry

    lax.fori_loop(0, i, body, 0)

    diagonal = []
    for sub in range(tk // gw):
        keep = [g for g in every if (g * gw) % tq >= sub * gw]
        on_diag = tuple(g for g in keep if (g * gw) % tq == sub * gw)
        diagonal += [(sub * gw, gw, hh, keep, on_diag) for hh in range(DIFF_HPS)]
    run(i, diagonal)

    lam = (jnp.exp(jnp.sum(lq1_ref[...] * lk1_ref[...], axis=-1, keepdims=True))
           - jnp.exp(jnp.sum(lq2_ref[...] * lk2_ref[...], axis=-1, keepdims=True)) + lambda_init)
    half = ngrp // 2
    for hh in range(DIFF_HPS):
        for g in range(half):
            n1, n2 = hh * ngrp + g, hh * ngrp + g + half
            a1, a2 = acc_s[n1][...], acc_s[n2][...]
            o = a1[:HEAD_V] / a1[HEAD_V:HEAD_V + 1] - lam * (a2[:HEAD_V] / a2[HEAD_V:HEAD_V + 1])
            o = o * lax.rsqrt(jnp.mean(o * o, axis=0, keepdims=True) + EPS) * dn_ref[...] * (1.0 - lambda_init)
            o_ref[g * gw:(g + 1) * gw, hh * HEAD_V:(hh + 1) * HEAD_V] = o.T.astype(BF16)


def _differential(dq, dk, dvt, lq1, lk1, lq2, lk2, diff_norm, lambda_init, batch, seq):
    tq = DIFF_TQ
    nkt = seq // DIFF_TK
    nst = DIFF_HPS * 2 * tq // DIFF_GW
    hw = DIFF_HPS * LANES
    q3, k3 = (t.reshape(batch, seq, MIX_W) for t in (dq, dk))
    qspec = pl.BlockSpec((None, tq, hw), lambda b, h, i: (b, i, h))
    kspec = pl.BlockSpec((None, seq, hw), lambda b, h, i: (b, 0, h))
    vspec = pl.BlockSpec((nkt, hw, DIFF_TK), lambda b, h, i: (b, h, 0))
    vec = lambda p: p.reshape(1, DIFF_DK).astype(F32)
    out = pl.pallas_call(
        functools.partial(_diff_kernel, lambda_init=lambda_init),
        grid=(batch, N_HEADS // DIFF_HPS, seq // tq),
        in_specs=[qspec, kspec, vspec] + [_const_spec((1, DIFF_DK))] * 4 + [_const_spec((HEAD_V, 1))],
        out_specs=qspec,
        out_shape=jax.ShapeDtypeStruct((batch, seq, MIX_W), BF16),
        scratch_shapes=([pltpu.VMEM((1, DIFF_GW), F32)] * nst
                        + [pltpu.VMEM((HEAD_V + 8, DIFF_GW), F32)] * nst),
        compiler_params=_params("parallel", "parallel", "arbitrary"),
        name="differential_attention",
    )(q3, k3, dvt, vec(lq1), vec(lk1), vec(lq2), vec(lk2), diff_norm.reshape(HEAD_V, 1))
    return out.reshape(batch * seq, MIX_W)


def kernel(x, l0_ffn1_norm, l0_ffn1_w_up, l0_ffn1_w_down, l0_mix_norm, l0_w_in, l0_conv_w, l0_a_log,
           l0_dt_bias, l0_gdn_norm, l0_w_out, l0_ffn2_norm, l0_ffn2_w_up, l0_ffn2_w_down,
           l1_ffn1_norm, l1_ffn1_w_up, l1_ffn1_w_down, l1_mix_norm, l1_w_in, l1_lambda_q1, l1_lambda_k1,
           l1_lambda_q2, l1_lambda_k2, l1_diff_norm, l1_w_out, l1_ffn2_norm, l1_ffn2_w_up, l1_ffn2_w_down,
           final_norm):
    batch, seq, _ = x.shape
    cos64, sin64 = _rope_tables(seq, 64)
    cos128, sin128 = _rope_tables(seq, 128)
    xf = x.reshape(batch * seq, D_MODEL)

    xf = _ffn(xf, l0_ffn1_norm, l0_ffn1_w_up, l0_ffn1_w_down)
    qkv, z, rq, rk, rv, rg, ba = _in0(xf, l0_mix_norm, _pack_w_in0(l0_w_in), cos64, sin64, seq)
    o_a = _gdn(qkv, ba, z, l0_conv_w, l0_a_log, l0_dt_bias, l0_gdn_norm, batch, seq)
    o_b = _retention(rq, rk, rv, rg, batch, seq)
    xf = _ffn(xf, l0_ffn2_norm, l0_ffn2_w_up, l0_ffn2_w_down, mix=(o_a, o_b, l0_w_out))

    xf = _ffn(xf, l1_ffn1_norm, l1_ffn1_w_up, l1_ffn1_w_down)
    w1, wv_t = _pack_w_in1(l1_w_in)
    cq, ck, cv, dq, dk, dvt = _in1(xf, l1_mix_norm, w1, wv_t, cos128, sin128, cos64, sin64, seq)
    o_c = _dilated(cq, ck, cv, batch, seq)
    lambda_init = 0.8 - 0.6 * math.exp(-0.3 * 1)
    o_d = _differential(dq, dk, dvt, l1_lambda_q1, l1_lambda_k1, l1_lambda_q2, l1_lambda_k2,
                        l1_diff_norm, lambda_init, batch, seq)
    xf = _ffn(xf, l1_ffn2_norm, l1_ffn2_w_up, l1_ffn2_w_down, mix=(o_c, o_d, l1_w_out), final_norm=final_norm)
    return xf.reshape(batch, seq, D_MODEL)
```

```python
import functools
import math

import numpy as np
import jax
import jax.numpy as jnp
from jax import lax
from jax.experimental import pallas as pl
from jax.experimental.pallas import tpu as pltpu

F32 = jnp.float32
BF16 = jnp.bfloat16
HIGHEST = lax.Precision.HIGHEST

EPS = 1e-6
ROPE_THETA = 10000.0
D_MODEL = 1024
D_FF = 2816
N_HEADS = 4
HEAD_V = 128
GDN_DK = 128
GDN_CONV = 4
GDN_CHUNK = 64
GDN_QK_W = N_HEADS * GDN_DK
GDN_V_W = N_HEADS * HEAD_V
GDN_CONV_W = 2 * GDN_QK_W + GDN_V_W
RET_DK = 64
RET_QK_W = N_HEADS * RET_DK
DIL_DH = 128
DIL_PATTERNS = ((128, 1), (512, 4), (2048, 16))
DIL_KEYS = 128
DIFF_DK = 64
MIX_W = N_HEADS * HEAD_V

LANES = 128
SUBLANES = 8
VMEM_LIMIT = 56 * 1024 * 1024

FFN_TM = 1024
FFN_TF = 256
PROJ_TM = 1024
GDN_TS = 256
GDN_TILE = 1024
RET_C = 256
RET_TILE = 1024
DIL_BLK = 2048
DIFF_TQ = 512
DIFF_TK = 512
DIFF_HPS = 4
DIFF_GW = 256
LOG2E = math.log2(math.e)


def _rms(x):
    return x * lax.rsqrt(jnp.mean(x * x, axis=-1, keepdims=True) + EPS)


def _silu(x):
    return x * (1.0 / (1.0 + jnp.exp(-x)))


def _dot(a, b):
    return jnp.dot(a, b, preferred_element_type=F32)


def _dot_nt(a, b):
    return lax.dot_general(a, b, (((1,), (1,)), ((), ())), preferred_element_type=F32)


def _dot_tn(a, b):
    return lax.dot_general(a, b, (((0,), (0,)), ((), ())), preferred_element_type=F32)


def _params(*sem):
    return pltpu.CompilerParams(dimension_semantics=sem, vmem_limit_bytes=VMEM_LIMIT)


def _const_spec(shape):
    nd = len(shape)
    return pl.BlockSpec(shape, lambda *_: (0,) * nd, pipeline_mode=pl.Buffered(1))


def _ffn_kernel(*refs, mixed, final):
    refs = list(refs)
    x_ref = refs.pop(0)
    x = x_ref[...]
    if mixed:
        oa_ref, ob_ref, wo_ref = refs.pop(0), refs.pop(0), refs.pop(0)
        x = x + _dot(oa_ref[...], wo_ref[0:MIX_W, :]) + _dot(ob_ref[...], wo_ref[MIX_W:2 * MIX_W, :])
    n_ref, wup_ref, wd_ref = refs.pop(0), refs.pop(0), refs.pop(0)
    fn_ref = refs.pop(0) if final else None
    o_ref, acc_ref = refs
    hb = (_rms(x) * n_ref[...]).astype(BF16)
    for c in range(D_FF // FFN_TF):
        lo = c * FFN_TF
        g = _dot(hb, wup_ref[:, lo:lo + FFN_TF])
        u = _dot(hb, wup_ref[:, D_FF + lo:D_FF + lo + FFN_TF])
        a = (_silu(g) * u).astype(BF16)
        d = _dot(a, wd_ref[lo:lo + FFN_TF, :])
        if c == 0:
            acc_ref[...] = d
        else:
            acc_ref[...] += d
    y = x + 0.5 * acc_ref[...]
    if final:
        y = _rms(y) * fn_ref[...]
    o_ref[...] = y


def _ffn(x2d, norm, w_up, w_down, mix=None, final_norm=None):
    T = x2d.shape[0]
    row = lambda w: pl.BlockSpec((FFN_TM, w), lambda i: (i, 0))
    args, specs = [x2d], [row(D_MODEL)]
    if mix is not None:
        o_a, o_b, w_out = mix
        args += [o_a, o_b, w_out.astype(BF16)]
        specs += [row(MIX_W), row(MIX_W), _const_spec((2 * MIX_W, D_MODEL))]
    args += [norm.reshape(1, D_MODEL), w_up.astype(BF16), w_down.astype(BF16)]
    specs += [_const_spec((1, D_MODEL)), _const_spec((D_MODEL, 2 * D_FF)), _const_spec((D_FF, D_MODEL))]
    if final_norm is not None:
        args.append(final_norm.reshape(1, D_MODEL))
        specs.append(_const_spec((1, D_MODEL)))
    return pl.pallas_call(
        functools.partial(_ffn_kernel, mixed=mix is not None, final=final_norm is not None),
        grid=(T // FFN_TM,),
        in_specs=specs,
        out_specs=row(D_MODEL),
        out_shape=jax.ShapeDtypeStruct((T, D_MODEL), F32),
        scratch_shapes=[pltpu.VMEM((FFN_TM, D_MODEL), F32)],
        compiler_params=_params("parallel"),
        name="ffn_final" if final_norm is not None else ("ffn_mix" if mix is not None else "ffn"),
    )(*args)


def _rope_tables(seq, dim):
    half = dim // 2
    inv = (np.float32(ROPE_THETA) ** (-np.arange(0, dim, 2, dtype=np.float32) / np.float32(dim))).astype(np.float32)
    ang = np.arange(seq, dtype=np.float32)[:, None] * inv[None, :]
    reps = (LANES // 2) // half
    cos = np.tile(np.cos(ang), (1, 2 * reps))
    sin = np.tile(np.sin(ang), (1, reps))
    return jnp.asarray(cos), jnp.asarray(np.concatenate([-sin, sin], axis=1))


def _rope(x, cos, sin):
    outs = []
    for g in range(x.shape[1] // LANES):
        xg = x[:, g * LANES:(g + 1) * LANES]
        outs.append(xg * cos + pltpu.roll(xg, LANES // 2, 1) * sin)
    return outs[0] if len(outs) == 1 else jnp.concatenate(outs, axis=1)


_P0_QKV = (0, GDN_CONV_W)
_P0_Z = (_P0_QKV[1], _P0_QKV[1] + GDN_V_W)
_P0_RQ = (_P0_Z[1], _P0_Z[1] + RET_QK_W)
_P0_RK = (_P0_RQ[1], _P0_RQ[1] + RET_QK_W)
_P0_RV = (_P0_RK[1], _P0_RK[1] + MIX_W)
_P0_RG = (_P0_RV[1], _P0_RV[1] + MIX_W)
_P0_BA = (_P0_RG[1], _P0_RG[1] + LANES)
_P0_W = _P0_BA[1]


def _ret_perm():
    perm = np.zeros(RET_QK_W, np.int32)
    for h in range(N_HEADS):
        for d in range(RET_DK):
            new = (h // 2) * LANES + (d // 32) * 64 + (h % 2) * 32 + d % 32
            perm[new] = h * RET_DK + d
    return perm


def _diff_perm():
    perm = np.zeros(N_HEADS * 2 * DIFF_DK, np.int32)
    for h in range(N_HEADS):
        for c in range(2):
            for d in range(DIFF_DK):
                new = h * LANES + (d // 32) * 64 + c * 32 + d % 32
                perm[new] = h * 2 * DIFF_DK + c * DIFF_DK + d
    return perm


def _pack_w_in0(w_in):
    o = np.cumsum((0, GDN_CONV_W, GDN_V_W, N_HEADS, N_HEADS, RET_QK_W, RET_QK_W, MIX_W, MIX_W))
    qkv, z, b, a, rq, rk, rv, rg = (w_in[:, o[i]:o[i + 1]] for i in range(8))
    perm = _ret_perm()
    pad = jnp.zeros((D_MODEL, LANES - 2 * N_HEADS), w_in.dtype)
    cols = [qkv, z, rq[:, perm], rk[:, perm], rv, rg, b, a, pad]
    return jnp.concatenate(cols, axis=1).astype(BF16)


def _in0_kernel(x_ref, n_ref, w_ref, cos_ref, sin_ref,
                qkv_ref, z_ref, rq_ref, rk_ref, rv_ref, rg_ref, ba_ref):
    hb = (_rms(x_ref[...]) * n_ref[...]).astype(BF16)

    def proj(seg):
        return _dot(hb, w_ref[:, seg[0]:seg[1]])

    cos = cos_ref[...]
    sin = sin_ref[...]
    qkv_ref[...] = proj(_P0_QKV)
    z_ref[...] = proj(_P0_Z).astype(BF16)
    rq_ref[...] = (_rope(proj(_P0_RQ), cos, sin) * RET_DK ** -0.5).astype(BF16)
    rk_ref[...] = _rope(proj(_P0_RK), cos, sin).astype(BF16)
    rv_ref[...] = proj(_P0_RV).astype(BF16)
    rg_ref[...] = proj(_P0_RG).astype(BF16)
    ba_ref[...] = proj(_P0_BA)


def _in0(x2d, norm, w_packed, cos, sin, seq):
    T = x2d.shape[0]
    tm = PROJ_TM
    spt = seq // tm
    row = lambda w: pl.BlockSpec((tm, w), lambda i: (i, 0))
    tab = pl.BlockSpec((tm, LANES), lambda i: (i % spt, 0))
    widths = (GDN_CONV_W, GDN_V_W, RET_QK_W, RET_QK_W, MIX_W, MIX_W, LANES)
    dtypes = (F32, BF16, BF16, BF16, BF16, BF16, F32)
    return pl.pallas_call(
        _in0_kernel,
        grid=(T // tm,),
        in_specs=[row(D_MODEL), _const_spec((1, D_MODEL)), _const_spec((D_MODEL, _P0_W)), tab, tab],
        out_specs=[row(w) for w in widths],
        out_shape=[jax.ShapeDtypeStruct((T, w), dt) for w, dt in zip(widths, dtypes)],
        compiler_params=_params("parallel"),
        name="in_proj0",
    )(x2d, norm.reshape(1, D_MODEL), w_packed, cos, sin)


def _bmm(a, b):
    return jnp.einsum("gik,gkj->gij", a.astype(BF16), b.astype(BF16), preferred_element_type=F32)


def _unit_lower_inverse(a, ri, ci):
    n = jnp.where((ri >> 3) == (ci >> 3), -a, 0.0)
    t = jnp.where(ri == ci, 1.0, 0.0) + n
    p = _bmm(n, n)
    t = t + _bmm(t, p)
    p = _bmm(p, p)
    t = t + _bmm(t, p)
    for sh in (3, 4, 5):
        e = jnp.where(((ri >> (sh + 1)) == (ci >> (sh + 1))) & ((ri >> sh) != (ci >> sh)), a, 0.0)
        t = t - _bmm(_bmm(t, e), t)
    return t


def _gdn_kernel(qkv_ref, ba_ref, z_ref, cw_ref, alog_ref, dtb_ref, gn_ref, o_ref,
                xe_ref, o_s, state_ref):
    @pl.when(pl.program_id(1) == 0)
    def _():
        xe_ref[...] = jnp.zeros_like(xe_ref)
        state_ref[...] = jnp.zeros_like(state_ref)

    for t in range(qkv_ref.shape[0] // GDN_TS):
        rows = slice(t * GDN_TS, (t + 1) * GDN_TS)
        _gdn_tile(qkv_ref.at[rows], ba_ref.at[rows], z_ref.at[rows], cw_ref, alog_ref, dtb_ref, gn_ref,
                  o_ref.at[rows], xe_ref, o_s, state_ref)


def _gdn_tile(qkv_ref, ba_ref, z_ref, cw_ref, alog_ref, dtb_ref, gn_ref, o_ref, xe_ref, o_s, state_ref):
    ts = qkv_ref.shape[0]
    C = GDN_CHUNK
    nch = ts // C
    halo = SUBLANES

    beta = 1.0 / (1.0 + jnp.exp(-ba_ref[...]))
    xa = ba_ref[...] + dtb_ref[...]
    softplus = jnp.maximum(xa, 0.0) + jnp.log(1.0 + jnp.exp(-jnp.abs(xa)))
    g = -jnp.exp(alog_ref[...]) * softplus
    rt = lax.broadcasted_iota(jnp.int32, (ts, ts), 0)
    ct = lax.broadcasted_iota(jnp.int32, (ts, ts), 1)
    tri = jnp.where((rt >= ct) & ((rt >> 6) == (ct >> 6)), 1.0, 0.0)
    gc = jnp.dot(tri, g, precision=HIGHEST, preferred_element_type=F32)

    x = qkv_ref[...]
    xb = x.astype(BF16)
    shift = jnp.concatenate([jnp.where(rt - ct == k, 1.0, 0.0) for k in range(1, GDN_CONV)], axis=0).astype(BF16)
    shifted = _dot(shift, xb)
    prev = xe_ref[...]
    row8 = lax.broadcasted_iota(jnp.int32, (halo, 1), 0)
    conv = cw_ref[GDN_CONV - 1:GDN_CONV, :] * x
    for k in range(1, GDN_CONV):
        sh = shifted[(k - 1) * ts:k * ts]
        head = jnp.where(row8 < k, pltpu.roll(prev, k, 0), sh[0:halo])
        conv = conv + cw_ref[GDN_CONV - 1 - k:GDN_CONV - k, :] * jnp.concatenate([head, sh[halo:]], axis=0)
    xe_ref[...] = x[ts - halo:ts]
    act = _silu(conv)
    qn, kn, vv = [], [], []
    for h in range(N_HEADS):
        lo = h * GDN_DK
        qh = act[:, lo:lo + GDN_DK]
        kh = act[:, GDN_QK_W + lo:GDN_QK_W + lo + GDN_DK]
        qn.append(qh * (lax.rsqrt(jnp.sum(qh * qh, -1, keepdims=True) + EPS) * GDN_DK ** -0.5))
        kn.append(kh * lax.rsqrt(jnp.sum(kh * kh, -1, keepdims=True) + EPS))
        vv.append(act[:, 2 * GDN_QK_W + h * HEAD_V:2 * GDN_QK_W + (h + 1) * HEAD_V])

    units = [(c, h) for c in range(nch) for h in range(N_HEADS)]
    rows = lambda x, c: x[c * C:(c + 1) * C]
    stack = lambda f: jnp.stack([f(c, h) for c, h in units], axis=0)
    q = stack(lambda c, h: rows(qn[h], c))
    k = stack(lambda c, h: rows(kn[h], c))
    v = stack(lambda c, h: rows(vv[h], c))
    gct = [rows(gc, c).T for c in range(nch)]
    gcol = stack(lambda c, h: rows(gc, c)[:, N_HEADS + h:N_HEADS + h + 1])
    grow = stack(lambda c, h: gct[c][N_HEADS + h:N_HEADS + h + 1, :])
    bcol = stack(lambda c, h: rows(beta, c)[:, h:h + 1])
    glast = gcol[:, C - 1:C, :]

    ri = lax.broadcasted_iota(jnp.int32, (1, C, C), 1)
    ci = lax.broadcasted_iota(jnp.int32, (1, C, C), 2)
    causal = ri >= ci
    decay = jnp.where(causal, jnp.exp(jnp.where(causal, gcol - grow, 0.0)), 0.0)
    kb = k.astype(BF16)
    kk = jnp.einsum("gid,gjd->gij", kb, kb, preferred_element_type=F32)
    a_mat = jnp.where(ri > ci, kk * decay * bcol, 0.0)
    t_inv = _unit_lower_inverse(a_mat, ri, ci)
    eg = jnp.exp(gcol)
    rhs = jnp.concatenate([v * bcol, k * (bcol * eg)], axis=2).astype(BF16)
    sol = jnp.einsum("gik,gkj->gij", t_inv.astype(BF16), rhs, preferred_element_type=F32)
    solb = sol.astype(BF16)
    qk = (jnp.einsum("gid,gjd->gij", q.astype(BF16), kb, preferred_element_type=F32) * decay).astype(BF16)
    qk_uw = jnp.einsum("gij,gje->gie", qk, solb, preferred_element_type=F32)
    kd = k * jnp.exp(glast - gcol)
    kdt = jnp.stack([kd[n].T for n in range(len(units))], axis=0).astype(BF16)
    kd_uw = jnp.einsum("gdc,gce->gde", kdt, solb, preferred_element_type=F32)
    lhs_all = jnp.concatenate([kd_uw[:, :, HEAD_V:], q * eg - qk_uw[:, :, HEAD_V:]], axis=1).astype(BF16)
    c_all = kd_uw[:, :, :HEAD_V]
    o0_all = qk_uw[:, :, :HEAD_V]
    egl = jnp.exp(glast)

    states = [state_ref[h] for h in range(N_HEADS)]
    for n, (c, h) in enumerate(units):
        r = _dot(lhs_all[n], states[h].astype(BF16))
        o_s[c * C:(c + 1) * C, h * HEAD_V:(h + 1) * HEAD_V] = o0_all[n] + r[GDN_DK:]
        states[h] = states[h] * egl[n] + c_all[n] - r[:GDN_DK]
    for h in range(N_HEADS):
        state_ref[h] = states[h]

    for h in range(N_HEADS):
        lo = h * HEAD_V
        zh = z_ref[:, lo:lo + HEAD_V].astype(F32)
        o_ref[:, lo:lo + HEAD_V] = (_rms(o_s[:, lo:lo + HEAD_V]) * gn_ref[...] * _silu(zh)).astype(BF16)


def _gdn(qkv, ba, z, conv_w, a_log, dt_bias, gdn_norm, batch, seq):
    T = qkv.shape[0]
    ts = GDN_TS
    spt = seq // GDN_TILE
    row = lambda w: pl.BlockSpec((GDN_TILE, w), lambda bi, i: (bi * spt + i, 0))
    pad4 = lambda p: jnp.pad(p.astype(F32), (N_HEADS, LANES - 2 * N_HEADS)).reshape(1, LANES)
    return pl.pallas_call(
        _gdn_kernel,
        grid=(batch, spt),
        in_specs=[row(GDN_CONV_W), row(LANES), row(GDN_V_W),
                  _const_spec((GDN_CONV, GDN_CONV_W)), _const_spec((1, LANES)), _const_spec((1, LANES)),
                  _const_spec((1, HEAD_V))],
        out_specs=row(GDN_V_W),
        out_shape=jax.ShapeDtypeStruct((T, GDN_V_W), BF16),
        scratch_shapes=[
            pltpu.VMEM((SUBLANES, GDN_CONV_W), F32),
            pltpu.VMEM((ts, GDN_V_W), F32),
            pltpu.VMEM((N_HEADS, GDN_DK, HEAD_V), F32),
        ],
        compiler_params=_params("arbitrary", "arbitrary"),
        name="gated_deltanet",
    )(qkv, ba, z, conv_w.astype(F32), pad4(a_log), pad4(dt_bias), gdn_norm.reshape(1, HEAD_V))


def _ret_kernel(q_ref, k_ref, v_ref, g_ref, o_ref, state_ref, dm_ref):
    i = pl.program_id(1)
    c = RET_C

    @pl.when(i == 0)
    def _():
        state_ref[...] = jnp.zeros_like(state_ref)

    @pl.when((pl.program_id(0) == 0) & (i == 0))
    def _():
        ri = lax.broadcasted_iota(jnp.int32, (c, c), 0)
        ci = lax.broadcasted_iota(jnp.int32, (c, c), 1)
        rel = (ri - ci).astype(F32)
        for h in range(N_HEADS):
            lg = math.log(1.0 - 2.0 ** (-5.0 - h))
            dm_ref[h] = jnp.where(rel >= 0, jnp.exp(jnp.where(rel >= 0, rel, 0.0) * lg), 0.0)

    pos = lax.broadcasted_iota(jnp.int32, (c, 1), 0).astype(F32)
    lane = lax.broadcasted_iota(jnp.int32, (1, LANES), 1)
    for t in range(q_ref.shape[0] // c):
        rows = slice(t * c, (t + 1) * c)
        for h in range(N_HEADS):
            lg = math.log(1.0 - 2.0 ** (-5.0 - h))
            grp = (h // 2) * LANES
            own = ((lane >> 5) & 1) == h % 2
            qm = jnp.where(own, q_ref[rows, grp:grp + LANES], 0)
            km = jnp.where(own, k_ref[rows, grp:grp + LANES], 0)
            v = v_ref[rows, h * HEAD_V:(h + 1) * HEAD_V]
            intra = _dot((_dot_nt(qm, km) * dm_ref[h]).astype(BF16), v)
            state = state_ref[h]
            inter = _dot(qm, state.astype(BF16)) * jnp.exp((pos + 1.0) * lg)
            k_sc = (km.astype(F32) * jnp.exp((c - 1.0 - pos) * lg)).astype(BF16)
            state_ref[h] = state * math.exp(c * lg) + _dot_tn(k_sc, v)
            gate = _silu(g_ref[rows, h * HEAD_V:(h + 1) * HEAD_V].astype(F32))
            o_ref[rows, h * HEAD_V:(h + 1) * HEAD_V] = (_rms(intra + inter) * gate).astype(BF16)


def _retention(rq, rk, rv, rg, batch, seq):
    T = rq.shape[0]
    c = RET_C
    tr = RET_TILE
    spt = seq // tr
    row = lambda w: pl.BlockSpec((tr, w), lambda bi, i: (bi * spt + i, 0))
    return pl.pallas_call(
        _ret_kernel,
        grid=(batch, spt),
        in_specs=[row(RET_QK_W), row(RET_QK_W), row(MIX_W), row(MIX_W)],
        out_specs=row(MIX_W),
        out_shape=jax.ShapeDtypeStruct((T, MIX_W), BF16),
        scratch_shapes=[pltpu.VMEM((N_HEADS, LANES, HEAD_V), F32), pltpu.VMEM((N_HEADS, c, c), F32)],
        compiler_params=_params("arbitrary", "arbitrary"),
        name="retention",
    )(rq, rk, rv, rg)


def _pack_w_in1(w_in):
    o = np.cumsum((0,) + (MIX_W,) * 6)
    cq, ck, cv, dq, dk, dv = (w_in[:, o[i]:o[i + 1]] for i in range(6))
    perm = _diff_perm()
    w = jnp.concatenate([cq, ck, cv, dq[:, perm], dk[:, perm]], axis=1).astype(BF16)
    return w, dv.T.astype(BF16)


def _in1_kernel(x_ref, n_ref, w_ref, wvt_ref, cosa_ref, sina_ref, cosb_ref, sinb_ref,
                cq_ref, ck_ref, cv_ref, dq_ref, dk_ref, dvt_ref):
    hb = (_rms(x_ref[...]) * n_ref[...]).astype(BF16)

    def proj(j):
        return _dot(hb, w_ref[:, j * MIX_W:(j + 1) * MIX_W])

    cq_ref[...] = _rope(proj(0), cosa_ref[...], sina_ref[...]) * (DIL_DH ** -0.5 * LOG2E)
    ck_ref[...] = _rope(proj(1), cosa_ref[...], sina_ref[...])
    cv_ref[...] = proj(2)
    dq_ref[...] = (_rope(proj(3), cosb_ref[...], sinb_ref[...]) * (DIFF_DK ** -0.5 * LOG2E)).astype(BF16)
    dk_ref[...] = _rope(proj(4), cosb_ref[...], sinb_ref[...]).astype(BF16)
    vt = _dot_nt(wvt_ref[...], hb).astype(BF16)
    for j in range(dvt_ref.shape[0]):
        dvt_ref[j] = vt[:, j * DIFF_TK:(j + 1) * DIFF_TK]


def _in1(x2d, norm, w_packed, wv_t, cosa, sina, cosb, sinb, seq):
    T = x2d.shape[0]
    tm = PROJ_TM
    spt = seq // tm
    vblk = tm // DIFF_TK
    row = lambda w: pl.BlockSpec((tm, w), lambda i: (i, 0))
    tab = pl.BlockSpec((tm, LANES), lambda i: (i % spt, 0))
    out_shape = [jax.ShapeDtypeStruct((T, MIX_W), dt) for dt in (F32, F32, F32, BF16, BF16)]
    out_shape.append(jax.ShapeDtypeStruct((T // DIFF_TK, MIX_W, DIFF_TK), BF16))
    return pl.pallas_call(
        _in1_kernel,
        grid=(T // tm,),
        in_specs=[row(D_MODEL), _const_spec((1, D_MODEL)), _const_spec((D_MODEL, 5 * MIX_W)),
                  _const_spec((MIX_W, D_MODEL)), tab, tab, tab, tab],
        out_specs=[row(MIX_W)] * 5 + [pl.BlockSpec((vblk, MIX_W, DIFF_TK), lambda i: (i, 0, 0))],
        out_shape=out_shape,
        compiler_params=_params("parallel"),
        name="in_proj1",
    )(x2d, norm.reshape(1, D_MODEL), w_packed, wv_t, cosa, sina, cosb, sinb)


def _dil_kernel(q_ref, kp_ref, kc_ref, vp_ref, vc_ref, o_ref, ob, mb, lb):
    i = pl.program_id(2)
    blk = q_ref.shape[0]
    nk = DIL_KEYS

    qi = lax.broadcasted_iota(jnp.int32, (nk, 2 * nk), 0)
    kj = lax.broadcasted_iota(jnp.int32, (nk, 2 * nk), 1)
    dist = qi + nk - kj
    band = (dist >= 0) & (dist <= nk)
    band_first = band & (kj >= jnp.where(i > 0, 0, nk))
    ones_v = jnp.ones((2 * nk, LANES), BF16)

    def window(prev_ref, cur_ref, start, d):
        if start >= blk:
            return cur_ref[pl.ds(start - blk, 2 * nk, stride=d), :]
        return jnp.concatenate([prev_ref[pl.ds(start, nk, stride=d), :],
                                cur_ref[pl.ds(start + nk * d - blk, nk, stride=d), :]], axis=0)

    for gi, (_, d) in enumerate(DIL_PATTERNS):
        span = nk * d
        pos = [((n // d) * span + n % d, n // d) for n in range(blk // nk)]
        qs_ = [q_ref[pl.ds(qs, nk, stride=d), :].astype(BF16) for qs, _ in pos]
        ks_ = [window(kp_ref, kc_ref, qs + blk - span, d).astype(BF16) for qs, _ in pos]
        vs_ = [jnp.concatenate([window(vp_ref, vc_ref, qs + blk - span, d).astype(BF16), ones_v], axis=1)
               for qs, _ in pos]
        ss = [_dot_nt(q, k) for q, k in zip(qs_, ks_)]
        ps, ms = [], []
        for s, (_, sub) in zip(ss, pos):
            sb = jnp.where(band if sub > 0 else band_first, s.astype(BF16), -jnp.inf)
            m = jnp.max(sb, axis=-1, keepdims=True)
            ps.append(jnp.exp2(sb - m))
            ms.append(m.astype(F32))
        os_ = [_dot(p, v) for p, v in zip(ps, vs_)]
        for (qs, _), o, m in zip(pos, os_, ms):
            ob[gi, pl.ds(qs, nk, stride=d), :] = o[:, :HEAD_V]
            lb[gi, pl.ds(qs, nk, stride=d), :] = o[:, HEAD_V:]
            mb[gi, pl.ds(qs, nk, stride=d), :] = jnp.broadcast_to(m, (nk, LANES))

    m_all = jnp.maximum(jnp.maximum(mb[0], mb[1]), mb[2])
    num = jnp.zeros((blk, HEAD_V), F32)
    den = jnp.zeros((blk, LANES), F32)
    for gi in range(len(DIL_PATTERNS)):
        sc = jnp.exp2(mb[gi] - m_all)
        num = num + ob[gi] * sc
        den = den + lb[gi] * sc
    o_ref[...] = (num / den).astype(BF16)


def _dilated(cq, ck, cv, batch, seq):
    blk = DIL_BLK
    q3, k3, v3 = (t.reshape(batch, seq, MIX_W) for t in (cq, ck, cv))
    cur = pl.BlockSpec((None, blk, DIL_DH), lambda b, h, i: (b, i, h))
    prev = pl.BlockSpec((None, blk, DIL_DH), lambda b, h, i: (b, jnp.maximum(i - 1, 0), h))
    nb = len(DIL_PATTERNS)
    out = pl.pallas_call(
        _dil_kernel,
        grid=(batch, N_HEADS, seq // blk),
        in_specs=[cur, prev, cur, prev, cur],
        out_specs=cur,
        out_shape=jax.ShapeDtypeStruct((batch, seq, MIX_W), BF16),
        scratch_shapes=[
            pltpu.VMEM((nb, blk, HEAD_V), F32),
            pltpu.VMEM((nb, blk, LANES), F32),
            pltpu.VMEM((nb, blk, LANES), F32),
        ],
        compiler_params=_params("parallel", "parallel", "arbitrary"),
        name="dilated_attention",
    )(q3, k3, k3, v3, v3)
    return out.reshape(batch * seq, MIX_W)


def _diff_kernel(q_ref, k_ref, vt_ref, lq1_ref, lk1_ref, lq2_ref, lk2_ref, dn_ref, o_ref,
                 *stats, lambda_init):
    i = pl.program_id(2)
    tq = q_ref.shape[0]
    tk = DIFF_TK
    assert tq == tk, "the diagonal handling assumes square tiles"
    gw = DIFF_GW
    ngrp = 2 * tq // gw
    nst = DIFF_HPS * ngrp
    m_s, acc_s = stats[0:nst], stats[nst:2 * nst]
    lane = lax.broadcasted_iota(jnp.int32, (1, LANES), 1)
    first = ((lane >> 5) & 1) == 0
    qgs = []
    for hh in range(DIFF_HPS):
        q = q_ref[:, hh * LANES:(hh + 1) * LANES]
        qq = jnp.concatenate([jnp.where(first, q, 0), jnp.where(first, 0, q)], axis=0)
        qgs.append([qq[g * gw:(g + 1) * gw] for g in range(ngrp)])
    for r in m_s:
        r[...] = jnp.full_like(r, -jnp.inf)
    for r in acc_s:
        r[...] = jnp.zeros_like(r)

    def scores(kt, c0, nr, hh, groups):
        k0 = pl.multiple_of(kt * tk, tk) + c0
        k_rows = k_ref[pl.ds(k0, nr), hh * LANES:(hh + 1) * LANES]
        return [_dot_nt(k_rows, qgs[hh][g]) for g in groups]

    def update(ss, kt, c0, nr, hh, groups, tri):
        vt = jnp.concatenate([vt_ref[kt, hh * HEAD_V:(hh + 1) * HEAD_V, c0:c0 + nr],
                              jnp.ones((SUBLANES, nr), BF16)], axis=0)
        ps, alphas = [], []
        for g, s in zip(groups, ss):
            n = hh * ngrp + g
            sb = s.astype(BF16)
            if g in tri:
                key = lax.broadcasted_iota(jnp.int32, (nr, gw), 0)
                qry = lax.broadcasted_iota(jnp.int32, (nr, gw), 1)
                sb = jnp.where(qry >= key, sb, -jnp.inf)
            m_prev = m_s[n][...]
            m_new = jnp.maximum(m_prev, jnp.max(sb, axis=0, keepdims=True).astype(F32))
            alphas.append(jnp.exp2(m_prev - m_new))
            ps.append(jnp.exp2(sb - m_new.astype(BF16)))
            m_s[n][...] = m_new
        pvs = [_dot(vt, p) for p in ps]
        for g, alpha, pv in zip(groups, alphas, pvs):
            n = hh * ngrp + g
            acc_s[n][...] = alpha * acc_s[n][...] + pv

    def run(kt, units):
        sss = [scores(kt, c0, nr, hh, groups) for c0, nr, hh, groups, _ in units]
        for ss, (c0, nr, hh, groups, tri) in zip(sss, units):
            update(ss, kt, c0, nr, hh, groups, tri)

    every = list(range(ngrp))
    full = [(0, tk, hh, every, ()) for hh in range(DIFF_HPS)]

    def body(kt, carry):
        run(kt, full)
        return carry

    lax.fori_loop(0, i, body, 0)

    diagonal = []
    for sub in range(tk // gw):
        keep = [g for g in every if (g * gw) % tq >= sub * gw]
        on_diag = tuple(g for g in keep if (g * gw) % tq == sub * gw)
        diagonal += [(sub * gw, gw, hh, keep, on_diag) for hh in range(DIFF_HPS)]
    run(i, diagonal)

    lam = (jnp.exp(jnp.sum(lq1_ref[...] * lk1_ref[...], axis=-1, keepdims=True))
           - jnp.exp(jnp.sum(lq2_ref[...] * lk2_ref[...], axis=-1, keepdims=True)) + lambda_init)
    half = ngrp // 2
    for hh in range(DIFF_HPS):
        for g in range(half):
            n1, n2 = hh * ngrp + g, hh * ngrp + g + half
            a1, a2 = acc_s[n1][...], acc_s[n2][...]
            o = a1[:HEAD_V] / a1[HEAD_V:HEAD_V + 1] - lam * (a2[:HEAD_V] / a2[HEAD_V:HEAD_V + 1])
            o = o * lax.rsqrt(jnp.mean(o * o, axis=0, keepdims=True) + EPS) * dn_ref[...] * (1.0 - lambda_init)
            o_ref[g * gw:(g + 1) * gw, hh * HEAD_V:(hh + 1) * HEAD_V] = o.T.astype(BF16)


def _differential(dq, dk, dvt, lq1, lk1, lq2, lk2, diff_norm, lambda_init, batch, seq):
    tq = DIFF_TQ
    nkt = seq // DIFF_TK
    nst = DIFF_HPS * 2 * tq // DIFF_GW
    hw = DIFF_HPS * LANES
    q3, k3 = (t.reshape(batch, seq, MIX_W) for t in (dq, dk))
    qspec = pl.BlockSpec((None, tq, hw), lambda b, h, i: (b, i, h))
    kspec = pl.BlockSpec((None, seq, hw), lambda b, h, i: (b, 0, h))
    vspec = pl.BlockSpec((nkt, hw, DIFF_TK), lambda b, h, i: (b, h, 0))
    vec = lambda p: p.reshape(1, DIFF_DK).astype(F32)
    out = pl.pallas_call(
        functools.partial(_diff_kernel, lambda_init=lambda_init),
        grid=(batch, N_HEADS // DIFF_HPS, seq // tq),
        in_specs=[qspec, kspec, vspec] + [_const_spec((1, DIFF_DK))] * 4 + [_const_spec((HEAD_V, 1))],
        out_specs=qspec,
        out_shape=jax.ShapeDtypeStruct((batch, seq, MIX_W), BF16),
        scratch_shapes=([pltpu.VMEM((1, DIFF_GW), F32)] * nst
                        + [pltpu.VMEM((HEAD_V + SUBLANES, DIFF_GW), F32)] * nst),
        compiler_params=_params("parallel", "parallel", "arbitrary"),
        name="differential_attention",
    )(q3, k3, dvt, vec(lq1), vec(lk1), vec(lq2), vec(lk2), diff_norm.reshape(HEAD_V, 1))
    return out.reshape(batch * seq, MIX_W)


def kernel(x, l0_ffn1_norm, l0_ffn1_w_up, l0_ffn1_w_down, l0_mix_norm, l0_w_in, l0_conv_w, l0_a_log,
           l0_dt_bias, l0_gdn_norm, l0_w_out, l0_ffn2_norm, l0_ffn2_w_up, l0_ffn2_w_down,
           l1_ffn1_norm, l1_ffn1_w_up, l1_ffn1_w_down, l1_mix_norm, l1_w_in, l1_lambda_q1, l1_lambda_k1,
           l1_lambda_q2, l1_lambda_k2, l1_diff_norm, l1_w_out, l1_ffn2_norm, l1_ffn2_w_up, l1_ffn2_w_down,
           final_norm):
    batch, seq, _ = x.shape
    cos64, sin64 = _rope_tables(seq, 64)
    cos128, sin128 = _rope_tables(seq, 128)
    xf = x.reshape(batch * seq, D_MODEL)

    xf = _ffn(xf, l0_ffn1_norm, l0_ffn1_w_up, l0_ffn1_w_down)
    qkv, z, rq, rk, rv, rg, ba = _in0(xf, l0_mix_norm, _pack_w_in0(l0_w_in), cos64, sin64, seq)
    o_a = _gdn(qkv, ba, z, l0_conv_w, l0_a_log, l0_dt_bias, l0_gdn_norm, batch, seq)
    o_b = _retention(rq, rk, rv, rg, batch, seq)
    xf = _ffn(xf, l0_ffn2_norm, l0_ffn2_w_up, l0_ffn2_w_down, mix=(o_a, o_b, l0_w_out))

    xf = _ffn(xf, l1_ffn1_norm, l1_ffn1_w_up, l1_ffn1_w_down)
    w1, wv_t = _pack_w_in1(l1_w_in)
    cq, ck, cv, dq, dk, dvt = _in1(xf, l1_mix_norm, w1, wv_t, cos128, sin128, cos64, sin64, seq)
    o_c = _dilated(cq, ck, cv, batch, seq)
    lambda_init = 0.8 - 0.6 * math.exp(-0.3 * 1)
    o_d = _differential(dq, dk, dvt, l1_lambda_q1, l1_lambda_k1, l1_lambda_q2, l1_lambda_k2,
                        l1_diff_norm, lambda_init, batch, seq)
    xf = _ffn(xf, l1_ffn2_norm, l1_ffn2_w_up, l1_ffn2_w_down, mix=(o_c, o_d, l1_w_out), final_norm=final_norm)
    return xf.reshape(batch, seq, D_MODEL)
```
